```python
import math
import jax, jax.numpy as jnp
from jax import lax
import numpy as np

D_MODEL = 2048
BATCH = 2
SEQ = 8192
DEPTH = 2

GRID_W = 64
CTX_LEN = 256
N_EVEN = (DEPTH + 1) // 2
N_ODD = DEPTH // 2
N_MOD = 6
EPS = 1e-6

HG_WIDTH = D_MODEL // 2
HG_HEAD_DIM = 128
HG_HEADS = HG_WIDTH // HG_HEAD_DIM
HG_CHUNK = 64
S5_WIDTH = D_MODEL - HG_WIDTH
S5_GROUP = 16
S5_GROUPS = S5_WIDTH // S5_GROUP
S5_STATE = 64
EVEN_IN = 5 * HG_WIDTH + S5_WIDTH

DN_QK_HEADS = 16
DN_V_HEADS = 32
DN_HEAD_DIM = 128
DN_QK = DN_QK_HEADS * DN_HEAD_DIM
DN_V = DN_V_HEADS * DN_HEAD_DIM
DN_QKV = 2 * DN_QK + DN_V
DN_CONV = 3
DN_CHUNK = 64
ODD_IN = DN_QKV + DN_V + 4 * DN_V_HEADS

N_EXPERTS = 16
EXPERT_FF = 1024
EC_CAPACITY = 2

kernel_name = 'hybrid_hgrn2_s5_gdn_ec_moe_dit'


def rmsnorm(x, g):
    xf = x.astype(jnp.float32)
    y = xf * lax.rsqrt(jnp.mean(xf * xf, axis=-1, keepdims=True) + EPS)
    return (y * g.astype(jnp.float32)).astype(x.dtype)


def modulate(h, shift, scale):
    return h * (1 + scale) + shift


def l2norm(a):
    return a * lax.rsqrt(jnp.sum(a * a, axis=-1, keepdims=True) + EPS)


def to_bwd(a, L):
    return jnp.concatenate([jnp.flip(a[:, :L], axis=1), jnp.flip(a[:, L:], axis=1)], axis=1)


def identity(a):
    return a


def hgrn2_chunk(q, k, v, log_f):
    B, T, H, K = q.shape
    V = v.shape[-1]
    C = HG_CHUNK
    n = T // C
    blk = lambda a: jnp.moveaxis(a.reshape(B, n, C, H, a.shape[-1]), 3, 1)
    q, k, v, log_f = blk(q), blk(k), blk(v), blk(log_f)
    b = jnp.cumsum(log_f, axis=3)
    ref = b[:, :, :, C // 2 - 1:C // 2]
    att = jnp.einsum('bhntk,bhnsk->bhnts', q * jnp.exp(b - ref), k * jnp.exp(ref - b))
    att = jnp.where(jnp.tril(jnp.ones((C, C), bool)), att, 0.0)
    b_last = b[:, :, :, -1]
    kv = jnp.einsum('bhnsk,bhnsv->bhnkv', k * jnp.exp(b_last[:, :, :, None] - b), v)

    def step(S, xs):
        dec, kv_c = xs
        return S * dec[..., None] + kv_c, S

    _, s_start = lax.scan(step, jnp.zeros((B, H, K, V), jnp.float32),
                          (jnp.moveaxis(jnp.exp(b_last), 2, 0), jnp.moveaxis(kv, 2, 0)))
    s_start = jnp.moveaxis(s_start, 0, 2)
    o = (jnp.einsum('bhnts,bhnsv->bhntv', att, v)
         + jnp.einsum('bhntk,bhnkv->bhntv', q * jnp.exp(b), s_start))
    return jnp.moveaxis(o, 1, 3).reshape(B, T, H, V)


def hgrn2_mixer(p, L, lb, norm_g):
    B, T, _ = p.shape
    heads = lambda a: a.astype(jnp.float32).reshape(B, T, HG_HEADS, HG_HEAD_DIM)
    q, f_f, f_b, i, g = jnp.split(p, 5, axis=-1)
    q, i = heads(q), heads(i)

    def direction(fz, lb_d, reorder):
        lbh = lb_d.astype(jnp.float32).reshape(HG_HEADS, HG_HEAD_DIM)
        f = lbh + (1.0 - lbh) * jax.nn.sigmoid(heads(fz))
        y = hgrn2_chunk(reorder(q), reorder(1.0 - f), reorder(i), reorder(jnp.log(f)))
        return reorder(y)

    o = direction(f_f, lb[0], identity) + direction(f_b, lb[1], lambda a: to_bwd(a, L))
    o = rmsnorm(o, norm_g) * jax.nn.silu(heads(g))
    return o.reshape(B, T, HG_WIDTH).astype(p.dtype)


def s5_combine(e1, e2):
    a1, b1 = e1
    a2, b2 = e2
    return a1 * a2, a2 * b1 + b2


def s5_mixer(u, L, a_re, a_im, log_dt, b_re, b_im, c_re, c_im, d_skip, glu_w, glu_b):
    B, T, _ = u.shape
    f32 = jnp.float32
    uf = u.astype(f32).reshape(B, T, S5_GROUPS, S5_GROUP)

    def direction(dd, reorder):
        A = lax.complex(a_re[dd].astype(f32), a_im[dd].astype(f32))
        dt = jnp.exp(log_dt[dd].astype(f32))[:, None]
        A_bar = jnp.exp(A * dt)
        B_bar = ((A_bar - 1.0) / A)[..., None] * lax.complex(b_re[dd].astype(f32), b_im[dd].astype(f32))
        bu = jnp.einsum('btgc,gpc->btgp', reorder(uf).astype(jnp.complex64), B_bar)
        a_seq = jnp.broadcast_to(A_bar, (1, T, S5_GROUPS, S5_STATE))
        _, states = lax.associative_scan(s5_combine, (a_seq, bu), axis=1)
        Cm = lax.complex(c_re[dd].astype(f32), c_im[dd].astype(f32))
        y = jnp.einsum('btgp,gcp->btgc', states, Cm).real
        return reorder(y)

    y = direction(0, identity) + direction(1, lambda a: to_bwd(a, L)) + d_skip.astype(f32) * uf
    y = jax.nn.gelu(y)
    y = y * jax.nn.sigmoid(jnp.einsum('btgc,gce->btge', y, glu_w.astype(f32)) + glu_b.astype(f32))
    return y.reshape(B, T, S5_WIDTH).astype(u.dtype)


def short_conv(a, w, rows):
    B, T, Ch = a.shape
    img = a.reshape(B, rows, T // rows, Ch)
    y = lax.conv_general_dilated(img, w[:, :, None, :].astype(a.dtype), window_strides=(1, 1),
                                 padding='SAME', dimension_numbers=('NHWC', 'HWIO', 'NHWC'),
                                 feature_group_count=Ch)
    return y.reshape(B, T, Ch)


def delta_chunk(q, k, v, g, beta):
    B, T, H, K = q.shape
    V = v.shape[-1]
    C = DN_CHUNK
    n = T // C
    blk = lambda a: jnp.moveaxis(a.reshape((B, n, C, H) + a.shape[3:]), 3, 1)
    q = blk(q) * K ** -0.5
    k, v, g, beta = blk(k), blk(v), blk(g), blk(beta)
    gc = jnp.cumsum(g, axis=-1)
    incl = jnp.tril(jnp.ones((C, C), bool))
    strict = jnp.tril(jnp.ones((C, C), bool), -1)
    diff = gc[..., :, None] - gc[..., None, :]
    decay = jnp.where(incl, jnp.exp(jnp.where(incl, diff, 0.0)), 0.0)
    kb = k * beta[..., None]
    a_mat = jnp.where(strict, jnp.einsum('bhnik,bhnjk->bhnij', kb, k) * decay, 0.0)
    eye = jnp.eye(C, dtype=jnp.float32)
    t_mat = lax.linalg.triangular_solve(eye + a_mat, jnp.broadcast_to(eye, a_mat.shape),
                                        left_side=True, lower=True)
    u = t_mat @ (v * beta[..., None])
    w = t_mat @ (kb * jnp.exp(gc)[..., None])
    qk = jnp.einsum('bhnik,bhnjk->bhnij', q, k) * decay
    q_dec = q * jnp.exp(gc)[..., None]
    k_dec = k * jnp.exp(gc[..., -1:] - gc)[..., None]
    g_last = jnp.exp(gc[..., -1])

    def step(S, xs):
        u_c, w_c, qk_c, qd_c, kd_c, gl_c = xs
        v_new = u_c - jnp.einsum('bhck,bhkv->bhcv', w_c, S)
        o = jnp.einsum('bhck,bhkv->bhcv', qd_c, S) + jnp.einsum('bhcs,bhsv->bhcv', qk_c, v_new)
        S = S * gl_c[..., None, None] + jnp.einsum('bhck,bhcv->bhkv', kd_c, v_new)
        return S, o

    xs = tuple(jnp.moveaxis(a, 2, 0) for a in (u, w, qk, q_dec, k_dec, g_last))
    _, o = lax.scan(step, jnp.zeros((B, H, K, V), jnp.float32), xs)
    return jnp.transpose(o, (1, 0, 3, 2, 4)).reshape(B, T, H, V)


def deltanet_mixer(h_all, L, rows, w_in, conv_w, a_log, dt_bias, norm_g, w_out):
    f32 = jnp.float32
    p = h_all @ w_in
    B, T, _ = p.shape
    conv_in = p[..., :DN_QKV]
    qkv = jax.nn.silu(jnp.concatenate([short_conv(conv_in[:, :L], conv_w, 1),
                                       short_conv(conv_in[:, L:], conv_w, rows)], axis=1)).astype(f32)
    q, k, v = jnp.split(qkv, [DN_QK, 2 * DN_QK], axis=-1)
    Hv = DN_V_HEADS
    z, b_f, b_b, a_f, a_b = jnp.split(p[..., DN_QKV:].astype(f32),
                                      [DN_V, DN_V + Hv, DN_V + 2 * Hv, DN_V + 3 * Hv], axis=-1)
    rep = DN_V_HEADS // DN_QK_HEADS
    q = jnp.repeat(l2norm(q.reshape(B, T, DN_QK_HEADS, DN_HEAD_DIM)), rep, axis=2)
    k = jnp.repeat(l2norm(k.reshape(B, T, DN_QK_HEADS, DN_HEAD_DIM)), rep, axis=2)
    v = v.reshape(B, T, DN_V_HEADS, DN_HEAD_DIM)

    def direction(dd, b_z, a_z, reorder):
        beta = jax.nn.sigmoid(b_z)
        g = -jnp.exp(a_log[dd].astype(f32)) * jax.nn.softplus(a_z + dt_bias[dd].astype(f32))
        return reorder(delta_chunk(reorder(q), reorder(k), reorder(v), reorder(g), reorder(beta)))

    o = direction(0, b_f, a_f, identity) + direction(1, b_b, a_b, lambda a: to_bwd(a, L))
    o = rmsnorm(o, norm_g) * jax.nn.silu(z.reshape(B, T, DN_V_HEADS, DN_HEAD_DIM))
    return o.reshape(B, T, DN_V).astype(h_all.dtype) @ w_out


def ec_moe(h, router_w, w_gate, w_up, w_down):
    B, n, D = h.shape
    cap = EC_CAPACITY * n // N_EXPERTS
    aff = jax.nn.softmax(jnp.einsum('bnd,de->bne', h, router_w).astype(jnp.float32), axis=-1)
    gate, idx = lax.top_k(jnp.swapaxes(aff, 1, 2), cap)
    xs = jax.vmap(lambda hb, ib: hb[ib])(h, idx)
    hid = jax.nn.silu(jnp.einsum('becd,edf->becf', xs, w_gate)) * jnp.einsum('becd,edf->becf', xs, w_up)
    y = jnp.einsum('becf,efd->becd', hid, w_down) * gate[..., None].astype(h.dtype)
    return jax.vmap(lambda ib, yb: jnp.zeros((n, D), yb.dtype).at[ib.reshape(-1)].add(yb.reshape(-1, D)))(idx, y)


def setup_inputs(seed: int = 0) -> dict:
    key = jax.random.key(seed)
    ks = iter(jax.random.split(key, 48))
    nrm = lambda shape, scale: jax.random.normal(next(ks), shape, jnp.float32) * scale
    D, G, P, Cg = D_MODEL, S5_GROUPS, S5_STATE, S5_GROUP
    n_idx = jnp.arange(P, dtype=jnp.float32)
    return {
        'x': nrm((BATCH, SEQ, D), 1.0),
        'c': nrm((BATCH, D), 1.0),
        'ctx': nrm((BATCH, CTX_LEN, D), 1.0),
        'c_ctx': nrm((D,), 1.0),
        'mod_w': nrm((DEPTH, D, N_MOD * D), 0.5 * D ** -0.5),
        'mod_b': nrm((DEPTH, N_MOD * D), 0.02),
        'norm_g': 1.0 + nrm((DEPTH, 4, D), 0.02),
        'ev_w_in': nrm((N_EVEN, D, EVEN_IN), D ** -0.5),
        'ev_w_out': nrm((N_EVEN, HG_WIDTH + S5_WIDTH, D), (HG_WIDTH + S5_WIDTH) ** -0.5),
        'hg_lb': 1.0 + nrm((N_EVEN + 1, 2, HG_WIDTH), 0.1),
        'hg_norm_g': 1.0 + nrm((N_EVEN, HG_HEAD_DIM), 0.02),
        's5_a_re': -0.5 + nrm((N_EVEN, 2, G, P), 0.01),
        's5_a_im': math.pi * n_idx + nrm((N_EVEN, 2, G, P), 0.01),
        's5_log_dt': jax.random.uniform(next(ks), (N_EVEN, 2, G), jnp.float32, math.log(1e-3), math.log(1e-1)),
        's5_b_re': nrm((N_EVEN, 2, G, P, Cg), (2 * Cg) ** -0.5),
        's5_b_im': nrm((N_EVEN, 2, G, P, Cg), (2 * Cg) ** -0.5),
        's5_c_re': nrm((N_EVEN, 2, G, Cg, P), (2 * P) ** -0.5),
        's5_c_im': nrm((N_EVEN, 2, G, Cg, P), (2 * P) ** -0.5),
        's5_d': nrm((N_EVEN, G, Cg), 1.0),
        's5_glu_w': nrm((N_EVEN, G, Cg, Cg), Cg ** -0.5),
        's5_glu_b': nrm((N_EVEN, G, Cg), 0.02),
        'od_w_in': nrm((N_ODD, D, ODD_IN), D ** -0.5),
        'od_conv_w': nrm((N_ODD, DN_CONV, DN_CONV, DN_QKV), 1.0 / DN_CONV),
        'od_a_log': jnp.log(jax.random.uniform(next(ks), (N_ODD, 2, DN_V_HEADS), jnp.float32, 1.0, 16.0)),
        'od_dt_bias': 1.0 + nrm((N_ODD, 2, DN_V_HEADS), 0.1),
        'od_norm_g': 1.0 + nrm((N_ODD, DN_HEAD_DIM), 0.02),
        'od_w_out': nrm((N_ODD, DN_V, D), DN_V ** -0.5),
        'router_w': nrm((DEPTH, D, N_EXPERTS), D ** -0.5),
        'moe_w_gate': nrm((DEPTH, N_EXPERTS, D, EXPERT_FF), D ** -0.5),
        'moe_w_up': nrm((DEPTH, N_EXPERTS, D, EXPERT_FF), D ** -0.5),
        'moe_w_down': nrm((DEPTH, N_EXPERTS, EXPERT_FF, D), EXPERT_FF ** -0.5),
    }


def reference(x, c, ctx, c_ctx, mod_w, mod_b, norm_g, ev_w_in, ev_w_out, hg_lb, hg_norm_g,
              s5_a_re, s5_a_im, s5_log_dt, s5_b_re, s5_b_im, s5_c_re, s5_c_im, s5_d, s5_glu_w, s5_glu_b,
              od_w_in, od_conv_w, od_a_log, od_dt_bias, od_norm_g, od_w_out,
              router_w, moe_w_gate, moe_w_up, moe_w_down):
    L = ctx.shape[1]
    N = x.shape[1]
    rows = N // GRID_W
    lb_all = jnp.cumsum(jax.nn.softmax(hg_lb.astype(jnp.float32), axis=0), axis=0)
    sc = jax.nn.silu(c)
    scc = jax.nn.silu(c_ctx)
    for layer in range(DEPTH):
        last = layer == DEPTH - 1
        j = layer // 2
        m_lat = jnp.split((sc @ mod_w[layer] + mod_b[layer])[:, None, :], N_MOD, axis=-1)
        m_ctx = jnp.split(scc @ mod_w[layer] + mod_b[layer], N_MOD, axis=-1)
        g_pre_mix, g_post_mix, g_pre_ffn, g_post_ffn = norm_g[layer]
        h_all = jnp.concatenate([modulate(rmsnorm(ctx, g_pre_mix), m_ctx[0], m_ctx[1]),
                                 modulate(rmsnorm(x, g_pre_mix), m_lat[0], m_lat[1])], axis=1)
        if layer % 2 == 0:
            p = h_all @ ev_w_in[j]
            o_hg = hgrn2_mixer(p[..., :5 * HG_WIDTH], L, lb_all[j], hg_norm_g[j])
            o_s5 = s5_mixer(p[..., 5 * HG_WIDTH:], L, s5_a_re[j], s5_a_im[j], s5_log_dt[j], s5_b_re[j],
                            s5_b_im[j], s5_c_re[j], s5_c_im[j], s5_d[j], s5_glu_w[j], s5_glu_b[j])
            o = jnp.concatenate([o_hg, o_s5], axis=-1) @ ev_w_out[j]
        else:
            o = deltanet_mixer(h_all, L, rows, od_w_in[j], od_conv_w[j], od_a_log[j], od_dt_bias[j],
                               od_norm_g[j], od_w_out[j])
        o = rmsnorm(o, g_post_mix)
        x = x + m_lat[2] * o[:, L:]
        if not last:
            ctx = ctx + m_ctx[2] * o[:, :L]
        h = modulate(rmsnorm(x, g_pre_ffn), m_lat[3], m_lat[4])
        x = x + m_lat[5] * rmsnorm(ec_moe(h, router_w[layer], moe_w_gate[layer], moe_w_up[layer], moe_w_down[layer]), g_post_ffn)
        if not last:
            hc = modulate(rmsnorm(ctx, g_pre_ffn), m_ctx[3], m_ctx[4])
            ctx = ctx + m_ctx[5] * rmsnorm(ec_moe(hc, router_w[layer], moe_w_gate[layer], moe_w_up[layer], moe_w_down[layer]), g_post_ffn)
    return x
```

```python
import functools
import math

import jax
import jax.numpy as jnp
from jax import lax
from jax.experimental import pallas as pl
from jax.experimental.pallas import tpu as pltpu

D_MODEL = 2048
DEPTH = 2
GRID_W = 64
N_MOD = 6
EPS = 1e-6

HG_WIDTH = D_MODEL // 2
HG_HEAD_DIM = 128
HG_HEADS = HG_WIDTH // HG_HEAD_DIM
HG_CHUNK = 64
S5_WIDTH = D_MODEL - HG_WIDTH
S5_GROUP = 16
S5_GROUPS = S5_WIDTH // S5_GROUP
S5_STATE = 64

DN_QK_HEADS = 16
DN_V_HEADS = 32
DN_HEAD_DIM = 128
DN_QK = DN_QK_HEADS * DN_HEAD_DIM
DN_V = DN_V_HEADS * DN_HEAD_DIM
DN_QKV = 2 * DN_QK + DN_V
DN_CHUNK = 64

N_EXPERTS = 16
EC_CAPACITY = 2

LANES = 128
VMEM_LIMIT = 56 * 1024 * 1024

S5_LC = 16
S5_TILES = S5_WIDTH // LANES
S5_GPT = LANES // S5_GROUP
S5_ST = S5_GPT * S5_STATE
S5_CAT = S5_LC * LANES
S5_NSPLIT = 2

F32 = jnp.float32
BF16 = jnp.bfloat16


def _s5_weights(a_re, a_im, log_dt, b_re, b_im, c_re, c_im, d_skip, glu_w, glu_b):
    G, P, C, LC, NT, GT = S5_GROUPS, S5_STATE, S5_GROUP, S5_LC, S5_TILES, S5_GPT
    A = lax.complex(a_re.astype(F32), a_im.astype(F32))
    dt = jnp.exp(log_dt.astype(F32))[..., None]
    adt = A * dt
    steps = jnp.arange(LC, dtype=F32)

    def apow(e):
        return jnp.exp(adt[None] * e[:, None, None, None])

    a_bar = jnp.exp(adt)
    bb = ((a_bar - 1.0) / A)[..., None] * lax.complex(b_re.astype(F32), b_im.astype(F32))
    cm = lax.complex(c_re.astype(F32), c_im.astype(F32))
    eye = jnp.eye(GT, dtype=F32)
    hi = lax.Precision.HIGHEST

    cp = cm[None] * apow(steps)[:, :, :, None, :]
    kk = (jnp.einsum('tdgcp,dgpk->dtgck', jnp.real(cp), jnp.real(bb), precision=hi)
          - jnp.einsum('tdgcp,dgpk->dtgck', jnp.imag(cp), jnp.imag(bb), precision=hi))
    s_in = jnp.arange(LC)[:, None, None]
    s_out = jnp.arange(LC)[None, :, None]
    lag = jnp.arange(LC)[None, None, :]
    sel_f = (s_out - s_in == lag).astype(F32)
    sel_b = (s_in - s_out == lag).astype(F32)
    toe = (jnp.einsum('abt,tgok->abgok', sel_f, kk[0], precision=hi)
           + jnp.einsum('abt,tgok->abgok', sel_b, kk[1], precision=hi))
    toe = toe.reshape(LC, LC, NT, GT, C, C).transpose(2, 0, 3, 5, 1, 4)
    toe = toe[:, :, :, :, :, None, :] * eye[None, None, :, None, None, :, None]
    toe = toe.reshape(NT, S5_CAT, S5_CAT)

    def inject(e):
        er = jnp.stack([jnp.real(e), jnp.imag(e)], 0)
        er = er.reshape(2, LC, NT, GT, P, C).transpose(2, 1, 3, 5, 0, 4)
        m = er[:, :, :, :, :, None, :] * eye[None, None, :, None, None, :, None]
        return m.reshape(NT, S5_CAT, 2 * S5_ST)

    inj_f = inject(apow(LC - 1 - steps)[:, 0][..., None] * bb[0][None])
    inj_b = inject(apow(steps)[:, 1][..., None] * bb[1][None])

    def readout(w):
        wr = jnp.stack([jnp.real(w), -jnp.imag(w)], 0)
        wr = wr.reshape(2, LC, NT, GT, C, P).transpose(2, 0, 3, 5, 1, 4)
        n = wr[:, :, :, :, :, None, :] * eye[None, None, :, None, None, :, None]
        return n.reshape(NT, 2 * S5_ST, S5_CAT)

    rd_f = readout(cm[0][None] * apow(steps + 1.0)[:, 0][:, :, None, :])
    rd_b = readout(cm[1][None] * apow(LC - steps)[:, 1][:, :, None, :])

    w1 = jnp.concatenate([toe, inj_f, inj_b], axis=-1).astype(BF16)
    w2 = jnp.concatenate([rd_f, rd_b], axis=1).astype(BF16)
    a_lc = apow(jnp.full((1,), LC, F32))[0]
    a16 = jnp.stack([jnp.real(a_lc[0]), jnp.imag(a_lc[0]), jnp.real(a_lc[1]), jnp.imag(a_lc[1])], 0)
    a16 = a16.reshape(4, NT, S5_ST).transpose(1, 0, 2)
    gw = glu_w.astype(F32).reshape(NT, GT, C, C)
    gw = (gw[:, :, :, None, :] * eye[None, :, None, :, None]).reshape(NT, LANES, LANES).astype(BF16)
    dk = d_skip.astype(F32).reshape(NT, 1, LANES)
    gb = glu_b.astype(F32).reshape(NT, 1, LANES)
    return w1, w2, a16, dk, gw, gb


def _s5_inject_kernel(u_ref, w_ref, r_ref, ucat_ref, *, nb, nch):
    @pl.when(pl.program_id(1) == 0)
    def _():
        for b in range(nb):
            for s in range(S5_LC):
                ucat_ref[b * nch:(b + 1) * nch, s * LANES:(s + 1) * LANES] = (
                    u_ref[b, pl.ds(s, nch, stride=S5_LC), :].astype(BF16))

    r = jnp.dot(ucat_ref[...], w_ref[...], preferred_element_type=F32)
    for b in range(nb):
        r_ref[b] = r[b * nch:(b + 1) * nch]


def _s5_scan_kernel(s_ref, a_ref, x_ref, *, nch, nctx):
    st = S5_ST
    arf, aif, arb, aib = a_ref[0:1, :], a_ref[1:2, :], a_ref[2:3, :], a_ref[3:4, :]

    def body(i, carry):
        fr, fi, br, bi = carry
        kf = i
        kb = jnp.where(i < nctx, nctx - 1 - i, nch + nctx - 1 - i)
        x_ref[pl.ds(kf, 1), 0:st] = fr
        x_ref[pl.ds(kf, 1), st:2 * st] = fi
        x_ref[pl.ds(kb, 1), 2 * st:3 * st] = br
        x_ref[pl.ds(kb, 1), 3 * st:4 * st] = bi
        sfr = s_ref[pl.ds(kf, 1), 0:st]
        sfi = s_ref[pl.ds(kf, 1), st:2 * st]
        sbr = s_ref[pl.ds(kb, 1), 2 * st:3 * st]
        sbi = s_ref[pl.ds(kb, 1), 3 * st:4 * st]
        return (arf * fr - aif * fi + sfr, arf * fi + aif * fr + sfi,
                arb * br - aib * bi + sbr, arb * bi + aib * br + sbi)

    z = jnp.zeros((1, st), F32)
    lax.fori_loop(0, nch, body, (z, z, z, z))


def _s5_readout_kernel(yi_ref, x_ref, w_ref, u_ref, d_ref, gw_ref, gb_ref, o_ref, *, nch):
    half = pl.program_id(2)
    y = yi_ref[...] + jnp.dot(x_ref[...].astype(BF16), w_ref[...], preferred_element_type=F32)
    per = S5_LC // S5_NSPLIT
    for sl in range(per):
        s = half * per + sl
        ys = y[:, sl * LANES:(sl + 1) * LANES] + d_ref[...] * u_ref[pl.ds(s, nch, stride=S5_LC), :]
        ys = jax.nn.gelu(ys)
        z = jnp.dot(ys.astype(BF16), gw_ref[...], preferred_element_type=F32) + gb_ref[...]
        o_ref[pl.ds(s, nch, stride=S5_LC), :] = ys * jax.nn.sigmoid(z)


def _s5_mixer(p, u_col0, L, weights):
    w1, w2, a16, dk, gw, gb = weights
    B, T, _ = p.shape
    nch = T // S5_LC
    nctx = L // S5_LC
    assert T % S5_LC == 0 and L % S5_LC == 0 and u_col0 % LANES == 0
    ucol = u_col0 // LANES
    n1 = w1.shape[-1]
    nblk1 = 1024
    cparams = functools.partial(pltpu.CompilerParams, vmem_limit_bytes=VMEM_LIMIT)

    r = pl.pallas_call(
        functools.partial(_s5_inject_kernel, nb=B, nch=nch),
        grid=(S5_TILES, n1 // nblk1),
        in_specs=[pl.BlockSpec((B, T, LANES), lambda j, n: (0, 0, ucol + j)),
                  pl.BlockSpec((None, S5_CAT, nblk1), lambda j, n: (j, 0, n))],
        out_specs=pl.BlockSpec((B, None, nch, nblk1), lambda j, n: (0, j, 0, n)),
        out_shape=jax.ShapeDtypeStruct((B, S5_TILES, nch, n1), F32),
        scratch_shapes=[pltpu.VMEM((B * nch, S5_CAT), BF16)],
        compiler_params=cparams(dimension_semantics=("arbitrary", "arbitrary")),
    )(p, w1)

    sblk = S5_CAT // (4 * S5_ST)
    assert S5_CAT % (4 * S5_ST) == 0
    xin = pl.pallas_call(
        functools.partial(_s5_scan_kernel, nch=nch, nctx=nctx),
        grid=(B, S5_TILES),
        in_specs=[pl.BlockSpec((None, None, nch, 4 * S5_ST), lambda b, j: (b, j, 0, sblk)),
                  pl.BlockSpec((None, 4, S5_ST), lambda b, j: (j, 0, 0))],
        out_specs=pl.BlockSpec((None, None, nch, 4 * S5_ST), lambda b, j: (b, j, 0, 0)),
        out_shape=jax.ShapeDtypeStruct((B, S5_TILES, nch, 4 * S5_ST), F32),
        compiler_params=cparams(dimension_semantics=("arbitrary", "arbitrary")),
    )(r, a16)

    ncol = S5_CAT // S5_NSPLIT
    return pl.pallas_call(
        functools.partial(_s5_readout_kernel, nch=nch),
        grid=(S5_TILES, B, S5_NSPLIT),
        in_specs=[pl.BlockSpec((None, None, nch, ncol), lambda j, b, h: (b, j, 0, h)),
                  pl.BlockSpec((None, None, nch, 4 * S5_ST), lambda j, b, h: (b, j, 0, 0)),
                  pl.BlockSpec((None, 4 * S5_ST, ncol), lambda j, b, h: (j, 0, h)),
                  pl.BlockSpec((None, T, LANES), lambda j, b, h: (b, 0, ucol + j)),
                  pl.BlockSpec((None, 1, LANES), lambda j, b, h: (j, 0, 0)),
                  pl.BlockSpec((None, LANES, LANES), lambda j, b, h: (j, 0, 0)),
                  pl.BlockSpec((None, 1, LANES), lambda j, b, h: (j, 0, 0))],
        out_specs=pl.BlockSpec((None, T, LANES), lambda j, b, h: (b, 0, j)),
        out_shape=jax.ShapeDtypeStruct((B, T, S5_WIDTH), F32),
        compiler_params=cparams(dimension_semantics=("arbitrary", "arbitrary", "arbitrary")),
    )(r, xin, w2, p, dk, gw, gb)


def _rmsnorm(x, g):
    xf = x.astype(F32)
    y = xf * lax.rsqrt(jnp.mean(xf * xf, axis=-1, keepdims=True) + EPS)
    return (y * g.astype(F32)).astype(x.dtype)


def _modulate(h, shift, scale):
    return h * (1 + scale) + shift


def _l2norm(a):
    return a * lax.rsqrt(jnp.sum(a * a, axis=-1, keepdims=True) + EPS)


def _to_bwd(a, L):
    return jnp.concatenate([jnp.flip(a[:, :L], axis=1), jnp.flip(a[:, L:], axis=1)], axis=1)


def _identity(a):
    return a


def _hgrn2_chunk(q, k, v, log_f):
    B, T, H, K = q.shape
    V = v.shape[-1]
    C = HG_CHUNK
    n = T // C
    blk = lambda a: jnp.moveaxis(a.reshape(B, n, C, H, a.shape[-1]), 3, 1)
    q, k, v, log_f = blk(q), blk(k), blk(v), blk(log_f)
    b = jnp.cumsum(log_f, axis=3)
    ref = b[:, :, :, C // 2 - 1:C // 2]
    att = jnp.einsum('bhntk,bhnsk->bhnts', q * jnp.exp(b - ref), k * jnp.exp(ref - b))
    att = jnp.where(jnp.tril(jnp.ones((C, C), bool)), att, 0.0)
    b_last = b[:, :, :, -1]
    kv = jnp.einsum('bhnsk,bhnsv->bhnkv', k * jnp.exp(b_last[:, :, :, None] - b), v)

    def step(S, xs):
        dec, kv_c = xs
        return S * dec[..., None] + kv_c, S

    _, s_start = lax.scan(step, jnp.zeros((B, H, K, V), F32),
                          (jnp.moveaxis(jnp.exp(b_last), 2, 0), jnp.moveaxis(kv, 2, 0)))
    s_start = jnp.moveaxis(s_start, 0, 2)
    o = (jnp.einsum('bhnts,bhnsv->bhntv', att, v)
         + jnp.einsum('bhntk,bhnkv->bhntv', q * jnp.exp(b), s_start))
    return jnp.moveaxis(o, 1, 3).reshape(B, T, H, V)


def _hgrn2_mixer(p, L, lb, norm_g):
    B, T, _ = p.shape
    heads = lambda a: a.astype(F32).reshape(B, T, HG_HEADS, HG_HEAD_DIM)
    q, f_f, f_b, i, g = jnp.split(p, 5, axis=-1)
    q, i = heads(q), heads(i)

    def direction(fz, lb_d, reorder):
        lbh = lb_d.astype(F32).reshape(HG_HEADS, HG_HEAD_DIM)
        f = lbh + (1.0 - lbh) * jax.nn.sigmoid(heads(fz))
        y = _hgrn2_chunk(reorder(q), reorder(1.0 - f), reorder(i), reorder(jnp.log(f)))
        return reorder(y)

    o = direction(f_f, lb[0], _identity) + direction(f_b, lb[1], lambda a: _to_bwd(a, L))
    o = _rmsnorm(o, norm_g) * jax.nn.silu(heads(g))
    return o.reshape(B, T, HG_WIDTH).astype(p.dtype)


def _short_conv(a, w, rows):
    B, T, Ch = a.shape
    img = a.reshape(B, rows, T // rows, Ch)
    y = lax.conv_general_dilated(img, w[:, :, None, :].astype(a.dtype), window_strides=(1, 1),
                                 padding='SAME', dimension_numbers=('NHWC', 'HWIO', 'NHWC'),
                                 feature_group_count=Ch)
    return y.reshape(B, T, Ch)


def _delta_chunk(q, k, v, g, beta):
    B, T, H, K = q.shape
    V = v.shape[-1]
    C = DN_CHUNK
    n = T // C
    blk = lambda a: jnp.moveaxis(a.reshape((B, n, C, H) + a.shape[3:]), 3, 1)
    q = blk(q) * K ** -0.5
    k, v, g, beta = blk(k), blk(v), blk(g), blk(beta)
    gc = jnp.cumsum(g, axis=-1)
    incl = jnp.tril(jnp.ones((C, C), bool))
    strict = jnp.tril(jnp.ones((C, C), bool), -1)
    diff = gc[..., :, None] - gc[..., None, :]
    decay = jnp.where(incl, jnp.exp(jnp.where(incl, diff, 0.0)), 0.0)
    kb = k * beta[..., None]
    a_mat = jnp.where(strict, jnp.einsum('bhnik,bhnjk->bhnij', kb, k) * decay, 0.0)
    eye = jnp.eye(C, dtype=F32)
    t_mat = lax.linalg.triangular_solve(eye + a_mat, jnp.broadcast_to(eye, a_mat.shape),
                                        left_side=True, lower=True)
    u = t_mat @ (v * beta[..., None])
    w = t_mat @ (kb * jnp.exp(gc)[..., None])
    qk = jnp.einsum('bhnik,bhnjk->bhnij', q, k) * decay
    q_dec = q * jnp.exp(gc)[..., None]
    k_dec = k * jnp.exp(gc[..., -1:] - gc)[..., None]
    g_last = jnp.exp(gc[..., -1])

    def step(S, xs):
        u_c, w_c, qk_c, qd_c, kd_c, gl_c = xs
        v_new = u_c - jnp.einsum('bhck,bhkv->bhcv', w_c, S)
        o = jnp.einsum('bhck,bhkv->bhcv', qd_c, S) + jnp.einsum('bhcs,bhsv->bhcv', qk_c, v_new)
        S = S * gl_c[..., None, None] + jnp.einsum('bhck,bhcv->bhkv', kd_c, v_new)
        return S, o

    xs = tuple(jnp.moveaxis(a, 2, 0) for a in (u, w, qk, q_dec, k_dec, g_last))
    _, o = lax.scan(step, jnp.zeros((B, H, K, V), F32), xs)
    return jnp.transpose(o, (1, 0, 3, 2, 4)).reshape(B, T, H, V)


def _deltanet_mixer(h_all, L, rows, w_in, conv_w, a_log, dt_bias, norm_g, w_out):
    p = h_all @ w_in
    B, T, _ = p.shape
    conv_in = p[..., :DN_QKV]
    qkv = jax.nn.silu(jnp.concatenate([_short_conv(conv_in[:, :L], conv_w, 1),
                                       _short_conv(conv_in[:, L:], conv_w, rows)], axis=1)).astype(F32)
    q, k, v = jnp.split(qkv, [DN_QK, 2 * DN_QK], axis=-1)
    Hv = DN_V_HEADS
    z, b_f, b_b, a_f, a_b = jnp.split(p[..., DN_QKV:].astype(F32),
                                      [DN_V, DN_V + Hv, DN_V + 2 * Hv, DN_V + 3 * Hv], axis=-1)
    rep = DN_V_HEADS // DN_QK_HEADS
    q = jnp.repeat(_l2norm(q.reshape(B, T, DN_QK_HEADS, DN_HEAD_DIM)), rep, axis=2)
    k = jnp.repeat(_l2norm(k.reshape(B, T, DN_QK_HEADS, DN_HEAD_DIM)), rep, axis=2)
    v = v.reshape(B, T, DN_V_HEADS, DN_HEAD_DIM)

    def direction(dd, b_z, a_z, reorder):
        beta = jax.nn.sigmoid(b_z)
        g = -jnp.exp(a_log[dd].astype(F32)) * jax.nn.softplus(a_z + dt_bias[dd].astype(F32))
        return reorder(_delta_chunk(reorder(q), reorder(k), reorder(v), reorder(g), reorder(beta)))

    o = direction(0, b_f, a_f, _identity) + direction(1, b_b, a_b, lambda a: _to_bwd(a, L))
    o = _rmsnorm(o, norm_g) * jax.nn.silu(z.reshape(B, T, DN_V_HEADS, DN_HEAD_DIM))
    return o.reshape(B, T, DN_V).astype(h_all.dtype) @ w_out


def _ec_moe(h, router_w, w_gate, w_up, w_down):
    B, n, D = h.shape
    cap = EC_CAPACITY * n // N_EXPERTS
    aff = jax.nn.softmax(jnp.einsum('bnd,de->bne', h, router_w).astype(F32), axis=-1)
    gate, idx = lax.top_k(jnp.swapaxes(aff, 1, 2), cap)
    xs = jax.vmap(lambda hb, ib: hb[ib])(h, idx)
    hid = jax.nn.silu(jnp.einsum('becd,edf->becf', xs, w_gate)) * jnp.einsum('becd,edf->becf', xs, w_up)
    y = jnp.einsum('becf,efd->becd', hid, w_down) * gate[..., None].astype(h.dtype)
    return jax.vmap(lambda ib, yb: jnp.zeros((n, D), yb.dtype).at[ib.reshape(-1)].add(yb.reshape(-1, D)))(idx, y)


def kernel(x, c, ctx, c_ctx, mod_w, mod_b, norm_g, ev_w_in, ev_w_out, hg_lb, hg_norm_g,
           s5_a_re, s5_a_im, s5_log_dt, s5_b_re, s5_b_im, s5_c_re, s5_c_im, s5_d, s5_glu_w, s5_glu_b,
           od_w_in, od_conv_w, od_a_log, od_dt_bias, od_norm_g, od_w_out,
           router_w, moe_w_gate, moe_w_up, moe_w_down):
    L = ctx.shape[1]
    N = x.shape[1]
    rows = N // GRID_W
    lb_all = jnp.cumsum(jax.nn.softmax(hg_lb.astype(F32), axis=0), axis=0)
    sc = jax.nn.silu(c)
    scc = jax.nn.silu(c_ctx)
    for layer in range(DEPTH):
        last = layer == DEPTH - 1
        j = layer // 2
        m_lat = jnp.split((sc @ mod_w[layer] + mod_b[layer])[:, None, :], N_MOD, axis=-1)
        m_ctx = jnp.split(scc @ mod_w[layer] + mod_b[layer], N_MOD, axis=-1)
        g_pre_mix, g_post_mix, g_pre_ffn, g_post_ffn = norm_g[layer]
        h_all = jnp.concatenate([_modulate(_rmsnorm(ctx, g_pre_mix), m_ctx[0], m_ctx[1]),
                                 _modulate(_rmsnorm(x, g_pre_mix), m_lat[0], m_lat[1])], axis=1)
        if layer % 2 == 0:
            p = h_all @ ev_w_in[j]
            o_hg = _hgrn2_mixer(p[..., :5 * HG_WIDTH], L, lb_all[j], hg_norm_g[j])
            s5w = _s5_weights(s5_a_re[j], s5_a_im[j], s5_log_dt[j], s5_b_re[j], s5_b_im[j],
                              s5_c_re[j], s5_c_im[j], s5_d[j], s5_glu_w[j], s5_glu_b[j])
            o_s5 = _s5_mixer(p, 5 * HG_WIDTH, L, s5w)
            o = jnp.concatenate([o_hg, o_s5], axis=-1) @ ev_w_out[j]
        else:
            o = _deltanet_mixer(h_all, L, rows, od_w_in[j], od_conv_w[j], od_a_log[j], od_dt_bias[j],
                                od_norm_g[j], od_w_out[j])
        o = _rmsnorm(o, g_post_mix)
        x = x + m_lat[2] * o[:, L:]
        if not last:
            ctx = ctx + m_ctx[2] * o[:, :L]
        h = _modulate(_rmsnorm(x, g_pre_ffn), m_lat[3], m_lat[4])
        x = x + m_lat[5] * _rmsnorm(_ec_moe(h, router_w[layer], moe_w_gate[layer], moe_w_up[layer],
                                            moe_w_down[layer]), g_post_ffn)
        if not last:
            hc = _modulate(_rmsnorm(ctx, g_pre_ffn), m_ctx[3], m_ctx[4])
            ctx = ctx + m_ctx[5] * _rmsnorm(_ec_moe(hc, router_w[layer], moe_w_gate[layer], moe_w_up[layer],
                                                    moe_w_down[layer]), g_post_ffn)
    return x
```

```python
import functools
import math

import jax
import jax.numpy as jnp
from jax import lax
from jax.experimental import pallas as pl
from jax.experimental.pallas import tpu as pltpu

D_MODEL = 2048
DEPTH = 2
GRID_W = 64
N_MOD = 6
EPS = 1e-6

HG_WIDTH = D_MODEL // 2
HG_HEAD_DIM = 128
HG_HEADS = HG_WIDTH // HG_HEAD_DIM
HG_CHUNK = 64
S5_WIDTH = D_MODEL - HG_WIDTH
S5_GROUP = 16
S5_GROUPS = S5_WIDTH // S5_GROUP
S5_STATE = 64

DN_QK_HEADS = 16
DN_V_HEADS = 32
DN_HEAD_DIM = 128
DN_QK = DN_QK_HEADS * DN_HEAD_DIM
DN_V = DN_V_HEADS * DN_HEAD_DIM
DN_QKV = 2 * DN_QK + DN_V
DN_CHUNK = 64

N_EXPERTS = 16
EC_CAPACITY = 2

LANES = 128
VMEM_LIMIT = 56 * 1024 * 1024

S5_LC = 16
S5_TILES = S5_WIDTH // LANES
S5_GPT = LANES // S5_GROUP
S5_ST = S5_GPT * S5_STATE
S5_CAT = S5_LC * LANES
S5_NSPLIT = 2

F32 = jnp.float32
BF16 = jnp.bfloat16


def _s5_weights(a_re, a_im, log_dt, b_re, b_im, c_re, c_im, d_skip, glu_w, glu_b):
    G, P, C, LC, NT, GT = S5_GROUPS, S5_STATE, S5_GROUP, S5_LC, S5_TILES, S5_GPT
    ar, ai = a_re.astype(F32), a_im.astype(F32)
    dt = jnp.exp(log_dt.astype(F32))[..., None]
    steps = jnp.arange(LC, dtype=F32)
    cmul = lambda x, y: (x[0] * y[0] - x[1] * y[1], x[0] * y[1] + x[1] * y[0])

    def apow(e):
        ex = e[:, None, None, None]
        mag = jnp.exp(ex * (ar * dt)[None])
        ang = ex * (ai * dt)[None]
        return mag * jnp.cos(ang), mag * jnp.sin(ang)

    one = apow(jnp.ones((1,), F32))
    nr, ni = one[0][0] - 1.0, one[1][0]
    den = ar * ar + ai * ai
    zoh = ((nr * ar + ni * ai) / den, (ni * ar - nr * ai) / den)
    bb = cmul((zoh[0][..., None], zoh[1][..., None]), (b_re.astype(F32), b_im.astype(F32)))
    cm = (c_re.astype(F32), c_im.astype(F32))
    eye = jnp.eye(GT, dtype=F32)
    hi = lax.Precision.HIGHEST
    at = lambda z, idx: (z[0][idx], z[1][idx])

    cp = cmul((cm[0][None], cm[1][None]), at(apow(steps), (slice(None), slice(None), slice(None), None)))
    kk = (jnp.einsum('tdgcp,dgpk->dtgck', cp[0], bb[0], precision=hi)
          - jnp.einsum('tdgcp,dgpk->dtgck', cp[1], bb[1], precision=hi))
    s_in = jnp.arange(LC)[:, None, None]
    s_out = jnp.arange(LC)[None, :, None]
    lag = jnp.arange(LC)[None, None, :]
    sel_f = (s_out - s_in == lag).astype(F32)
    sel_b = (s_in - s_out == lag).astype(F32)
    toe = (jnp.einsum('abt,tgok->abgok', sel_f, kk[0], precision=hi)
           + jnp.einsum('abt,tgok->abgok', sel_b, kk[1], precision=hi))
    toe = toe.reshape(LC, LC, NT, GT, C, C).transpose(2, 0, 3, 5, 1, 4)
    toe = toe[:, :, :, :, :, None, :] * eye[None, None, :, None, None, :, None]
    toe = toe.reshape(NT, S5_CAT, S5_CAT)

    def inject(e):
        er = jnp.stack(e, 0)
        er = er.reshape(2, LC, NT, GT, P, C).transpose(2, 1, 3, 5, 0, 4)
        m = er[:, :, :, :, :, None, :] * eye[None, None, :, None, None, :, None]
        return m.reshape(NT, S5_CAT, 2 * S5_ST)

    def pw_dir(e, d):
        z = apow(e)
        return z[0][:, d][..., None], z[1][:, d][..., None]

    inj_f = inject(cmul(pw_dir(LC - 1 - steps, 0), (bb[0][0][None], bb[1][0][None])))
    inj_b = inject(cmul(pw_dir(steps, 1), (bb[0][1][None], bb[1][1][None])))

    def readout(w):
        wr = jnp.stack([w[0], -w[1]], 0)
        wr = wr.reshape(2, LC, NT, GT, C, P).transpose(2, 0, 3, 5, 1, 4)
        n = wr[:, :, :, :, :, None, :] * eye[None, None, :, None, None, :, None]
        return n.reshape(NT, 2 * S5_ST, S5_CAT)

    def pw_row(e, d):
        z = apow(e)
        return z[0][:, d][:, :, None, :], z[1][:, d][:, :, None, :]

    rd_f = readout(cmul((cm[0][0][None], cm[1][0][None]), pw_row(steps + 1.0, 0)))
    rd_b = readout(cmul((cm[0][1][None], cm[1][1][None]), pw_row(LC - steps, 1)))

    w1 = jnp.concatenate([toe, inj_f, inj_b], axis=-1).astype(BF16)
    w2 = jnp.concatenate([rd_f, rd_b], axis=1).astype(BF16)
    a_lc = apow(jnp.full((1,), LC, F32))
    a16 = jnp.stack([a_lc[0][0, 0], a_lc[1][0, 0], a_lc[0][0, 1], a_lc[1][0, 1]], 0)
    a16 = a16.reshape(4, NT, S5_ST).transpose(1, 0, 2)
    gw = glu_w.astype(F32).reshape(NT, GT, C, C)
    gw = (gw[:, :, :, None, :] * eye[None, :, None, :, None]).reshape(NT, LANES, LANES).astype(BF16)
    dk = d_skip.astype(F32).reshape(NT, 1, LANES)
    gb = glu_b.astype(F32).reshape(NT, 1, LANES)
    return w1, w2, a16, dk, gw, gb


def _s5_inject_kernel(u_ref, w_ref, r_ref, ucat_ref, *, nb, nch):
    @pl.when(pl.program_id(1) == 0)
    def _():
        for b in range(nb):
            for s in range(S5_LC):
                ucat_ref[b * nch:(b + 1) * nch, s * LANES:(s + 1) * LANES] = (
                    u_ref[b, pl.ds(s, nch, stride=S5_LC), :].astype(BF16))

    r = jnp.dot(ucat_ref[...], w_ref[...], preferred_element_type=F32)
    for b in range(nb):
        r_ref[b] = r[b * nch:(b + 1) * nch]


def _s5_scan_kernel(s_ref, a_ref, x_ref, *, nch, nctx):
    st = S5_ST
    arf, aif, arb, aib = a_ref[0:1, :], a_ref[1:2, :], a_ref[2:3, :], a_ref[3:4, :]

    def body(i, carry):
        fr, fi, br, bi = carry
        kf = i
        kb = jnp.where(i < nctx, nctx - 1 - i, nch + nctx - 1 - i)
        x_ref[pl.ds(kf, 1), 0:st] = fr
        x_ref[pl.ds(kf, 1), st:2 * st] = fi
        x_ref[pl.ds(kb, 1), 2 * st:3 * st] = br
        x_ref[pl.ds(kb, 1), 3 * st:4 * st] = bi
        sfr = s_ref[pl.ds(kf, 1), 0:st]
        sfi = s_ref[pl.ds(kf, 1), st:2 * st]
        sbr = s_ref[pl.ds(kb, 1), 2 * st:3 * st]
        sbi = s_ref[pl.ds(kb, 1), 3 * st:4 * st]
        return (arf * fr - aif * fi + sfr, arf * fi + aif * fr + sfi,
                arb * br - aib * bi + sbr, arb * bi + aib * br + sbi)

    z = jnp.zeros((1, st), F32)
    lax.fori_loop(0, nch, body, (z, z, z, z))


def _s5_readout_kernel(yi_ref, x_ref, w_ref, u_ref, d_ref, gw_ref, gb_ref, o_ref, *, nch):
    half = pl.program_id(2)
    y = yi_ref[...] + jnp.dot(x_ref[...].astype(BF16), w_ref[...], preferred_element_type=F32)
    per = S5_LC // S5_NSPLIT
    for sl in range(per):
        s = half * per + sl
        ys = y[:, sl * LANES:(sl + 1) * LANES] + d_ref[...] * u_ref[pl.ds(s, nch, stride=S5_LC), :]
        ys = jax.nn.gelu(ys)
        z = jnp.dot(ys.astype(BF16), gw_ref[...], preferred_element_type=F32) + gb_ref[...]
        o_ref[pl.ds(s, nch, stride=S5_LC), :] = ys * jax.nn.sigmoid(z)


def _s5_mixer(p, u_col0, L, weights):
    w1, w2, a16, dk, gw, gb = weights
    B, T, _ = p.shape
    nch = T // S5_LC
    nctx = L // S5_LC
    assert T % S5_LC == 0 and L % S5_LC == 0 and u_col0 % LANES == 0
    ucol = u_col0 // LANES
    n1 = w1.shape[-1]
    nblk1 = 1024
    cparams = functools.partial(pltpu.CompilerParams, vmem_limit_bytes=VMEM_LIMIT)

    r = pl.pallas_call(
        functools.partial(_s5_inject_kernel, nb=B, nch=nch),
        grid=(S5_TILES, n1 // nblk1),
        in_specs=[pl.BlockSpec((B, T, LANES), lambda j, n: (0, 0, ucol + j)),
                  pl.BlockSpec((None, S5_CAT, nblk1), lambda j, n: (j, 0, n))],
        out_specs=pl.BlockSpec((B, None, nch, nblk1), lambda j, n: (0, j, 0, n)),
        out_shape=jax.ShapeDtypeStruct((B, S5_TILES, nch, n1), F32),
        scratch_shapes=[pltpu.VMEM((B * nch, S5_CAT), BF16)],
        compiler_params=cparams(dimension_semantics=("arbitrary", "arbitrary")),
    )(p, w1)

    sblk = S5_CAT // (4 * S5_ST)
    assert S5_CAT % (4 * S5_ST) == 0
    xin = pl.pallas_call(
        functools.partial(_s5_scan_kernel, nch=nch, nctx=nctx),
        grid=(B, S5_TILES),
        in_specs=[pl.BlockSpec((None, None, nch, 4 * S5_ST), lambda b, j: (b, j, 0, sblk)),
                  pl.BlockSpec((None, 4, S5_ST), lambda b, j: (j, 0, 0))],
        out_specs=pl.BlockSpec((None, None, nch, 4 * S5_ST), lambda b, j: (b, j, 0, 0)),
        out_shape=jax.ShapeDtypeStruct((B, S5_TILES, nch, 4 * S5_ST), F32),
        compiler_params=cparams(dimension_semantics=("arbitrary", "arbitrary")),
    )(r, a16)

    ncol = S5_CAT // S5_NSPLIT
    return pl.pallas_call(
        functools.partial(_s5_readout_kernel, nch=nch),
        grid=(S5_TILES, B, S5_NSPLIT),
        in_specs=[pl.BlockSpec((None, None, nch, ncol), lambda j, b, h: (b, j, 0, h)),
                  pl.BlockSpec((None, None, nch, 4 * S5_ST), lambda j, b, h: (b, j, 0, 0)),
                  pl.BlockSpec((None, 4 * S5_ST, ncol), lambda j, b, h: (j, 0, h)),
                  pl.BlockSpec((None, T, LANES), lambda j, b, h: (b, 0, ucol + j)),
                  pl.BlockSpec((None, 1, LANES), lambda j, b, h: (j, 0, 0)),
                  pl.BlockSpec((None, LANES, LANES), lambda j, b, h: (j, 0, 0)),
                  pl.BlockSpec((None, 1, LANES), lambda j, b, h: (j, 0, 0))],
        out_specs=pl.BlockSpec((None, T, LANES), lambda j, b, h: (b, 0, j)),
        out_shape=jax.ShapeDtypeStruct((B, T, S5_WIDTH), F32),
        compiler_params=cparams(dimension_semantics=("arbitrary", "arbitrary", "arbitrary")),
    )(r, xin, w2, p, dk, gw, gb)


def _rmsnorm(x, g):
    xf = x.astype(F32)
    y = xf * lax.rsqrt(jnp.mean(xf * xf, axis=-1, keepdims=True) + EPS)
    return (y * g.astype(F32)).astype(x.dtype)


def _modulate(h, shift, scale):
    return h * (1 + scale) + shift


def _l2norm(a):
    return a * lax.rsqrt(jnp.sum(a * a, axis=-1, keepdims=True) + EPS)


def _to_bwd(a, L):
    return jnp.concatenate([jnp.flip(a[:, :L], axis=1), jnp.flip(a[:, L:], axis=1)], axis=1)


def _identity(a):
    return a


def _hgrn2_chunk(q, k, v, log_f):
    B, T, H, K = q.shape
    V = v.shape[-1]
    C = HG_CHUNK
    n = T // C
    blk = lambda a: jnp.moveaxis(a.reshape(B, n, C, H, a.shape[-1]), 3, 1)
    q, k, v, log_f = blk(q), blk(k), blk(v), blk(log_f)
    b = jnp.cumsum(log_f, axis=3)
    ref = b[:, :, :, C // 2 - 1:C // 2]
    att = jnp.einsum('bhntk,bhnsk->bhnts', q * jnp.exp(b - ref), k * jnp.exp(ref - b))
    att = jnp.where(jnp.tril(jnp.ones((C, C), bool)), att, 0.0)
    b_last = b[:, :, :, -1]
    kv = jnp.einsum('bhnsk,bhnsv->bhnkv', k * jnp.exp(b_last[:, :, :, None] - b), v)

    def step(S, xs):
        dec, kv_c = xs
        return S * dec[..., None] + kv_c, S

    _, s_start = lax.scan(step, jnp.zeros((B, H, K, V), F32),
                          (jnp.moveaxis(jnp.exp(b_last), 2, 0), jnp.moveaxis(kv, 2, 0)))
    s_start = jnp.moveaxis(s_start, 0, 2)
    o = (jnp.einsum('bhnts,bhnsv->bhntv', att, v)
         + jnp.einsum('bhntk,bhnkv->bhntv', q * jnp.exp(b), s_start))
    return jnp.moveaxis(o, 1, 3).reshape(B, T, H, V)


def _hgrn2_mixer(p, L, lb, norm_g):
    B, T, _ = p.shape
    heads = lambda a: a.astype(F32).reshape(B, T, HG_HEADS, HG_HEAD_DIM)
    q, f_f, f_b, i, g = jnp.split(p, 5, axis=-1)
    q, i = heads(q), heads(i)

    def direction(fz, lb_d, reorder):
        lbh = lb_d.astype(F32).reshape(HG_HEADS, HG_HEAD_DIM)
        f = lbh + (1.0 - lbh) * jax.nn.sigmoid(heads(fz))
        y = _hgrn2_chunk(reorder(q), reorder(1.0 - f), reorder(i), reorder(jnp.log(f)))
        return reorder(y)

    o = direction(f_f, lb[0], _identity) + direction(f_b, lb[1], lambda a: _to_bwd(a, L))
    o = _rmsnorm(o, norm_g) * jax.nn.silu(heads(g))
    return o.reshape(B, T, HG_WIDTH).astype(p.dtype)


def _short_conv(a, w, rows):
    B, T, Ch = a.shape
    img = a.reshape(B, rows, T // rows, Ch)
    y = lax.conv_general_dilated(img, w[:, :, None, :].astype(a.dtype), window_strides=(1, 1),
                                 padding='SAME', dimension_numbers=('NHWC', 'HWIO', 'NHWC'),
                                 feature_group_count=Ch)
    return y.reshape(B, T, Ch)


DN_SC = 256
DN_NC = DN_SC // DN_CHUNK
DN_REP = DN_V_HEADS // DN_QK_HEADS
DN_LEVELS = DN_CHUNK.bit_length() - 1
DN_SCAN_HG = 4


def _dn_masks():
    out = []
    for reverse in (False, True):
        i = jnp.arange(DN_SC)[:, None]
        j = jnp.arange(DN_SC)[None, :]
        same = (i // DN_CHUNK) == (j // DN_CHUNK)
        if reverse:
            i, j = j, i
        ms = [same & (i >= j), same & (i > j)]
        for lv in range(DN_LEVELS):
            ms.append(same & ((i >> (lv + 1)) == (j >> (lv + 1))) & (((i >> lv) & 1) == 1) & (((j >> lv) & 1) == 0))
        out.append(jnp.stack(ms))
    return jnp.stack(out).astype(F32)


def _lane_pick(x, col):
    hot = (lax.broadcasted_iota(jnp.int32, (1, LANES), 1) == col).astype(F32)
    return jnp.sum(x * hot, axis=1, keepdims=True)


def _dn_prep_kernel(q_ref, k_ref, v_ref, g_ref, m_ref, uwf_ref, uwb_ref, qkf_ref, qkb_ref, gc_ref,
                    gcs_ref, gct_ref):
    C, D = DN_CHUNK, DN_HEAD_DIM
    hq = pl.program_id(2)
    gb = g_ref[...]

    @pl.when(hq == 0)
    def _():
        lane = lax.broadcasted_iota(jnp.int32, (1, LANES), 1)
        out = gb
        for d in range(2):
            cs = jnp.dot(m_ref[d, 0], gb, preferred_element_type=F32, precision=lax.Precision.HIGHEST)
            gcs_ref[d] = cs
            gct_ref[d] = cs.T
            lo = 2 * DN_V_HEADS + d * DN_V_HEADS
            out = jnp.where((lane >= lo) & (lane < lo + DN_V_HEADS), cs, out)
        gc_ref[...] = out

    q = q_ref[...]
    k = k_ref[...]
    kk = lax.dot_general(k, k, (((1,), (1,)), ((), ())), preferred_element_type=F32)
    qk = lax.dot_general(q, k, (((1,), (1,)), ((), ())), preferred_element_type=F32)
    kf = k.astype(F32)
    uw_refs = (uwf_ref, uwb_ref)
    qk_refs = (qkf_ref, qkb_ref)
    insts = [(d, r) for d in range(2) for r in range(DN_REP)]
    bcol, gcol, dec, a_mat, m = {}, {}, {}, {}, {}
    for i in insts:
        d, r = i
        hv = hq * DN_REP + r
        incl = m_ref[d, 0]
        strict = m_ref[d, 1]
        bcol[i] = _lane_pick(gb, d * DN_V_HEADS + hv)
        gcol[i] = _lane_pick(gcs_ref[d], 2 * DN_V_HEADS + d * DN_V_HEADS + hv)
        grow = gct_ref[d, pl.ds(2 * DN_V_HEADS + d * DN_V_HEADS + hv, 1), :]
        dec[i] = incl * jnp.exp(jnp.where(incl > 0, gcol[i] - grow, 0.0))
        a_mat[i] = strict * (bcol[i] * (kk * dec[i]))
        m[i] = (incl - strict) - a_mat[i] * m_ref[d, 2]
    for lv in range(1, DN_LEVELS):
        mb = {i: m[i].astype(BF16) for i in insts}
        x = {i: jnp.dot(mb[i], (a_mat[i] * m_ref[i[0], 2 + lv]).astype(BF16), preferred_element_type=F32)
             for i in insts}
        for i in insts:
            m[i] = m[i] - jnp.dot(x[i].astype(BF16), mb[i], preferred_element_type=F32)
    for i in insts:
        d, r = i
        egc = jnp.exp(gcol[i])
        v = v_ref[:, r * D:(r + 1) * D].astype(F32)
        rhs = jnp.concatenate([(v * bcol[i]).astype(BF16), (kf * (bcol[i] * egc)).astype(BF16)], axis=1)
        uw = jnp.dot(m[i].astype(BF16), rhs, preferred_element_type=F32)
        uw_refs[d][:, r * 2 * D:(r + 1) * 2 * D] = uw.astype(BF16)
    for d in range(2):
        parts = []
        for r in range(DN_REP):
            qd = qk * dec[(d, r)]
            parts.append(jnp.concatenate([qd[c * C:(c + 1) * C, c * C:(c + 1) * C] for c in range(DN_NC)],
                                         axis=0))
        qk_refs[d][...] = jnp.concatenate(parts, axis=1).astype(BF16)


def _dn_scan_kernel(uw_ref, qkd_ref, q_ref, k_ref, gc_ref, o_ref, s_ref, *, reverse):
    C, D = DN_CHUNK, DN_HEAD_DIM
    dirn = 1 if reverse else 0
    hgrp = pl.program_id(1)
    last = 0 if reverse else C - 1

    @pl.when(pl.program_id(2) == 0)
    def _():
        s_ref[...] = jnp.zeros_like(s_ref)

    heads = range(DN_SCAN_HG * DN_REP)
    for ci in range(DN_NC):
        c = (DN_NC - 1 - ci) if reverse else ci
        r0 = c * C
        gcb = gc_ref[r0:r0 + C, :]
        gcol, g_end, s_old, ws = {}, {}, {}, {}
        for hl in heads:
            hq = hl // DN_REP
            hv = hgrp * (DN_SCAN_HG * DN_REP) + hl
            gcol[hl] = _lane_pick(gcb, 2 * DN_V_HEADS + dirn * DN_V_HEADS + hv)
            g_end[hl] = gcol[hl][last:last + 1, :]
            qf = q_ref[r0:r0 + C, hq * D:(hq + 1) * D].astype(F32)
            w = uw_ref[r0:r0 + C, hl * 2 * D + D:(hl + 1) * 2 * D]
            s_old[hl] = s_ref[hl]
            wq = jnp.concatenate([w, (qf * jnp.exp(gcol[hl])).astype(BF16)], axis=0)
            ws[hl] = jnp.dot(wq, s_old[hl].astype(BF16), preferred_element_type=F32)
        for hl in heads:
            hq = hl // DN_REP
            u = uw_ref[r0:r0 + C, hl * 2 * D:hl * 2 * D + D].astype(F32)
            vnb = (u - ws[hl][:C]).astype(BF16)
            qkd = qkd_ref[r0:r0 + C, hl * C:(hl + 1) * C]
            o_ref[r0:r0 + C, hl * D:(hl + 1) * D] = ws[hl][C:] + jnp.dot(qkd, vnb, preferred_element_type=F32)
            kf = k_ref[r0:r0 + C, hq * D:(hq + 1) * D].astype(F32)
            k_dec_t = (kf * jnp.exp(g_end[hl] - gcol[hl])).T.astype(BF16)
            s_ref[hl] = s_old[hl] * jnp.exp(g_end[hl]) + jnp.dot(k_dec_t, vnb, preferred_element_type=F32)


def _delta_rule(q, k, v, gates, L):
    B, T, _ = q.shape
    nsc = T // DN_SC
    assert T % DN_SC == 0 and L == DN_SC and DN_QK_HEADS % DN_SCAN_HG == 0
    D, C = DN_HEAD_DIM, DN_CHUNK
    masks = _dn_masks()
    cp = functools.partial(pltpu.CompilerParams, vmem_limit_bytes=VMEM_LIMIT)
    blk = lambda w: pl.BlockSpec((None, DN_SC, w), lambda b, s, h: (b, s, h))
    uwf, uwb, qkf, qkb, gc = pl.pallas_call(
        _dn_prep_kernel,
        grid=(B, nsc, DN_QK_HEADS),
        in_specs=[blk(D), blk(D), blk(DN_REP * D),
                  pl.BlockSpec((None, DN_SC, LANES), lambda b, s, h: (b, s, 0)),
                  pl.BlockSpec(masks.shape, lambda b, s, h: (0, 0, 0, 0))],
        out_specs=[blk(DN_REP * 2 * D), blk(DN_REP * 2 * D), blk(DN_REP * C), blk(DN_REP * C),
                   pl.BlockSpec((None, DN_SC, LANES), lambda b, s, h: (b, s, 0))],
        out_shape=[jax.ShapeDtypeStruct((B, T, DN_V_HEADS * 2 * D), BF16)] * 2
        + [jax.ShapeDtypeStruct((B, T, DN_V_HEADS * C), BF16)] * 2
        + [jax.ShapeDtypeStruct((B, T, LANES), F32)],
        scratch_shapes=[pltpu.VMEM((2, DN_SC, LANES), F32), pltpu.VMEM((2, LANES, DN_SC), F32)],
        compiler_params=cp(dimension_semantics=("arbitrary", "arbitrary", "arbitrary")),
    )(q, k, v, gates, masks)

    outs = []
    G = DN_SCAN_HG
    for reverse, uw, qkd in ((False, uwf, qkf), (True, uwb, qkb)):
        if reverse:
            sc_of = lambda i: jnp.where(i == 0, 0, nsc - i)
        else:
            sc_of = lambda i: i
        sblk = lambda w, f=sc_of: pl.BlockSpec((None, DN_SC, w), lambda b, h, i: (b, f(i), h))
        outs.append(pl.pallas_call(
            functools.partial(_dn_scan_kernel, reverse=reverse),
            grid=(B, DN_QK_HEADS // G, nsc),
            in_specs=[sblk(G * DN_REP * 2 * D), sblk(G * DN_REP * C), sblk(G * D), sblk(G * D),
                      pl.BlockSpec((None, DN_SC, LANES), lambda b, h, i, f=sc_of: (b, f(i), 0))],
            out_specs=sblk(G * DN_REP * D),
            out_shape=jax.ShapeDtypeStruct((B, T, DN_V), F32),
            scratch_shapes=[pltpu.VMEM((G * DN_REP, D, D), F32)],
            compiler_params=cp(dimension_semantics=("arbitrary", "arbitrary", "arbitrary")),
        )(uw, qkd, q, k, gc))
    return outs


def _deltanet_mixer(h_all, L, rows, w_in, conv_w, a_log, dt_bias, norm_g, w_out):
    p = h_all @ w_in
    B, T, _ = p.shape
    conv_in = p[..., :DN_QKV]
    qkv = jax.nn.silu(jnp.concatenate([_short_conv(conv_in[:, :L], conv_w, 1),
                                       _short_conv(conv_in[:, L:], conv_w, rows)], axis=1)).astype(F32)
    q, k, v = jnp.split(qkv, [DN_QK, 2 * DN_QK], axis=-1)
    Hv = DN_V_HEADS
    z, b_f, b_b, a_f, a_b = jnp.split(p[..., DN_QKV:].astype(F32),
                                      [DN_V, DN_V + Hv, DN_V + 2 * Hv, DN_V + 3 * Hv], axis=-1)
    q = _l2norm(q.reshape(B, T, DN_QK_HEADS, DN_HEAD_DIM)) * DN_HEAD_DIM ** -0.5
    q = q.reshape(B, T, DN_QK).astype(BF16)
    k = _l2norm(k.reshape(B, T, DN_QK_HEADS, DN_HEAD_DIM)).reshape(B, T, DN_QK).astype(BF16)
    v = v.astype(BF16)

    def log_decay(dd, a_z):
        return -jnp.exp(a_log[dd].astype(F32)) * jax.nn.softplus(a_z + dt_bias[dd].astype(F32))

    gates = jnp.concatenate([jax.nn.sigmoid(b_f), jax.nn.sigmoid(b_b), log_decay(0, a_f), log_decay(1, a_b)],
                            axis=-1)
    o_f, o_b = _delta_rule(q, k, v, gates, L)
    o = o_f + o_b
    o = o.reshape(B, T, DN_V_HEADS, DN_HEAD_DIM)
    o = _rmsnorm(o, norm_g) * jax.nn.silu(z.reshape(B, T, DN_V_HEADS, DN_HEAD_DIM))
    return o.reshape(B, T, DN_V).astype(h_all.dtype) @ w_out


def _ec_moe(h, router_w, w_gate, w_up, w_down):
    B, n, D = h.shape
    cap = EC_CAPACITY * n // N_EXPERTS
    aff = jax.nn.softmax(jnp.einsum('bnd,de->bne', h, router_w).astype(F32), axis=-1)
    gate, idx = lax.top_k(jnp.swapaxes(aff, 1, 2), cap)
    xs = jax.vmap(lambda hb, ib: hb[ib])(h, idx)
    hid = jax.nn.silu(jnp.einsum('becd,edf->becf', xs, w_gate)) * jnp.einsum('becd,edf->becf', xs, w_up)
    y = jnp.einsum('becf,efd->becd', hid, w_down) * gate[..., None].astype(h.dtype)
    return jax.vmap(lambda ib, yb: jnp.zeros((n, D), yb.dtype).at[ib.reshape(-1)].add(yb.reshape(-1, D)))(idx, y)


def kernel(x, c, ctx, c_ctx, mod_w, mod_b, norm_g, ev_w_in, ev_w_out, hg_lb, hg_norm_g,
           s5_a_re, s5_a_im, s5_log_dt, s5_b_re, s5_b_im, s5_c_re, s5_c_im, s5_d, s5_glu_w, s5_glu_b,
           od_w_in, od_conv_w, od_a_log, od_dt_bias, od_norm_g, od_w_out,
           router_w, moe_w_gate, moe_w_up, moe_w_down):
    L = ctx.shape[1]
    N = x.shape[1]
    rows = N // GRID_W
    lb_all = jnp.cumsum(jax.nn.softmax(hg_lb.astype(F32), axis=0), axis=0)
    sc = jax.nn.silu(c)
    scc = jax.nn.silu(c_ctx)
    for layer in range(DEPTH):
        last = layer == DEPTH - 1
        j = layer // 2
        m_lat = jnp.split((sc @ mod_w[layer] + mod_b[layer])[:, None, :], N_MOD, axis=-1)
        m_ctx = jnp.split(scc @ mod_w[layer] + mod_b[layer], N_MOD, axis=-1)
        g_pre_mix, g_post_mix, g_pre_ffn, g_post_ffn = norm_g[layer]
        h_all = jnp.concatenate([_modulate(_rmsnorm(ctx, g_pre_mix), m_ctx[0], m_ctx[1]),
                                 _modulate(_rmsnorm(x, g_pre_mix), m_lat[0], m_lat[1])], axis=1)
        if layer % 2 == 0:
            p = h_all @ ev_w_in[j]
            o_hg = _hgrn2_mixer(p[..., :5 * HG_WIDTH], L, lb_all[j], hg_norm_g[j])
            s5w = _s5_weights(s5_a_re[j], s5_a_im[j], s5_log_dt[j], s5_b_re[j], s5_b_im[j],
                              s5_c_re[j], s5_c_im[j], s5_d[j], s5_glu_w[j], s5_glu_b[j])
            o_s5 = _s5_mixer(p, 5 * HG_WIDTH, L, s5w)
            o = jnp.concatenate([o_hg, o_s5], axis=-1) @ ev_w_out[j]
        else:
            o = _deltanet_mixer(h_all, L, rows, od_w_in[j], od_conv_w[j], od_a_log[j], od_dt_bias[j],
                                od_norm_g[j], od_w_out[j])
        o = _rmsnorm(o, g_post_mix)
        x = x + m_lat[2] * o[:, L:]
        if not last:
            ctx = ctx + m_ctx[2] * o[:, :L]
        h = _modulate(_rmsnorm(x, g_pre_ffn), m_lat[3], m_lat[4])
        x = x + m_lat[5] * _rmsnorm(_ec_moe(h, router_w[layer], moe_w_gate[layer], moe_w_up[layer],
                                            moe_w_down[layer]), g_post_ffn)
        if not last:
            hc = _modulate(_rmsnorm(ctx, g_pre_ffn), m_ctx[3], m_ctx[4])
            ctx = ctx + m_ctx[5] * _rmsnorm(_ec_moe(hc, router_w[layer], moe_w_gate[layer], moe_w_up[layer],
                                                    moe_w_down[layer]), g_post_ffn)
    return x
```

```python
import functools
import math

import jax
import jax.numpy as jnp
from jax import lax
from jax.experimental import pallas as pl
from jax.experimental.pallas import tpu as pltpu

D_MODEL = 2048
DEPTH = 2
GRID_W = 64
N_MOD = 6
EPS = 1e-6

HG_WIDTH = D_MODEL // 2
HG_HEAD_DIM = 128
HG_HEADS = HG_WIDTH // HG_HEAD_DIM
HG_CHUNK = 64
S5_WIDTH = D_MODEL - HG_WIDTH
S5_GROUP = 16
S5_GROUPS = S5_WIDTH // S5_GROUP
S5_STATE = 64

DN_QK_HEADS = 16
DN_V_HEADS = 32
DN_HEAD_DIM = 128
DN_QK = DN_QK_HEADS * DN_HEAD_DIM
DN_V = DN_V_HEADS * DN_HEAD_DIM
DN_QKV = 2 * DN_QK + DN_V
DN_CHUNK = 64

N_EXPERTS = 16
EC_CAPACITY = 2

LANES = 128
VMEM_LIMIT = 56 * 1024 * 1024

S5_LC = 16
S5_TILES = S5_WIDTH // LANES
S5_GPT = LANES // S5_GROUP
S5_ST = S5_GPT * S5_STATE
S5_CAT = S5_LC * LANES
S5_NSPLIT = 2

F32 = jnp.float32
BF16 = jnp.bfloat16


def _s5_weights(a_re, a_im, log_dt, b_re, b_im, c_re, c_im, d_skip, glu_w, glu_b):
    G, P, C, LC, NT, GT = S5_GROUPS, S5_STATE, S5_GROUP, S5_LC, S5_TILES, S5_GPT
    ar, ai = a_re.astype(F32), a_im.astype(F32)
    dt = jnp.exp(log_dt.astype(F32))[..., None]
    steps = jnp.arange(LC, dtype=F32)
    cmul = lambda x, y: (x[0] * y[0] - x[1] * y[1], x[0] * y[1] + x[1] * y[0])

    def apow(e):
        ex = e[:, None, None, None]
        mag = jnp.exp(ex * (ar * dt)[None])
        ang = ex * (ai * dt)[None]
        return mag * jnp.cos(ang), mag * jnp.sin(ang)

    one = apow(jnp.ones((1,), F32))
    nr, ni = one[0][0] - 1.0, one[1][0]
    den = ar * ar + ai * ai
    zoh = ((nr * ar + ni * ai) / den, (ni * ar - nr * ai) / den)
    bb = cmul((zoh[0][..., None], zoh[1][..., None]), (b_re.astype(F32), b_im.astype(F32)))
    cm = (c_re.astype(F32), c_im.astype(F32))
    eye = jnp.eye(GT, dtype=F32)
    hi = lax.Precision.HIGHEST
    at = lambda z, idx: (z[0][idx], z[1][idx])

    cp = cmul((cm[0][None], cm[1][None]), at(apow(steps), (slice(None), slice(None), slice(None), None)))
    kk = (jnp.einsum('tdgcp,dgpk->dtgck', cp[0], bb[0], precision=hi)
          - jnp.einsum('tdgcp,dgpk->dtgck', cp[1], bb[1], precision=hi))
    s_in = jnp.arange(LC)[:, None, None]
    s_out = jnp.arange(LC)[None, :, None]
    lag = jnp.arange(LC)[None, None, :]
    sel_f = (s_out - s_in == lag).astype(F32)
    sel_b = (s_in - s_out == lag).astype(F32)
    toe = (jnp.einsum('abt,tgok->abgok', sel_f, kk[0], precision=hi)
           + jnp.einsum('abt,tgok->abgok', sel_b, kk[1], precision=hi))
    toe = toe.reshape(LC, LC, NT, GT, C, C).transpose(2, 0, 3, 5, 1, 4)
    toe = toe[:, :, :, :, :, None, :] * eye[None, None, :, None, None, :, None]
    toe = toe.reshape(NT, S5_CAT, S5_CAT)

    def inject(e):
        er = jnp.stack(e, 0)
        er = er.reshape(2, LC, NT, GT, P, C).transpose(2, 1, 3, 5, 0, 4)
        m = er[:, :, :, :, :, None, :] * eye[None, None, :, None, None, :, None]
        return m.reshape(NT, S5_CAT, 2 * S5_ST)

    def pw_dir(e, d):
        z = apow(e)
        return z[0][:, d][..., None], z[1][:, d][..., None]

    inj_f = inject(cmul(pw_dir(LC - 1 - steps, 0), (bb[0][0][None], bb[1][0][None])))
    inj_b = inject(cmul(pw_dir(steps, 1), (bb[0][1][None], bb[1][1][None])))

    def readout(w):
        wr = jnp.stack([w[0], -w[1]], 0)
        wr = wr.reshape(2, LC, NT, GT, C, P).transpose(2, 0, 3, 5, 1, 4)
        n = wr[:, :, :, :, :, None, :] * eye[None, None, :, None, None, :, None]
        return n.reshape(NT, 2 * S5_ST, S5_CAT)

    def pw_row(e, d):
        z = apow(e)
        return z[0][:, d][:, :, None, :], z[1][:, d][:, :, None, :]

    rd_f = readout(cmul((cm[0][0][None], cm[1][0][None]), pw_row(steps + 1.0, 0)))
    rd_b = readout(cmul((cm[0][1][None], cm[1][1][None]), pw_row(LC - steps, 1)))

    w1 = jnp.concatenate([toe, inj_f, inj_b], axis=-1).astype(BF16)
    w2 = jnp.concatenate([rd_f, rd_b], axis=1).astype(BF16)
    a_lc = apow(jnp.full((1,), LC, F32))
    a16 = jnp.stack([a_lc[0][0, 0], a_lc[1][0, 0], a_lc[0][0, 1], a_lc[1][0, 1]], 0)
    a16 = a16.reshape(4, NT, S5_ST).transpose(1, 0, 2)
    gw = glu_w.astype(F32).reshape(NT, GT, C, C)
    gw = (gw[:, :, :, None, :] * eye[None, :, None, :, None]).reshape(NT, LANES, LANES).astype(BF16)
    dk = d_skip.astype(F32).reshape(NT, 1, LANES)
    gb = glu_b.astype(F32).reshape(NT, 1, LANES)
    return w1, w2, a16, dk, gw, gb


def _s5_inject_kernel(u_ref, w_ref, r_ref, ucat_ref, *, nb, nch):
    @pl.when(pl.program_id(1) == 0)
    def _():
        for b in range(nb):
            for s in range(S5_LC):
                ucat_ref[b * nch:(b + 1) * nch, s * LANES:(s + 1) * LANES] = (
                    u_ref[b, pl.ds(s, nch, stride=S5_LC), :].astype(BF16))

    r = jnp.dot(ucat_ref[...], w_ref[...], preferred_element_type=F32)
    for b in range(nb):
        r_ref[b] = r[b * nch:(b + 1) * nch]


def _s5_scan_kernel(s_ref, a_ref, x_ref, *, nch, nctx):
    st = S5_ST
    arf, aif, arb, aib = a_ref[0:1, :], a_ref[1:2, :], a_ref[2:3, :], a_ref[3:4, :]

    def body(i, carry):
        fr, fi, br, bi = carry
        kf = i
        kb = jnp.where(i < nctx, nctx - 1 - i, nch + nctx - 1 - i)
        x_ref[pl.ds(kf, 1), 0:st] = fr
        x_ref[pl.ds(kf, 1), st:2 * st] = fi
        x_ref[pl.ds(kb, 1), 2 * st:3 * st] = br
        x_ref[pl.ds(kb, 1), 3 * st:4 * st] = bi
        sfr = s_ref[pl.ds(kf, 1), 0:st]
        sfi = s_ref[pl.ds(kf, 1), st:2 * st]
        sbr = s_ref[pl.ds(kb, 1), 2 * st:3 * st]
        sbi = s_ref[pl.ds(kb, 1), 3 * st:4 * st]
        return (arf * fr - aif * fi + sfr, arf * fi + aif * fr + sfi,
                arb * br - aib * bi + sbr, arb * bi + aib * br + sbi)

    z = jnp.zeros((1, st), F32)
    lax.fori_loop(0, nch, body, (z, z, z, z))


def _s5_readout_kernel(yi_ref, x_ref, w_ref, u_ref, d_ref, gw_ref, gb_ref, o_ref, *, nch):
    half = pl.program_id(2)
    y = yi_ref[...] + jnp.dot(x_ref[...].astype(BF16), w_ref[...], preferred_element_type=F32)
    per = S5_LC // S5_NSPLIT
    for sl in range(per):
        s = half * per + sl
        ys = y[:, sl * LANES:(sl + 1) * LANES] + d_ref[...] * u_ref[pl.ds(s, nch, stride=S5_LC), :]
        ys = jax.nn.gelu(ys)
        z = jnp.dot(ys.astype(BF16), gw_ref[...], preferred_element_type=F32) + gb_ref[...]
        o_ref[pl.ds(s, nch, stride=S5_LC), :] = ys * jax.nn.sigmoid(z)


def _s5_mixer(p, u_col0, L, weights):
    w1, w2, a16, dk, gw, gb = weights
    B, T, _ = p.shape
    nch = T // S5_LC
    nctx = L // S5_LC
    assert T % S5_LC == 0 and L % S5_LC == 0 and u_col0 % LANES == 0
    ucol = u_col0 // LANES
    n1 = w1.shape[-1]
    nblk1 = 1024
    cparams = functools.partial(pltpu.CompilerParams, vmem_limit_bytes=VMEM_LIMIT)

    r = pl.pallas_call(
        functools.partial(_s5_inject_kernel, nb=B, nch=nch),
        grid=(S5_TILES, n1 // nblk1),
        in_specs=[pl.BlockSpec((B, T, LANES), lambda j, n: (0, 0, ucol + j)),
                  pl.BlockSpec((None, S5_CAT, nblk1), lambda j, n: (j, 0, n))],
        out_specs=pl.BlockSpec((B, None, nch, nblk1), lambda j, n: (0, j, 0, n)),
        out_shape=jax.ShapeDtypeStruct((B, S5_TILES, nch, n1), F32),
        scratch_shapes=[pltpu.VMEM((B * nch, S5_CAT), BF16)],
        compiler_params=cparams(dimension_semantics=("arbitrary", "arbitrary")),
    )(p, w1)

    sblk = S5_CAT // (4 * S5_ST)
    assert S5_CAT % (4 * S5_ST) == 0
    xin = pl.pallas_call(
        functools.partial(_s5_scan_kernel, nch=nch, nctx=nctx),
        grid=(B, S5_TILES),
        in_specs=[pl.BlockSpec((None, None, nch, 4 * S5_ST), lambda b, j: (b, j, 0, sblk)),
                  pl.BlockSpec((None, 4, S5_ST), lambda b, j: (j, 0, 0))],
        out_specs=pl.BlockSpec((None, None, nch, 4 * S5_ST), lambda b, j: (b, j, 0, 0)),
        out_shape=jax.ShapeDtypeStruct((B, S5_TILES, nch, 4 * S5_ST), F32),
        compiler_params=cparams(dimension_semantics=("arbitrary", "arbitrary")),
    )(r, a16)

    ncol = S5_CAT // S5_NSPLIT
    return pl.pallas_call(
        functools.partial(_s5_readout_kernel, nch=nch),
        grid=(S5_TILES, B, S5_NSPLIT),
        in_specs=[pl.BlockSpec((None, None, nch, ncol), lambda j, b, h: (b, j, 0, h)),
                  pl.BlockSpec((None, None, nch, 4 * S5_ST), lambda j, b, h: (b, j, 0, 0)),
                  pl.BlockSpec((None, 4 * S5_ST, ncol), lambda j, b, h: (j, 0, h)),
                  pl.BlockSpec((None, T, LANES), lambda j, b, h: (b, 0, ucol + j)),
                  pl.BlockSpec((None, 1, LANES), lambda j, b, h: (j, 0, 0)),
                  pl.BlockSpec((None, LANES, LANES), lambda j, b, h: (j, 0, 0)),
                  pl.BlockSpec((None, 1, LANES), lambda j, b, h: (j, 0, 0))],
        out_specs=pl.BlockSpec((None, T, LANES), lambda j, b, h: (b, 0, j)),
        out_shape=jax.ShapeDtypeStruct((B, T, S5_WIDTH), F32),
        compiler_params=cparams(dimension_semantics=("arbitrary", "arbitrary", "arbitrary")),
    )(r, xin, w2, p, dk, gw, gb)


def _rmsnorm(x, g):
    xf = x.astype(F32)
    y = xf * lax.rsqrt(jnp.mean(xf * xf, axis=-1, keepdims=True) + EPS)
    return (y * g.astype(F32)).astype(x.dtype)


def _modulate(h, shift, scale):
    return h * (1 + scale) + shift


DN_SC = 256
DN_NC = DN_SC // DN_CHUNK
DN_REP = DN_V_HEADS // DN_QK_HEADS
DN_LEVELS = DN_CHUNK.bit_length() - 1
DN_SCAN_HG = 4


def _dn_masks():
    out = []
    for reverse in (False, True):
        i = jnp.arange(DN_SC)[:, None]
        j = jnp.arange(DN_SC)[None, :]
        same = (i // DN_CHUNK) == (j // DN_CHUNK)
        if reverse:
            i, j = j, i
        ms = [same & (i >= j), same & (i > j)]
        for lv in range(DN_LEVELS):
            ms.append(same & ((i >> (lv + 1)) == (j >> (lv + 1))) & (((i >> lv) & 1) == 1) & (((j >> lv) & 1) == 0))
        out.append(jnp.stack(ms))
    return jnp.stack(out).astype(F32)


def _lane_pick(x, col):
    hot = (lax.broadcasted_iota(jnp.int32, (1, LANES), 1) == col).astype(F32)
    return jnp.sum(x * hot, axis=1, keepdims=True)


def _dn_prep_kernel(q_ref, k_ref, v_ref, g_ref, m_ref, uwf_ref, uwb_ref, qkf_ref, qkb_ref, gc_ref,
                    gcs_ref, gct_ref):
    C, D = DN_CHUNK, DN_HEAD_DIM
    hq = pl.program_id(2)
    gb = g_ref[...]

    @pl.when(hq == 0)
    def _():
        lane = lax.broadcasted_iota(jnp.int32, (1, LANES), 1)
        out = gb
        for d in range(2):
            cs = jnp.dot(m_ref[d, 0], gb, preferred_element_type=F32, precision=lax.Precision.HIGHEST)
            gcs_ref[d] = cs
            gct_ref[d] = cs.T
            lo = 2 * DN_V_HEADS + d * DN_V_HEADS
            out = jnp.where((lane >= lo) & (lane < lo + DN_V_HEADS), cs, out)
        gc_ref[...] = out

    q = q_ref[...]
    k = k_ref[...]
    kk = lax.dot_general(k, k, (((1,), (1,)), ((), ())), preferred_element_type=F32)
    qk = lax.dot_general(q, k, (((1,), (1,)), ((), ())), preferred_element_type=F32)
    kf = k.astype(F32)
    uw_refs = (uwf_ref, uwb_ref)
    qk_refs = (qkf_ref, qkb_ref)
    insts = [(d, r) for d in range(2) for r in range(DN_REP)]
    bcol, gcol, dec, a_mat, m = {}, {}, {}, {}, {}
    for i in insts:
        d, r = i
        hv = hq * DN_REP + r
        incl = m_ref[d, 0]
        strict = m_ref[d, 1]
        bcol[i] = _lane_pick(gb, d * DN_V_HEADS + hv)
        gcol[i] = _lane_pick(gcs_ref[d], 2 * DN_V_HEADS + d * DN_V_HEADS + hv)
        grow = gct_ref[d, pl.ds(2 * DN_V_HEADS + d * DN_V_HEADS + hv, 1), :]
        dec[i] = incl * jnp.exp(jnp.where(incl > 0, gcol[i] - grow, 0.0))
        a_mat[i] = strict * (bcol[i] * (kk * dec[i]))
        m[i] = (incl - strict) - a_mat[i] * m_ref[d, 2]
    for lv in range(1, DN_LEVELS):
        mb = {i: m[i].astype(BF16) for i in insts}
        x = {i: jnp.dot(mb[i], (a_mat[i] * m_ref[i[0], 2 + lv]).astype(BF16), preferred_element_type=F32)
             for i in insts}
        for i in insts:
            m[i] = m[i] - jnp.dot(x[i].astype(BF16), mb[i], preferred_element_type=F32)
    for i in insts:
        d, r = i
        egc = jnp.exp(gcol[i])
        v = v_ref[:, r * D:(r + 1) * D].astype(F32)
        rhs = jnp.concatenate([(v * bcol[i]).astype(BF16), (kf * (bcol[i] * egc)).astype(BF16)], axis=1)
        uw = jnp.dot(m[i].astype(BF16), rhs, preferred_element_type=F32)
        uw_refs[d][:, r * 2 * D:(r + 1) * 2 * D] = uw.astype(BF16)
    for d in range(2):
        parts = []
        for r in range(DN_REP):
            qd = qk * dec[(d, r)]
            parts.append(jnp.concatenate([qd[c * C:(c + 1) * C, c * C:(c + 1) * C] for c in range(DN_NC)],
                                         axis=0))
        qk_refs[d][...] = jnp.concatenate(parts, axis=1).astype(BF16)


def _dn_scan_kernel(uw_ref, qkd_ref, q_ref, k_ref, gc_ref, o_ref, s_ref, *, reverse):
    C, D = DN_CHUNK, DN_HEAD_DIM
    dirn = 1 if reverse else 0
    hgrp = pl.program_id(1)
    last = 0 if reverse else C - 1

    @pl.when(pl.program_id(2) == 0)
    def _():
        s_ref[...] = jnp.zeros_like(s_ref)

    heads = range(DN_SCAN_HG * DN_REP)
    for ci in range(DN_NC):
        c = (DN_NC - 1 - ci) if reverse else ci
        r0 = c * C
        gcb = gc_ref[r0:r0 + C, :]
        gcol, g_end, s_old, ws = {}, {}, {}, {}
        for hl in heads:
            hq = hl // DN_REP
            hv = hgrp * (DN_SCAN_HG * DN_REP) + hl
            gcol[hl] = _lane_pick(gcb, 2 * DN_V_HEADS + dirn * DN_V_HEADS + hv)
            g_end[hl] = gcol[hl][last:last + 1, :]
            qf = q_ref[r0:r0 + C, hq * D:(hq + 1) * D].astype(F32)
            w = uw_ref[r0:r0 + C, hl * 2 * D + D:(hl + 1) * 2 * D]
            s_old[hl] = s_ref[hl]
            wq = jnp.concatenate([w, (qf * jnp.exp(gcol[hl])).astype(BF16)], axis=0)
            ws[hl] = jnp.dot(wq, s_old[hl].astype(BF16), preferred_element_type=F32)
        for hl in heads:
            hq = hl // DN_REP
            u = uw_ref[r0:r0 + C, hl * 2 * D:hl * 2 * D + D].astype(F32)
            vnb = (u - ws[hl][:C]).astype(BF16)
            qkd = qkd_ref[r0:r0 + C, hl * C:(hl + 1) * C]
            o_ref[r0:r0 + C, hl * D:(hl + 1) * D] = ws[hl][C:] + jnp.dot(qkd, vnb, preferred_element_type=F32)
            kf = k_ref[r0:r0 + C, hq * D:(hq + 1) * D].astype(F32)
            k_dec_t = (kf * jnp.exp(g_end[hl] - gcol[hl])).T.astype(BF16)
            s_ref[hl] = s_old[hl] * jnp.exp(g_end[hl]) + jnp.dot(k_dec_t, vnb, preferred_element_type=F32)


def _delta_rule(qkv, gates, L):
    B, T, _ = qkv.shape
    nsc = T // DN_SC
    assert T % DN_SC == 0 and L == DN_SC and DN_QK_HEADS % DN_SCAN_HG == 0
    D, C = DN_HEAD_DIM, DN_CHUNK
    masks = _dn_masks()
    cp = functools.partial(pltpu.CompilerParams, vmem_limit_bytes=VMEM_LIMIT)
    blk = lambda w, off=0: pl.BlockSpec((None, DN_SC, w), lambda b, s, h: (b, s, off // w + h))
    uwf, uwb, qkf, qkb, gc = pl.pallas_call(
        _dn_prep_kernel,
        grid=(B, nsc, DN_QK_HEADS),
        in_specs=[blk(D), blk(D, DN_QK), blk(DN_REP * D, 2 * DN_QK),
                  pl.BlockSpec((None, DN_SC, LANES), lambda b, s, h: (b, s, 0)),
                  pl.BlockSpec(masks.shape, lambda b, s, h: (0, 0, 0, 0))],
        out_specs=[blk(DN_REP * 2 * D), blk(DN_REP * 2 * D), blk(DN_REP * C), blk(DN_REP * C),
                   pl.BlockSpec((None, DN_SC, LANES), lambda b, s, h: (b, s, 0))],
        out_shape=[jax.ShapeDtypeStruct((B, T, DN_V_HEADS * 2 * D), BF16)] * 2
        + [jax.ShapeDtypeStruct((B, T, DN_V_HEADS * C), BF16)] * 2
        + [jax.ShapeDtypeStruct((B, T, LANES), F32)],
        scratch_shapes=[pltpu.VMEM((2, DN_SC, LANES), F32), pltpu.VMEM((2, LANES, DN_SC), F32)],
        compiler_params=cp(dimension_semantics=("arbitrary", "arbitrary", "arbitrary")),
    )(qkv, qkv, qkv, gates, masks)

    outs = []
    G = DN_SCAN_HG
    for reverse, uw, qkd in ((False, uwf, qkf), (True, uwb, qkb)):
        if reverse:
            sc_of = lambda i: jnp.where(i == 0, 0, nsc - i)
        else:
            sc_of = lambda i: i
        sblk = lambda w, off=0, f=sc_of: pl.BlockSpec((None, DN_SC, w), lambda b, h, i: (b, f(i), off // w + h))
        outs.append(pl.pallas_call(
            functools.partial(_dn_scan_kernel, reverse=reverse),
            grid=(B, DN_QK_HEADS // G, nsc),
            in_specs=[sblk(G * DN_REP * 2 * D), sblk(G * DN_REP * C), sblk(G * D), sblk(G * D, DN_QK),
                      pl.BlockSpec((None, DN_SC, LANES), lambda b, h, i, f=sc_of: (b, f(i), 0))],
            out_specs=sblk(G * DN_REP * D),
            out_shape=jax.ShapeDtypeStruct((B, T, DN_V), F32),
            scratch_shapes=[pltpu.VMEM((G * DN_REP, D, D), F32)],
            compiler_params=cp(dimension_semantics=("arbitrary", "arbitrary", "arbitrary")),
        )(uw, qkd, qkv, qkv, gc))
    return outs


HG_SC = DN_SC
HG_NC = HG_SC // HG_CHUNK
HG_GRP = 4
assert HG_CHUNK == DN_CHUNK


def _hg_scan_kernel(*refs, reverse, final):
    if final:
        q_ref, f_ref, i_ref, lb_ref, tri_ref, of_ref, g_ref, ng_ref, o_ref, st_ref = refs
    else:
        q_ref, f_ref, i_ref, lb_ref, tri_ref, o_ref, st_ref = refs
    C, D = HG_CHUNK, HG_HEAD_DIM
    mid = (C - 1 - (C // 2 - 1)) if reverse else (C // 2 - 1)
    last = 0 if reverse else C - 1

    @pl.when(pl.program_id(2) == 0)
    def _():
        st_ref[...] = jnp.zeros_like(st_ref)

    tri = tri_ref[...]
    tri_c = tri[0:C, 0:C]
    heads = range(HG_GRP)
    qs, ks, bs = {}, {}, {}
    for h in heads:
        cols = slice(h * D, (h + 1) * D)
        lb = lb_ref[:, cols]
        f = lb + (1.0 - lb) * jax.nn.sigmoid(f_ref[:, cols])
        ks[h] = 1.0 - f
        qs[h] = q_ref[:, cols]
        bs[h] = jnp.dot(tri, jnp.log(f), preferred_element_type=F32, precision=lax.Precision.HIGHEST)
    for ci in range(HG_NC):
        c = (HG_NC - 1 - ci) if reverse else ci
        r0 = c * C
        rows = slice(r0, r0 + C)
        att, qd, kd, vb, dl = {}, {}, {}, {}, {}
        for h in heads:
            b = bs[h][rows]
            ref = b[mid:mid + 1, :]
            b_last = b[last:last + 1, :]
            q = qs[h][rows]
            k = ks[h][rows]
            qa = (q * jnp.exp(b - ref)).astype(BF16)
            ka = (k * jnp.exp(ref - b)).astype(BF16)
            att[h] = lax.dot_general(qa, ka, (((1,), (1,)), ((), ())), preferred_element_type=F32) * tri_c
            qd[h] = (q * jnp.exp(b)).astype(BF16)
            kd[h] = (k * jnp.exp(b_last - b)).astype(BF16)
            dl[h] = jnp.exp(b_last)
            vb[h] = i_ref[rows, h * D:(h + 1) * D]
        for h in heads:
            cols = slice(h * D, (h + 1) * D)
            st = st_ref[h]
            v16 = vb[h].astype(BF16)
            o = (jnp.dot(att[h].astype(BF16), v16, preferred_element_type=F32)
                 + lax.dot_general(qd[h], st.astype(BF16), (((1,), (1,)), ((), ())), preferred_element_type=F32))
            st_ref[h] = st * dl[h] + jnp.dot(vb[h].T.astype(BF16), kd[h], preferred_element_type=F32)
            if final:
                o = o + of_ref[rows, cols]
                y = o * lax.rsqrt(jnp.mean(o * o, axis=-1, keepdims=True) + EPS) * ng_ref[...]
                o_ref[rows, cols] = (y * jax.nn.silu(g_ref[rows, cols])).astype(o_ref.dtype)
            else:
                o_ref[rows, cols] = o


def _hgrn2_mixer(p, L, lb, norm_g):
    B, T, _ = p.shape
    nsc = T // HG_SC
    assert T % HG_SC == 0 and L == HG_SC and HG_HEADS % HG_GRP == 0
    gw = HG_GRP * HG_HEAD_DIM
    per = HG_WIDTH // gw
    tri = _dn_masks()[:, 0]
    ng = norm_g.astype(F32).reshape(1, HG_HEAD_DIM)
    lbf = lb.astype(F32)
    cp = pltpu.CompilerParams(dimension_semantics=("arbitrary", "arbitrary", "arbitrary"),
                              vmem_limit_bytes=VMEM_LIMIT)
    o_prev = None
    for reverse in (False, True):
        d = 1 if reverse else 0
        if reverse:
            sc_of = lambda i: jnp.where(i == 0, 0, nsc - i)
        else:
            sc_of = lambda i: i
        col = lambda sec, f=sc_of: pl.BlockSpec((None, HG_SC, gw), lambda b, h, i: (b, f(i), sec * per + h))
        in_specs = [col(0), col(1 + d), col(3),
                    pl.BlockSpec((None, 1, gw), lambda b, h, i: (d, 0, h)),
                    pl.BlockSpec((None, HG_SC, HG_SC), lambda b, h, i: (d, 0, 0))]
        args = [p, p, p, lbf.reshape(2, 1, HG_WIDTH), tri]
        final = reverse
        if final:
            in_specs += [pl.BlockSpec((None, HG_SC, gw), lambda b, h, i, f=sc_of: (b, f(i), h)), col(4),
                         pl.BlockSpec((1, HG_HEAD_DIM), lambda b, h, i: (0, 0))]
            args += [o_prev, p, ng]
        o_prev = pl.pallas_call(
            functools.partial(_hg_scan_kernel, reverse=reverse, final=final),
            grid=(B, HG_HEADS // HG_GRP, nsc),
            in_specs=in_specs,
            out_specs=pl.BlockSpec((None, HG_SC, gw), lambda b, h, i, f=sc_of: (b, f(i), h)),
            out_shape=jax.ShapeDtypeStruct((B, T, HG_WIDTH), BF16 if final else F32),
            scratch_shapes=[pltpu.VMEM((HG_GRP, HG_HEAD_DIM, HG_HEAD_DIM), F32)],
            compiler_params=cp,
        )(*args)
    return o_prev


CV_CW = 512
CV_PAD = 8


def _dn_conv_kernel(xm_ref, xp_ref, xn_ref, w_ref, o_ref, xs_ref, *, nsc):
    s = pl.program_id(1)
    cb = pl.program_id(2)
    base = CV_PAD + GRID_W
    ext = DN_SC + 2 * GRID_W
    is_ctx = s == 0
    has_up = s > 1
    has_dn = jnp.logical_and(s > 0, s < nsc - 1)
    xs_ref[1, 0:CV_PAD, :] = jnp.zeros((CV_PAD, CV_CW), F32)
    xs_ref[1, CV_PAD + ext:, :] = jnp.zeros((CV_PAD, CV_CW), F32)
    xs_ref[1, CV_PAD:base, :] = jnp.where(has_up, xp_ref[...], 0.0)
    xs_ref[1, base:base + DN_SC, :] = xm_ref[...]
    xs_ref[1, base + DN_SC:base + DN_SC + GRID_W, :] = jnp.where(has_dn, xn_ref[...], 0.0)
    t = lax.broadcasted_iota(jnp.int32, (ext, 1), 0) - GRID_W
    pos = jnp.where(is_ctx, t, t % GRID_W)
    ok_lf = pos > 0
    ok_rt = pos < jnp.where(is_ctx, DN_SC - 1, GRID_W - 1)
    xs_ref[0, CV_PAD:CV_PAD + ext, :] = jnp.where(ok_lf, xs_ref[1, CV_PAD - 1:CV_PAD - 1 + ext, :], 0.0)
    xs_ref[2, CV_PAD:CV_PAD + ext, :] = jnp.where(ok_rt, xs_ref[1, CV_PAD + 1:CV_PAD + 1 + ext, :], 0.0)
    rows_on = jnp.where(is_ctx, 0.0, 1.0)
    acc = jnp.zeros((DN_SC, CV_CW), F32)
    for dr in (-1, 0, 1):
        for dc in (-1, 0, 1):
            k = (dr + 1) * 3 + (dc + 1)
            w = w_ref[k:k + 1, :]
            if dr != 0:
                w = w * rows_on
            start = base + dr * GRID_W
            acc = acc + xs_ref[dc + 1, start:start + DN_SC, :] * w
    y = jax.nn.silu(acc)
    nq = DN_QK // CV_CW
    scale = jnp.where(cb < nq, DN_HEAD_DIM ** -0.5, 1.0)
    is_qk = cb < 2 * nq
    for h in range(CV_CW // LANES):
        yh = y[:, h * LANES:(h + 1) * LANES]
        rs = lax.rsqrt(jnp.sum(yh * yh, axis=-1, keepdims=True) + EPS) * scale
        o_ref[:, h * LANES:(h + 1) * LANES] = (yh * jnp.where(is_qk, rs, 1.0)).astype(o_ref.dtype)


def _dn_gate_kernel(x_ref, na_ref, dtb_ref, o_ref):
    x = x_ref[...]
    lane = lax.broadcasted_iota(jnp.int32, (1, LANES), 1)
    o_ref[...] = jnp.where(lane < 2 * DN_V_HEADS, jax.nn.sigmoid(x),
                           na_ref[...] * jax.nn.softplus(x + dtb_ref[...]))


def _dn_inputs(p, L, conv_w, a_log, dt_bias):
    B, T, _ = p.shape
    nsc = T // DN_SC
    gpb = DN_SC // GRID_W
    nrow = T // GRID_W
    assert L == DN_SC and DN_QK % CV_CW == 0 and DN_QKV % CV_CW == 0 and DN_QKV % LANES == 0
    w9 = conv_w.astype(F32).reshape(9, DN_QKV)
    cp = functools.partial(pltpu.CompilerParams, vmem_limit_bytes=VMEM_LIMIT)
    qkv = pl.pallas_call(
        functools.partial(_dn_conv_kernel, nsc=nsc),
        grid=(B, nsc, DN_QKV // CV_CW),
        in_specs=[pl.BlockSpec((None, DN_SC, CV_CW), lambda b, s, c: (b, s, c)),
                  pl.BlockSpec((None, GRID_W, CV_CW), lambda b, s, c: (b, jnp.maximum(s * gpb - 1, 0), c)),
                  pl.BlockSpec((None, GRID_W, CV_CW), lambda b, s, c: (b, jnp.minimum(s * gpb + gpb, nrow - 1), c)),
                  pl.BlockSpec((9, CV_CW), lambda b, s, c: (0, c))],
        out_specs=pl.BlockSpec((None, DN_SC, CV_CW), lambda b, s, c: (b, s, c)),
        out_shape=jax.ShapeDtypeStruct((B, T, DN_QKV), BF16),
        scratch_shapes=[pltpu.VMEM((3, 2 * CV_PAD + 2 * GRID_W + DN_SC, CV_CW), F32)],
        compiler_params=cp(dimension_semantics=("arbitrary", "arbitrary", "arbitrary")),
    )(p, p, p, w9)

    gcol = (DN_QKV + DN_V) // LANES
    zeros = jnp.zeros((2 * DN_V_HEADS,), F32)
    na = jnp.concatenate([zeros, -jnp.exp(a_log.astype(F32)).reshape(-1)]).reshape(1, LANES)
    dtb = jnp.concatenate([zeros, dt_bias.astype(F32).reshape(-1)]).reshape(1, LANES)
    gates = pl.pallas_call(
        _dn_gate_kernel,
        grid=(B, nsc),
        in_specs=[pl.BlockSpec((None, DN_SC, LANES), lambda b, s: (b, s, gcol)),
                  pl.BlockSpec((1, LANES), lambda b, s: (0, 0)),
                  pl.BlockSpec((1, LANES), lambda b, s: (0, 0))],
        out_specs=pl.BlockSpec((None, DN_SC, LANES), lambda b, s: (b, s, 0)),
        out_shape=jax.ShapeDtypeStruct((B, T, LANES), F32),
        compiler_params=cp(dimension_semantics=("arbitrary", "arbitrary")),
    )(p, na, dtb)
    return qkv, gates


def _deltanet_mixer(h_all, L, w_in, conv_w, a_log, dt_bias, norm_g, w_out):
    p = h_all @ w_in
    B, T, _ = p.shape
    qkv, gates = _dn_inputs(p, L, conv_w, a_log, dt_bias)
    o_f, o_b = _delta_rule(qkv, gates, L)
    o = (o_f + o_b).reshape(B, T, DN_V_HEADS, DN_HEAD_DIM)
    z = p[..., DN_QKV:DN_QKV + DN_V].astype(F32)
    o = _rmsnorm(o, norm_g) * jax.nn.silu(z.reshape(B, T, DN_V_HEADS, DN_HEAD_DIM))
    return o.reshape(B, T, DN_V).astype(h_all.dtype) @ w_out


def _ec_moe(h, router_w, w_gate, w_up, w_down):
    B, n, D = h.shape
    cap = EC_CAPACITY * n // N_EXPERTS
    aff = jax.nn.softmax(jnp.einsum('bnd,de->bne', h, router_w).astype(F32), axis=-1)
    gate, idx = lax.top_k(jnp.swapaxes(aff, 1, 2), cap)
    xs = jax.vmap(lambda hb, ib: hb[ib])(h, idx)
    hid = jax.nn.silu(jnp.einsum('becd,edf->becf', xs, w_gate)) * jnp.einsum('becd,edf->becf', xs, w_up)
    y = jnp.einsum('becf,efd->becd', hid, w_down) * gate[..., None].astype(h.dtype)
    return jax.vmap(lambda ib, yb: jnp.zeros((n, D), yb.dtype).at[ib.reshape(-1)].add(yb.reshape(-1, D)))(idx, y)


def kernel(x, c, ctx, c_ctx, mod_w, mod_b, norm_g, ev_w_in, ev_w_out, hg_lb, hg_norm_g,
           s5_a_re, s5_a_im, s5_log_dt, s5_b_re, s5_b_im, s5_c_re, s5_c_im, s5_d, s5_glu_w, s5_glu_b,
           od_w_in, od_conv_w, od_a_log, od_dt_bias, od_norm_g, od_w_out,
           router_w, moe_w_gate, moe_w_up, moe_w_down):
    L = ctx.shape[1]
    N = x.shape[1]
    lb_all = jnp.cumsum(jax.nn.softmax(hg_lb.astype(F32), axis=0), axis=0)
    sc = jax.nn.silu(c)
    scc = jax.nn.silu(c_ctx)
    for layer in range(DEPTH):
        last = layer == DEPTH - 1
        j = layer // 2
        m_lat = jnp.split((sc @ mod_w[layer] + mod_b[layer])[:, None, :], N_MOD, axis=-1)
        m_ctx = jnp.split(scc @ mod_w[layer] + mod_b[layer], N_MOD, axis=-1)
        g_pre_mix, g_post_mix, g_pre_ffn, g_post_ffn = norm_g[layer]
        h_all = jnp.concatenate([_modulate(_rmsnorm(ctx, g_pre_mix), m_ctx[0], m_ctx[1]),
                                 _modulate(_rmsnorm(x, g_pre_mix), m_lat[0], m_lat[1])], axis=1)
        if layer % 2 == 0:
            p = h_all @ ev_w_in[j]
            o_hg = _hgrn2_mixer(p, L, lb_all[j], hg_norm_g[j])
            s5w = _s5_weights(s5_a_re[j], s5_a_im[j], s5_log_dt[j], s5_b_re[j], s5_b_im[j],
                              s5_c_re[j], s5_c_im[j], s5_d[j], s5_glu_w[j], s5_glu_b[j])
            o_s5 = _s5_mixer(p, 5 * HG_WIDTH, L, s5w)
            o = jnp.concatenate([o_hg.astype(F32), o_s5], axis=-1) @ ev_w_out[j]
        else:
            o = _deltanet_mixer(h_all, L, od_w_in[j], od_conv_w[j], od_a_log[j], od_dt_bias[j],
                                od_norm_g[j], od_w_out[j])
        o = _rmsnorm(o, g_post_mix)
        x = x + m_lat[2] * o[:, L:]
        if not last:
            ctx = ctx + m_ctx[2] * o[:, :L]
        h = _modulate(_rmsnorm(x, g_pre_ffn), m_lat[3], m_lat[4])
        x = x + m_lat[5] * _rmsnorm(_ec_moe(h, router_w[layer], moe_w_gate[layer], moe_w_up[layer],
                                            moe_w_down[layer]), g_post_ffn)
        if not last:
            hc = _modulate(_rmsnorm(ctx, g_pre_ffn), m_ctx[3], m_ctx[4])
            ctx = ctx + m_ctx[5] * _rmsnorm(_ec_moe(hc, router_w[layer], moe_w_gate[layer], moe_w_up[layer],
                                                    moe_w_down[layer]), g_post_ffn)
    return x
```

```python
import functools
import math

import jax
import jax.numpy as jnp
from jax import lax
from jax.experimental import pallas as pl
from jax.experimental.pallas import tpu as pltpu

D_MODEL = 2048
DEPTH = 2
GRID_W = 64
N_MOD = 6
EPS = 1e-6

HG_WIDTH = D_MODEL // 2
HG_HEAD_DIM = 128
HG_HEADS = HG_WIDTH // HG_HEAD_DIM
HG_CHUNK = 64
S5_WIDTH = D_MODEL - HG_WIDTH
S5_GROUP = 16
S5_GROUPS = S5_WIDTH // S5_GROUP
S5_STATE = 64

DN_QK_HEADS = 16
DN_V_HEADS = 32
DN_HEAD_DIM = 128
DN_QK = DN_QK_HEADS * DN_HEAD_DIM
DN_V = DN_V_HEADS * DN_HEAD_DIM
DN_QKV = 2 * DN_QK + DN_V
DN_CHUNK = 64

N_EXPERTS = 16
EC_CAPACITY = 2

LANES = 128
VMEM_LIMIT = 56 * 1024 * 1024

S5_LC = 16
S5_TILES = S5_WIDTH // LANES
S5_GPT = LANES // S5_GROUP
S5_ST = S5_GPT * S5_STATE
S5_CAT = S5_LC * LANES
S5_NSPLIT = 2

F32 = jnp.float32
BF16 = jnp.bfloat16


def _s5_weights(a_re, a_im, log_dt, b_re, b_im, c_re, c_im, d_skip, glu_w, glu_b):
    G, P, C, LC, NT, GT = S5_GROUPS, S5_STATE, S5_GROUP, S5_LC, S5_TILES, S5_GPT
    ar, ai = a_re.astype(F32), a_im.astype(F32)
    dt = jnp.exp(log_dt.astype(F32))[..., None]
    steps = jnp.arange(LC, dtype=F32)
    cmul = lambda x, y: (x[0] * y[0] - x[1] * y[1], x[0] * y[1] + x[1] * y[0])

    def apow(e):
        ex = e[:, None, None, None]
        mag = jnp.exp(ex * (ar * dt)[None])
        ang = ex * (ai * dt)[None]
        return mag * jnp.cos(ang), mag * jnp.sin(ang)

    one = apow(jnp.ones((1,), F32))
    nr, ni = one[0][0] - 1.0, one[1][0]
    den = ar * ar + ai * ai
    zoh = ((nr * ar + ni * ai) / den, (ni * ar - nr * ai) / den)
    bb = cmul((zoh[0][..., None], zoh[1][..., None]), (b_re.astype(F32), b_im.astype(F32)))
    cm = (c_re.astype(F32), c_im.astype(F32))
    eye = jnp.eye(GT, dtype=F32)
    hi = lax.Precision.HIGHEST
    at = lambda z, idx: (z[0][idx], z[1][idx])

    cp = cmul((cm[0][None], cm[1][None]), at(apow(steps), (slice(None), slice(None), slice(None), None)))
    kk = (jnp.einsum('tdgcp,dgpk->dtgck', cp[0], bb[0], precision=hi)
          - jnp.einsum('tdgcp,dgpk->dtgck', cp[1], bb[1], precision=hi))
    s_in = jnp.arange(LC)[:, None, None]
    s_out = jnp.arange(LC)[None, :, None]
    lag = jnp.arange(LC)[None, None, :]
    sel_f = (s_out - s_in == lag).astype(F32)
    sel_b = (s_in - s_out == lag).astype(F32)
    toe = (jnp.einsum('abt,tgok->abgok', sel_f, kk[0], precision=hi)
           + jnp.einsum('abt,tgok->abgok', sel_b, kk[1], precision=hi))
    toe = toe.reshape(LC, LC, NT, GT, C, C).transpose(2, 0, 3, 5, 1, 4)
    toe = toe[:, :, :, :, :, None, :] * eye[None, None, :, None, None, :, None]
    toe = toe.reshape(NT, S5_CAT, S5_CAT)

    def inject(e):
        er = jnp.stack(e, 0)
        er = er.reshape(2, LC, NT, GT, P, C).transpose(2, 1, 3, 5, 0, 4)
        m = er[:, :, :, :, :, None, :] * eye[None, None, :, None, None, :, None]
        return m.reshape(NT, S5_CAT, 2 * S5_ST)

    def pw_dir(e, d):
        z = apow(e)
        return z[0][:, d][..., None], z[1][:, d][..., None]

    inj_f = inject(cmul(pw_dir(LC - 1 - steps, 0), (bb[0][0][None], bb[1][0][None])))
    inj_b = inject(cmul(pw_dir(steps, 1), (bb[0][1][None], bb[1][1][None])))

    def readout(w):
        wr = jnp.stack([w[0], -w[1]], 0)
        wr = wr.reshape(2, LC, NT, GT, C, P).transpose(2, 0, 3, 5, 1, 4)
        n = wr[:, :, :, :, :, None, :] * eye[None, None, :, None, None, :, None]
        return n.reshape(NT, 2 * S5_ST, S5_CAT)

    def pw_row(e, d):
        z = apow(e)
        return z[0][:, d][:, :, None, :], z[1][:, d][:, :, None, :]

    rd_f = readout(cmul((cm[0][0][None], cm[1][0][None]), pw_row(steps + 1.0, 0)))
    rd_b = readout(cmul((cm[0][1][None], cm[1][1][None]), pw_row(LC - steps, 1)))

    w1 = jnp.concatenate([toe, inj_f, inj_b], axis=-1).astype(BF16)
    w2 = jnp.concatenate([rd_f, rd_b], axis=1).astype(BF16)
    a_lc = apow(jnp.full((1,), LC, F32))
    a16 = jnp.stack([a_lc[0][0, 0], a_lc[1][0, 0], a_lc[0][0, 1], a_lc[1][0, 1]], 0)
    a16 = a16.reshape(4, NT, S5_ST).transpose(1, 0, 2)
    gw = glu_w.astype(F32).reshape(NT, GT, C, C)
    gw = (gw[:, :, :, None, :] * eye[None, :, None, :, None]).reshape(NT, LANES, LANES).astype(BF16)
    dk = d_skip.astype(F32).reshape(NT, 1, LANES)
    gb = glu_b.astype(F32).reshape(NT, 1, LANES)
    return w1, w2, a16, dk, gw, gb


def _s5_inject_kernel(u_ref, w_ref, r_ref, ucat_ref, *, nb, nch):
    @pl.when(pl.program_id(1) == 0)
    def _():
        for b in range(nb):
            for s in range(S5_LC):
                ucat_ref[b * nch:(b + 1) * nch, s * LANES:(s + 1) * LANES] = (
                    u_ref[b, pl.ds(s, nch, stride=S5_LC), :].astype(BF16))

    r = jnp.dot(ucat_ref[...], w_ref[...], preferred_element_type=F32)
    for b in range(nb):
        r_ref[b] = r[b * nch:(b + 1) * nch]


def _s5_scan_kernel(s_ref, a_ref, x_ref, *, nch, nctx):
    st = S5_ST
    arf, aif, arb, aib = a_ref[0:1, :], a_ref[1:2, :], a_ref[2:3, :], a_ref[3:4, :]

    def body(i, carry):
        fr, fi, br, bi = carry
        kf = i
        kb = jnp.where(i < nctx, nctx - 1 - i, nch + nctx - 1 - i)
        x_ref[pl.ds(kf, 1), 0:st] = fr
        x_ref[pl.ds(kf, 1), st:2 * st] = fi
        x_ref[pl.ds(kb, 1), 2 * st:3 * st] = br
        x_ref[pl.ds(kb, 1), 3 * st:4 * st] = bi
        sfr = s_ref[pl.ds(kf, 1), 0:st]
        sfi = s_ref[pl.ds(kf, 1), st:2 * st]
        sbr = s_ref[pl.ds(kb, 1), 2 * st:3 * st]
        sbi = s_ref[pl.ds(kb, 1), 3 * st:4 * st]
        return (arf * fr - aif * fi + sfr, arf * fi + aif * fr + sfi,
                arb * br - aib * bi + sbr, arb * bi + aib * br + sbi)

    z = jnp.zeros((1, st), F32)
    lax.fori_loop(0, nch, body, (z, z, z, z))


def _s5_readout_kernel(yi_ref, x_ref, w_ref, u_ref, d_ref, gw_ref, gb_ref, o_ref, *, nch):
    half = pl.program_id(2)
    y = yi_ref[...] + jnp.dot(x_ref[...].astype(BF16), w_ref[...], preferred_element_type=F32)
    per = S5_LC // S5_NSPLIT
    for sl in range(per):
        s = half * per + sl
        ys = y[:, sl * LANES:(sl + 1) * LANES] + d_ref[...] * u_ref[pl.ds(s, nch, stride=S5_LC), :]
        ys = jax.nn.gelu(ys)
        z = jnp.dot(ys.astype(BF16), gw_ref[...], preferred_element_type=F32) + gb_ref[...]
        o_ref[pl.ds(s, nch, stride=S5_LC), :] = ys * jax.nn.sigmoid(z)


def _s5_mixer(p, u_col0, L, weights):
    w1, w2, a16, dk, gw, gb = weights
    B, T, _ = p.shape
    nch = T // S5_LC
    nctx = L // S5_LC
    assert T % S5_LC == 0 and L % S5_LC == 0 and u_col0 % LANES == 0
    ucol = u_col0 // LANES
    n1 = w1.shape[-1]
    nblk1 = 1024
    cparams = functools.partial(pltpu.CompilerParams, vmem_limit_bytes=VMEM_LIMIT)

    r = pl.pallas_call(
        functools.partial(_s5_inject_kernel, nb=B, nch=nch),
        grid=(S5_TILES, n1 // nblk1),
        in_specs=[pl.BlockSpec((B, T, LANES), lambda j, n: (0, 0, ucol + j)),
                  pl.BlockSpec((None, S5_CAT, nblk1), lambda j, n: (j, 0, n))],
        out_specs=pl.BlockSpec((B, None, nch, nblk1), lambda j, n: (0, j, 0, n)),
        out_shape=jax.ShapeDtypeStruct((B, S5_TILES, nch, n1), F32),
        scratch_shapes=[pltpu.VMEM((B * nch, S5_CAT), BF16)],
        compiler_params=cparams(dimension_semantics=("arbitrary", "arbitrary")),
    )(p, w1)

    sblk = S5_CAT // (4 * S5_ST)
    assert S5_CAT % (4 * S5_ST) == 0
    xin = pl.pallas_call(
        functools.partial(_s5_scan_kernel, nch=nch, nctx=nctx),
        grid=(B, S5_TILES),
        in_specs=[pl.BlockSpec((None, None, nch, 4 * S5_ST), lambda b, j: (b, j, 0, sblk)),
                  pl.BlockSpec((None, 4, S5_ST), lambda b, j: (j, 0, 0))],
        out_specs=pl.BlockSpec((None, None, nch, 4 * S5_ST), lambda b, j: (b, j, 0, 0)),
        out_shape=jax.ShapeDtypeStruct((B, S5_TILES, nch, 4 * S5_ST), F32),
        compiler_params=cparams(dimension_semantics=("arbitrary", "arbitrary")),
    )(r, a16)

    ncol = S5_CAT // S5_NSPLIT
    return pl.pallas_call(
        functools.partial(_s5_readout_kernel, nch=nch),
        grid=(S5_TILES, B, S5_NSPLIT),
        in_specs=[pl.BlockSpec((None, None, nch, ncol), lambda j, b, h: (b, j, 0, h)),
                  pl.BlockSpec((None, None, nch, 4 * S5_ST), lambda j, b, h: (b, j, 0, 0)),
                  pl.BlockSpec((None, 4 * S5_ST, ncol), lambda j, b, h: (j, 0, h)),
                  pl.BlockSpec((None, T, LANES), lambda j, b, h: (b, 0, ucol + j)),
                  pl.BlockSpec((None, 1, LANES), lambda j, b, h: (j, 0, 0)),
                  pl.BlockSpec((None, LANES, LANES), lambda j, b, h: (j, 0, 0)),
                  pl.BlockSpec((None, 1, LANES), lambda j, b, h: (j, 0, 0))],
        out_specs=pl.BlockSpec((None, T, LANES), lambda j, b, h: (b, 0, j)),
        out_shape=jax.ShapeDtypeStruct((B, T, S5_WIDTH), F32),
        compiler_params=cparams(dimension_semantics=("arbitrary", "arbitrary", "arbitrary")),
    )(r, xin, w2, p, dk, gw, gb)


DN_SC = 256
DN_NC = DN_SC // DN_CHUNK
DN_REP = DN_V_HEADS // DN_QK_HEADS
DN_LEVELS = DN_CHUNK.bit_length() - 1
DN_SCAN_HG = 4


def _dn_masks():
    out = []
    for reverse in (False, True):
        i = jnp.arange(DN_SC)[:, None]
        j = jnp.arange(DN_SC)[None, :]
        same = (i // DN_CHUNK) == (j // DN_CHUNK)
        if reverse:
            i, j = j, i
        ms = [same & (i >= j), same & (i > j)]
        for lv in range(DN_LEVELS):
            ms.append(same & ((i >> (lv + 1)) == (j >> (lv + 1))) & (((i >> lv) & 1) == 1) & (((j >> lv) & 1) == 0))
        out.append(jnp.stack(ms))
    return jnp.stack(out).astype(F32)


def _lane_pick(x, col):
    hot = (lax.broadcasted_iota(jnp.int32, (1, LANES), 1) == col).astype(F32)
    return jnp.sum(x * hot, axis=1, keepdims=True)


def _dn_prep_kernel(q_ref, k_ref, v_ref, g_ref, m_ref, uwf_ref, uwb_ref, qkf_ref, qkb_ref, gc_ref,
                    gcs_ref, gct_ref):
    C, D = DN_CHUNK, DN_HEAD_DIM
    hq = pl.program_id(2)
    gb = g_ref[...]

    @pl.when(hq == 0)
    def _():
        lane = lax.broadcasted_iota(jnp.int32, (1, LANES), 1)
        out = gb
        for d in range(2):
            cs = jnp.dot(m_ref[d, 0], gb, preferred_element_type=F32, precision=lax.Precision.HIGHEST)
            gcs_ref[d] = cs
            gct_ref[d] = cs.T
            lo = 2 * DN_V_HEADS + d * DN_V_HEADS
            out = jnp.where((lane >= lo) & (lane < lo + DN_V_HEADS), cs, out)
        gc_ref[...] = out

    q = q_ref[...]
    k = k_ref[...]
    kk = lax.dot_general(k, k, (((1,), (1,)), ((), ())), preferred_element_type=F32)
    qk = lax.dot_general(q, k, (((1,), (1,)), ((), ())), preferred_element_type=F32)
    kf = k.astype(F32)
    uw_refs = (uwf_ref, uwb_ref)
    qk_refs = (qkf_ref, qkb_ref)
    insts = [(d, r) for d in range(2) for r in range(DN_REP)]
    bcol, gcol, dec, a_mat, m = {}, {}, {}, {}, {}
    for i in insts:
        d, r = i
        hv = hq * DN_REP + r
        incl = m_ref[d, 0]
        strict = m_ref[d, 1]
        bcol[i] = _lane_pick(gb, d * DN_V_HEADS + hv)
        gcol[i] = _lane_pick(gcs_ref[d], 2 * DN_V_HEADS + d * DN_V_HEADS + hv)
        grow = gct_ref[d, pl.ds(2 * DN_V_HEADS + d * DN_V_HEADS + hv, 1), :]
        dec[i] = incl * jnp.exp(jnp.where(incl > 0, gcol[i] - grow, 0.0))
        a_mat[i] = strict * (bcol[i] * (kk * dec[i]))
        m[i] = (incl - strict) - a_mat[i] * m_ref[d, 2]
    for lv in range(1, DN_LEVELS):
        mb = {i: m[i].astype(BF16) for i in insts}
        x = {i: jnp.dot(mb[i], (a_mat[i] * m_ref[i[0], 2 + lv]).astype(BF16), preferred_element_type=F32)
             for i in insts}
        for i in insts:
            m[i] = m[i] - jnp.dot(x[i].astype(BF16), mb[i], preferred_element_type=F32)
    for i in insts:
        d, r = i
        egc = jnp.exp(gcol[i])
        v = v_ref[:, r * D:(r + 1) * D].astype(F32)
        rhs = jnp.concatenate([(v * bcol[i]).astype(BF16), (kf * (bcol[i] * egc)).astype(BF16)], axis=1)
        uw = jnp.dot(m[i].astype(BF16), rhs, preferred_element_type=F32)
        uw_refs[d][:, r * 2 * D:(r + 1) * 2 * D] = uw.astype(BF16)
    for d in range(2):
        parts = []
        for r in range(DN_REP):
            qd = qk * dec[(d, r)]
            parts.append(jnp.concatenate([qd[c * C:(c + 1) * C, c * C:(c + 1) * C] for c in range(DN_NC)],
                                         axis=0))
        qk_refs[d][...] = jnp.concatenate(parts, axis=1).astype(BF16)


def _dn_scan_kernel(uw_ref, qkd_ref, q_ref, k_ref, gc_ref, o_ref, s_ref, *, reverse):
    C, D = DN_CHUNK, DN_HEAD_DIM
    dirn = 1 if reverse else 0
    hgrp = pl.program_id(1)
    last = 0 if reverse else C - 1

    @pl.when(pl.program_id(2) == 0)
    def _():
        s_ref[...] = jnp.zeros_like(s_ref)

    heads = range(DN_SCAN_HG * DN_REP)
    for ci in range(DN_NC):
        c = (DN_NC - 1 - ci) if reverse else ci
        r0 = c * C
        gcb = gc_ref[r0:r0 + C, :]
        gcol, g_end, s_old, ws = {}, {}, {}, {}
        for hl in heads:
            hq = hl // DN_REP
            hv = hgrp * (DN_SCAN_HG * DN_REP) + hl
            gcol[hl] = _lane_pick(gcb, 2 * DN_V_HEADS + dirn * DN_V_HEADS + hv)
            g_end[hl] = gcol[hl][last:last + 1, :]
            qf = q_ref[r0:r0 + C, hq * D:(hq + 1) * D].astype(F32)
            w = uw_ref[r0:r0 + C, hl * 2 * D + D:(hl + 1) * 2 * D]
            s_old[hl] = s_ref[hl]
            wq = jnp.concatenate([w, (qf * jnp.exp(gcol[hl])).astype(BF16)], axis=0)
            ws[hl] = jnp.dot(wq, s_old[hl].astype(BF16), preferred_element_type=F32)
        for hl in heads:
            hq = hl // DN_REP
            u = uw_ref[r0:r0 + C, hl * 2 * D:hl * 2 * D + D].astype(F32)
            vnb = (u - ws[hl][:C]).astype(BF16)
            qkd = qkd_ref[r0:r0 + C, hl * C:(hl + 1) * C]
            o_ref[r0:r0 + C, hl * D:(hl + 1) * D] = ws[hl][C:] + jnp.dot(qkd, vnb, preferred_element_type=F32)
            kf = k_ref[r0:r0 + C, hq * D:(hq + 1) * D].astype(F32)
            k_dec_t = (kf * jnp.exp(g_end[hl] - gcol[hl])).T.astype(BF16)
            s_ref[hl] = s_old[hl] * jnp.exp(g_end[hl]) + jnp.dot(k_dec_t, vnb, preferred_element_type=F32)


def _delta_rule(qkv, gates, L):
    B, T, _ = qkv.shape
    nsc = T // DN_SC
    assert T % DN_SC == 0 and L == DN_SC and DN_QK_HEADS % DN_SCAN_HG == 0
    D, C = DN_HEAD_DIM, DN_CHUNK
    masks = _dn_masks()
    cp = functools.partial(pltpu.CompilerParams, vmem_limit_bytes=VMEM_LIMIT)
    blk = lambda w, off=0: pl.BlockSpec((None, DN_SC, w), lambda b, s, h: (b, s, off // w + h))
    uwf, uwb, qkf, qkb, gc = pl.pallas_call(
        _dn_prep_kernel,
        grid=(B, nsc, DN_QK_HEADS),
        in_specs=[blk(D), blk(D, DN_QK), blk(DN_REP * D, 2 * DN_QK),
                  pl.BlockSpec((None, DN_SC, LANES), lambda b, s, h: (b, s, 0)),
                  pl.BlockSpec(masks.shape, lambda b, s, h: (0, 0, 0, 0))],
        out_specs=[blk(DN_REP * 2 * D), blk(DN_REP * 2 * D), blk(DN_REP * C), blk(DN_REP * C),
                   pl.BlockSpec((None, DN_SC, LANES), lambda b, s, h: (b, s, 0))],
        out_shape=[jax.ShapeDtypeStruct((B, T, DN_V_HEADS * 2 * D), BF16)] * 2
        + [jax.ShapeDtypeStruct((B, T, DN_V_HEADS * C), BF16)] * 2
        + [jax.ShapeDtypeStruct((B, T, LANES), F32)],
        scratch_shapes=[pltpu.VMEM((2, DN_SC, LANES), F32), pltpu.VMEM((2, LANES, DN_SC), F32)],
        compiler_params=cp(dimension_semantics=("arbitrary", "arbitrary", "arbitrary")),
    )(qkv, qkv, qkv, gates, masks)

    outs = []
    G = DN_SCAN_HG
    for reverse, uw, qkd in ((False, uwf, qkf), (True, uwb, qkb)):
        if reverse:
            sc_of = lambda i: jnp.where(i == 0, 0, nsc - i)
        else:
            sc_of = lambda i: i
        sblk = lambda w, off=0, f=sc_of: pl.BlockSpec((None, DN_SC, w), lambda b, h, i: (b, f(i), off // w + h))
        outs.append(pl.pallas_call(
            functools.partial(_dn_scan_kernel, reverse=reverse),
            grid=(B, DN_QK_HEADS // G, nsc),
            in_specs=[sblk(G * DN_REP * 2 * D), sblk(G * DN_REP * C), sblk(G * D), sblk(G * D, DN_QK),
                      pl.BlockSpec((None, DN_SC, LANES), lambda b, h, i, f=sc_of: (b, f(i), 0))],
            out_specs=sblk(G * DN_REP * D),
            out_shape=jax.ShapeDtypeStruct((B, T, DN_V), F32),
            scratch_shapes=[pltpu.VMEM((G * DN_REP, D, D), F32)],
            compiler_params=cp(dimension_semantics=("arbitrary", "arbitrary", "arbitrary")),
        )(uw, qkd, qkv, qkv, gc))
    return outs


HG_SC = DN_SC
HG_NC = HG_SC // HG_CHUNK
HG_GRP = 4
assert HG_CHUNK == DN_CHUNK


def _hg_scan_kernel(*refs, reverse, final):
    if final:
        q_ref, f_ref, i_ref, lb_ref, tri_ref, of_ref, g_ref, ng_ref, o_ref, st_ref = refs
    else:
        q_ref, f_ref, i_ref, lb_ref, tri_ref, o_ref, st_ref = refs
    C, D = HG_CHUNK, HG_HEAD_DIM
    mid = (C - 1 - (C // 2 - 1)) if reverse else (C // 2 - 1)
    last = 0 if reverse else C - 1

    @pl.when(pl.program_id(2) == 0)
    def _():
        st_ref[...] = jnp.zeros_like(st_ref)

    tri = tri_ref[...]
    tri_c = tri[0:C, 0:C]
    heads = range(HG_GRP)
    qs, ks, bs = {}, {}, {}
    for h in heads:
        cols = slice(h * D, (h + 1) * D)
        lb = lb_ref[:, cols]
        f = lb + (1.0 - lb) * jax.nn.sigmoid(f_ref[:, cols])
        ks[h] = 1.0 - f
        qs[h] = q_ref[:, cols]
        bs[h] = jnp.dot(tri, jnp.log(f), preferred_element_type=F32, precision=lax.Precision.HIGHEST)
    for ci in range(HG_NC):
        c = (HG_NC - 1 - ci) if reverse else ci
        r0 = c * C
        rows = slice(r0, r0 + C)
        att, qd, kd, vb, dl = {}, {}, {}, {}, {}
        for h in heads:
            b = bs[h][rows]
            ref = b[mid:mid + 1, :]
            b_last = b[last:last + 1, :]
            q = qs[h][rows]
            k = ks[h][rows]
            qa = (q * jnp.exp(b - ref)).astype(BF16)
            ka = (k * jnp.exp(ref - b)).astype(BF16)
            att[h] = lax.dot_general(qa, ka, (((1,), (1,)), ((), ())), preferred_element_type=F32) * tri_c
            qd[h] = (q * jnp.exp(b)).astype(BF16)
            kd[h] = (k * jnp.exp(b_last - b)).astype(BF16)
            dl[h] = jnp.exp(b_last)
            vb[h] = i_ref[rows, h * D:(h + 1) * D]
        for h in heads:
            cols = slice(h * D, (h + 1) * D)
            st = st_ref[h]
            v16 = vb[h].astype(BF16)
            o = (jnp.dot(att[h].astype(BF16), v16, preferred_element_type=F32)
                 + lax.dot_general(qd[h], st.astype(BF16), (((1,), (1,)), ((), ())), preferred_element_type=F32))
            st_ref[h] = st * dl[h] + jnp.dot(vb[h].T.astype(BF16), kd[h], preferred_element_type=F32)
            if final:
                o = o + of_ref[rows, cols]
                y = o * lax.rsqrt(jnp.mean(o * o, axis=-1, keepdims=True) + EPS) * ng_ref[...]
                o_ref[rows, cols] = (y * jax.nn.silu(g_ref[rows, cols])).astype(o_ref.dtype)
            else:
                o_ref[rows, cols] = o


def _hgrn2_mixer(p, L, lb, norm_g):
    B, T, _ = p.shape
    nsc = T // HG_SC
    assert T % HG_SC == 0 and L == HG_SC and HG_HEADS % HG_GRP == 0
    gw = HG_GRP * HG_HEAD_DIM
    per = HG_WIDTH // gw
    tri = _dn_masks()[:, 0]
    ng = norm_g.astype(F32).reshape(1, HG_HEAD_DIM)
    lbf = lb.astype(F32)
    cp = pltpu.CompilerParams(dimension_semantics=("arbitrary", "arbitrary", "arbitrary"),
                              vmem_limit_bytes=VMEM_LIMIT)
    o_prev = None
    for reverse in (False, True):
        d = 1 if reverse else 0
        if reverse:
            sc_of = lambda i: jnp.where(i == 0, 0, nsc - i)
        else:
            sc_of = lambda i: i
        col = lambda sec, f=sc_of: pl.BlockSpec((None, HG_SC, gw), lambda b, h, i: (b, f(i), sec * per + h))
        in_specs = [col(0), col(1 + d), col(3),
                    pl.BlockSpec((None, 1, gw), lambda b, h, i: (d, 0, h)),
                    pl.BlockSpec((None, HG_SC, HG_SC), lambda b, h, i: (d, 0, 0))]
        args = [p, p, p, lbf.reshape(2, 1, HG_WIDTH), tri]
        final = reverse
        if final:
            in_specs += [pl.BlockSpec((None, HG_SC, gw), lambda b, h, i, f=sc_of: (b, f(i), h)), col(4),
                         pl.BlockSpec((1, HG_HEAD_DIM), lambda b, h, i: (0, 0))]
            args += [o_prev, p, ng]
        o_prev = pl.pallas_call(
            functools.partial(_hg_scan_kernel, reverse=reverse, final=final),
            grid=(B, HG_HEADS // HG_GRP, nsc),
            in_specs=in_specs,
            out_specs=pl.BlockSpec((None, HG_SC, gw), lambda b, h, i, f=sc_of: (b, f(i), h)),
            out_shape=jax.ShapeDtypeStruct((B, T, HG_WIDTH), BF16 if final else F32),
            scratch_shapes=[pltpu.VMEM((HG_GRP, HG_HEAD_DIM, HG_HEAD_DIM), F32)],
            compiler_params=cp,
        )(*args)
    return o_prev


CV_CW = 512
CV_PAD = 8


def _dn_conv_kernel(xm_ref, xp_ref, xn_ref, w_ref, o_ref, xs_ref, *, nsc):
    s = pl.program_id(1)
    cb = pl.program_id(2)
    base = CV_PAD + GRID_W
    ext = DN_SC + 2 * GRID_W
    is_ctx = s == 0
    has_up = s > 1
    has_dn = jnp.logical_and(s > 0, s < nsc - 1)
    xs_ref[1, 0:CV_PAD, :] = jnp.zeros((CV_PAD, CV_CW), F32)
    xs_ref[1, CV_PAD + ext:, :] = jnp.zeros((CV_PAD, CV_CW), F32)
    xs_ref[1, CV_PAD:base, :] = jnp.where(has_up, xp_ref[...], 0.0)
    xs_ref[1, base:base + DN_SC, :] = xm_ref[...]
    xs_ref[1, base + DN_SC:base + DN_SC + GRID_W, :] = jnp.where(has_dn, xn_ref[...], 0.0)
    t = lax.broadcasted_iota(jnp.int32, (ext, 1), 0) - GRID_W
    pos = jnp.where(is_ctx, t, t % GRID_W)
    ok_lf = pos > 0
    ok_rt = pos < jnp.where(is_ctx, DN_SC - 1, GRID_W - 1)
    xs_ref[0, CV_PAD:CV_PAD + ext, :] = jnp.where(ok_lf, xs_ref[1, CV_PAD - 1:CV_PAD - 1 + ext, :], 0.0)
    xs_ref[2, CV_PAD:CV_PAD + ext, :] = jnp.where(ok_rt, xs_ref[1, CV_PAD + 1:CV_PAD + 1 + ext, :], 0.0)
    rows_on = jnp.where(is_ctx, 0.0, 1.0)
    acc = jnp.zeros((DN_SC, CV_CW), F32)
    for dr in (-1, 0, 1):
        for dc in (-1, 0, 1):
            k = (dr + 1) * 3 + (dc + 1)
            w = w_ref[k:k + 1, :]
            if dr != 0:
                w = w * rows_on
            start = base + dr * GRID_W
            acc = acc + xs_ref[dc + 1, start:start + DN_SC, :] * w
    y = jax.nn.silu(acc)
    nq = DN_QK // CV_CW
    scale = jnp.where(cb < nq, DN_HEAD_DIM ** -0.5, 1.0)
    is_qk = cb < 2 * nq
    for h in range(CV_CW // LANES):
        yh = y[:, h * LANES:(h + 1) * LANES]
        rs = lax.rsqrt(jnp.sum(yh * yh, axis=-1, keepdims=True) + EPS) * scale
        o_ref[:, h * LANES:(h + 1) * LANES] = (yh * jnp.where(is_qk, rs, 1.0)).astype(o_ref.dtype)


def _dn_gate_kernel(x_ref, na_ref, dtb_ref, o_ref):
    x = x_ref[...]
    lane = lax.broadcasted_iota(jnp.int32, (1, LANES), 1)
    o_ref[...] = jnp.where(lane < 2 * DN_V_HEADS, jax.nn.sigmoid(x),
                           na_ref[...] * jax.nn.softplus(x + dtb_ref[...]))


def _dn_inputs(p, pg, L, conv_w, a_log, dt_bias):
    B, T, _ = p.shape
    nsc = T // DN_SC
    gpb = DN_SC // GRID_W
    nrow = T // GRID_W
    assert L == DN_SC and DN_QK % CV_CW == 0 and DN_QKV % CV_CW == 0 and DN_QKV % LANES == 0
    w9 = conv_w.astype(F32).reshape(9, DN_QKV)
    cp = functools.partial(pltpu.CompilerParams, vmem_limit_bytes=VMEM_LIMIT)
    qkv = pl.pallas_call(
        functools.partial(_dn_conv_kernel, nsc=nsc),
        grid=(B, nsc, DN_QKV // CV_CW),
        in_specs=[pl.BlockSpec((None, DN_SC, CV_CW), lambda b, s, c: (b, s, c)),
                  pl.BlockSpec((None, GRID_W, CV_CW), lambda b, s, c: (b, jnp.maximum(s * gpb - 1, 0), c)),
                  pl.BlockSpec((None, GRID_W, CV_CW), lambda b, s, c: (b, jnp.minimum(s * gpb + gpb, nrow - 1), c)),
                  pl.BlockSpec((9, CV_CW), lambda b, s, c: (0, c))],
        out_specs=pl.BlockSpec((None, DN_SC, CV_CW), lambda b, s, c: (b, s, c)),
        out_shape=jax.ShapeDtypeStruct((B, T, DN_QKV), BF16),
        scratch_shapes=[pltpu.VMEM((3, 2 * CV_PAD + 2 * GRID_W + DN_SC, CV_CW), F32)],
        compiler_params=cp(dimension_semantics=("arbitrary", "arbitrary", "arbitrary")),
    )(p, p, p, w9)

    zeros = jnp.zeros((2 * DN_V_HEADS,), F32)
    na = jnp.concatenate([zeros, -jnp.exp(a_log.astype(F32)).reshape(-1)]).reshape(1, LANES)
    dtb = jnp.concatenate([zeros, dt_bias.astype(F32).reshape(-1)]).reshape(1, LANES)
    gates = pl.pallas_call(
        _dn_gate_kernel,
        grid=(B, nsc),
        in_specs=[pl.BlockSpec((None, DN_SC, LANES), lambda b, s: (b, s, 0)),
                  pl.BlockSpec((1, LANES), lambda b, s: (0, 0)),
                  pl.BlockSpec((1, LANES), lambda b, s: (0, 0))],
        out_specs=pl.BlockSpec((None, DN_SC, LANES), lambda b, s: (b, s, 0)),
        out_shape=jax.ShapeDtypeStruct((B, T, LANES), F32),
        compiler_params=cp(dimension_semantics=("arbitrary", "arbitrary")),
    )(pg, na, dtb)
    return qkv, gates


PJ_TM = 768
PJ_TN = 512
PO_TM = 384
PO_TK = 1024


def _row_mod(mods_ref, slot, row0, nrows, n_ctx):
    is_ctx = (row0 + lax.broadcasted_iota(jnp.int32, (nrows, 1), 0)) < n_ctx
    return jnp.where(is_ctx, mods_ref[0, slot:slot + 1, :], mods_ref[1, slot:slot + 1, :])


def _norm_mod(x, g, mods_ref, shift_slot, row0, n_ctx):
    y = x * lax.rsqrt(jnp.mean(x * x, axis=-1, keepdims=True) + EPS) * g
    n = x.shape[0]
    return y * (1.0 + _row_mod(mods_ref, shift_slot + 1, row0, n, n_ctx)) + _row_mod(mods_ref, shift_slot, row0, n, n_ctx)


def _in_proj_kernel(x_ref, g_ref, mods_ref, w_ref, o_ref, h_ref, *, slot, n_ctx):
    @pl.when(pl.program_id(2) == 0)
    def _():
        row0 = pl.program_id(1) * x_ref.shape[0]
        h_ref[...] = _norm_mod(x_ref[...], g_ref[...], mods_ref, slot, row0, n_ctx).astype(BF16)

    o_ref[...] = jnp.dot(h_ref[...], w_ref[...], preferred_element_type=F32)


def _in_proj(xs, g, mods, w, slot, n_ctx, tn=PJ_TN):
    B, T, D = xs.shape
    N = w.shape[1]
    assert T % PJ_TM == 0 and N % tn == 0
    return pl.pallas_call(
        functools.partial(_in_proj_kernel, slot=slot, n_ctx=n_ctx),
        grid=(B, T // PJ_TM, N // tn),
        in_specs=[pl.BlockSpec((None, PJ_TM, D), lambda b, i, n: (b, i, 0)),
                  pl.BlockSpec((1, D), lambda b, i, n: (0, 0)),
                  pl.BlockSpec((None, 2, N_MOD, D), lambda b, i, n: (b, 0, 0, 0)),
                  pl.BlockSpec((D, tn), lambda b, i, n: (0, n))],
        out_specs=pl.BlockSpec((None, PJ_TM, tn), lambda b, i, n: (b, i, n)),
        out_shape=jax.ShapeDtypeStruct((B, T, N), F32),
        scratch_shapes=[pltpu.VMEM((PJ_TM, D), BF16)],
        compiler_params=pltpu.CompilerParams(dimension_semantics=("arbitrary", "arbitrary", "arbitrary"),
                                             vmem_limit_bytes=VMEM_LIMIT),
    )(xs, g.astype(F32).reshape(1, D), mods, w)


def _norm_mod_kernel(x_ref, g_ref, mods_ref, o_ref, *, slot, n_ctx):
    row0 = pl.program_id(1) * x_ref.shape[0]
    o_ref[...] = _norm_mod(x_ref[...], g_ref[...], mods_ref, slot, row0, n_ctx).astype(o_ref.dtype)


def _norm_mod_rows(xs, g, mods, slot, n_ctx):
    B, T, D = xs.shape
    return pl.pallas_call(
        functools.partial(_norm_mod_kernel, slot=slot, n_ctx=n_ctx),
        grid=(B, T // PJ_TM),
        in_specs=[pl.BlockSpec((None, PJ_TM, D), lambda b, i: (b, i, 0)),
                  pl.BlockSpec((1, D), lambda b, i: (0, 0)),
                  pl.BlockSpec((None, 2, N_MOD, D), lambda b, i: (b, 0, 0, 0))],
        out_specs=pl.BlockSpec((None, PJ_TM, D), lambda b, i: (b, i, 0)),
        out_shape=jax.ShapeDtypeStruct((B, T, D), BF16),
        compiler_params=pltpu.CompilerParams(dimension_semantics=("arbitrary", "arbitrary"),
                                             vmem_limit_bytes=VMEM_LIMIT),
    )(xs, g.astype(F32).reshape(1, D), mods)


def _residual(x, o, g, mods_ref, slot, row0, n_ctx):
    on = o * lax.rsqrt(jnp.mean(o * o, axis=-1, keepdims=True) + EPS) * g
    return x + _row_mod(mods_ref, slot, row0, x.shape[0], n_ctx) * on


def _out_proj_even_kernel(a_ref, b_ref, w_ref, x_ref, g_ref, mods_ref, o_ref, *, slot, n_ctx):
    lhs = jnp.concatenate([a_ref[...], b_ref[...].astype(BF16)], axis=1)
    o = jnp.dot(lhs, w_ref[...], preferred_element_type=F32)
    row0 = pl.program_id(1) * x_ref.shape[0]
    o_ref[...] = _residual(x_ref[...], o, g_ref[...], mods_ref, slot, row0, n_ctx)


def _out_proj_even(o_hg, o_s5, w, xs, g, mods, slot, n_ctx):
    B, T, D = xs.shape
    assert T % PO_TM == 0
    row = lambda wd: pl.BlockSpec((None, PO_TM, wd), lambda b, i: (b, i, 0))
    return pl.pallas_call(
        functools.partial(_out_proj_even_kernel, slot=slot, n_ctx=n_ctx),
        grid=(B, T // PO_TM),
        in_specs=[row(HG_WIDTH), row(S5_WIDTH), pl.BlockSpec((D, D), lambda b, i: (0, 0)), row(D),
                  pl.BlockSpec((1, D), lambda b, i: (0, 0)),
                  pl.BlockSpec((None, 2, N_MOD, D), lambda b, i: (b, 0, 0, 0))],
        out_specs=row(D),
        out_shape=jax.ShapeDtypeStruct((B, T, D), F32),
        compiler_params=pltpu.CompilerParams(dimension_semantics=("arbitrary", "arbitrary"),
                                             vmem_limit_bytes=VMEM_LIMIT),
    )(o_hg, o_s5, w, xs, g.astype(F32).reshape(1, D), mods)


def _out_proj_odd_kernel(of_ref, ob_ref, z_ref, ng_ref, w_ref, x_ref, g_ref, mods_ref, o_ref, acc_ref, *, slot, n_ctx):
    k = pl.program_id(2)

    @pl.when(k == 0)
    def _():
        acc_ref[...] = jnp.zeros_like(acc_ref)

    parts = []
    for h in range(of_ref.shape[1] // DN_HEAD_DIM):
        cols = slice(h * DN_HEAD_DIM, (h + 1) * DN_HEAD_DIM)
        o = of_ref[:, cols] + ob_ref[:, cols]
        y = o * lax.rsqrt(jnp.mean(o * o, axis=-1, keepdims=True) + EPS) * ng_ref[...]
        parts.append((y * jax.nn.silu(z_ref[:, cols])).astype(BF16))
    acc_ref[...] += jnp.dot(jnp.concatenate(parts, axis=1), w_ref[...], preferred_element_type=F32)

    @pl.when(k == pl.num_programs(2) - 1)
    def _():
        row0 = pl.program_id(1) * x_ref.shape[0]
        o_ref[...] = _residual(x_ref[...], acc_ref[...], g_ref[...], mods_ref, slot, row0, n_ctx)


def _out_proj_odd(o_f, o_b, p, z_col0, norm_g, w, xs, g, mods, slot, n_ctx):
    B, T, D = xs.shape
    assert T % PO_TM == 0 and DN_V % PO_TK == 0 and z_col0 % PO_TK == 0
    zb = z_col0 // PO_TK
    return pl.pallas_call(
        functools.partial(_out_proj_odd_kernel, slot=slot, n_ctx=n_ctx),
        grid=(B, T // PO_TM, DN_V // PO_TK),
        in_specs=[pl.BlockSpec((None, PO_TM, PO_TK), lambda b, i, k: (b, i, k)),
                  pl.BlockSpec((None, PO_TM, PO_TK), lambda b, i, k: (b, i, k)),
                  pl.BlockSpec((None, PO_TM, PO_TK), lambda b, i, k: (b, i, zb + k)),
                  pl.BlockSpec((1, DN_HEAD_DIM), lambda b, i, k: (0, 0)),
                  pl.BlockSpec((PO_TK, D), lambda b, i, k: (k, 0)),
                  pl.BlockSpec((None, PO_TM, D), lambda b, i, k: (b, i, 0)),
                  pl.BlockSpec((1, D), lambda b, i, k: (0, 0)),
                  pl.BlockSpec((None, 2, N_MOD, D), lambda b, i, k: (b, 0, 0, 0))],
        out_specs=pl.BlockSpec((None, PO_TM, D), lambda b, i, k: (b, i, 0)),
        out_shape=jax.ShapeDtypeStruct((B, T, D), F32),
        scratch_shapes=[pltpu.VMEM((PO_TM, D), F32)],
        compiler_params=pltpu.CompilerParams(dimension_semantics=("arbitrary", "arbitrary", "arbitrary"),
                                             vmem_limit_bytes=VMEM_LIMIT),
    )(o_f, o_b, p, norm_g.astype(F32).reshape(1, DN_HEAD_DIM), w, xs, g.astype(F32).reshape(1, D), mods)


def _post_residual_kernel(x_ref, y_ref, g_ref, mods_ref, o_ref, *, slot, n_ctx):
    row0 = pl.program_id(1) * x_ref.shape[0]
    o_ref[...] = _residual(x_ref[...], y_ref[...], g_ref[...], mods_ref, slot, row0, n_ctx)


def _post_residual(xs, y, g, mods, slot, n_ctx):
    B, T, D = xs.shape
    row = pl.BlockSpec((None, PJ_TM, D), lambda b, i: (b, i, 0))
    return pl.pallas_call(
        functools.partial(_post_residual_kernel, slot=slot, n_ctx=n_ctx),
        grid=(B, T // PJ_TM),
        in_specs=[row, row, pl.BlockSpec((1, D), lambda b, i: (0, 0)),
                  pl.BlockSpec((None, 2, N_MOD, D), lambda b, i: (b, 0, 0, 0))],
        out_specs=row,
        out_shape=jax.ShapeDtypeStruct((B, T, D), F32),
        compiler_params=pltpu.CompilerParams(dimension_semantics=("arbitrary", "arbitrary"),
                                             vmem_limit_bytes=VMEM_LIMIT),
    )(xs, y, g.astype(F32).reshape(1, D), mods)


def _ec_moe(h, router_w, w_gate, w_up, w_down):
    B, n, D = h.shape
    cap = EC_CAPACITY * n // N_EXPERTS
    aff = jax.nn.softmax(jnp.einsum('bnd,de->bne', h, router_w).astype(F32), axis=-1)
    gate, idx = lax.top_k(jnp.swapaxes(aff, 1, 2), cap)
    xs = jax.vmap(lambda hb, ib: hb[ib])(h, idx)
    hid = jax.nn.silu(jnp.einsum('becd,edf->becf', xs, w_gate)) * jnp.einsum('becd,edf->becf', xs, w_up)
    y = jnp.einsum('becf,efd->becd', hid, w_down) * gate[..., None]
    return jax.vmap(lambda ib, yb: jnp.zeros((n, D), yb.dtype).at[ib.reshape(-1)].add(yb.reshape(-1, D)))(idx, y)


def kernel(x, c, ctx, c_ctx, mod_w, mod_b, norm_g, ev_w_in, ev_w_out, hg_lb, hg_norm_g,
           s5_a_re, s5_a_im, s5_log_dt, s5_b_re, s5_b_im, s5_c_re, s5_c_im, s5_d, s5_glu_w, s5_glu_b,
           od_w_in, od_conv_w, od_a_log, od_dt_bias, od_norm_g, od_w_out,
           router_w, moe_w_gate, moe_w_up, moe_w_down):
    B, L, D = ctx.shape
    lb_all = jnp.cumsum(jax.nn.softmax(hg_lb.astype(F32), axis=0), axis=0)
    sc = jax.nn.silu(c)
    scc = jax.nn.silu(c_ctx)
    xs = jnp.concatenate([ctx, x], axis=1)
    for layer in range(DEPTH):
        last = layer == DEPTH - 1
        j = layer // 2
        m_lat = (sc @ mod_w[layer] + mod_b[layer]).reshape(B, N_MOD, D)
        m_ctx = (scc @ mod_w[layer] + mod_b[layer]).reshape(N_MOD, D)
        mods = jnp.stack([jnp.broadcast_to(m_ctx, (B, N_MOD, D)), m_lat], axis=1)
        g_pre_mix, g_post_mix, g_pre_ffn, g_post_ffn = norm_g[layer]
        if layer % 2 == 0:
            p = _in_proj(xs, g_pre_mix, mods, ev_w_in[j].astype(BF16), 0, L)
            o_hg = _hgrn2_mixer(p, L, lb_all[j], hg_norm_g[j])
            s5w = _s5_weights(s5_a_re[j], s5_a_im[j], s5_log_dt[j], s5_b_re[j], s5_b_im[j],
                              s5_c_re[j], s5_c_im[j], s5_d[j], s5_glu_w[j], s5_glu_b[j])
            o_s5 = _s5_mixer(p, 5 * HG_WIDTH, L, s5w)
            xs = _out_proj_even(o_hg, o_s5, ev_w_out[j].astype(BF16), xs, g_post_mix, mods, 2, L)
        else:
            w_in = od_w_in[j].astype(BF16)
            p = _in_proj(xs, g_pre_mix, mods, w_in[:, :DN_QKV + DN_V], 0, L)
            pg = _in_proj(xs, g_pre_mix, mods, w_in[:, DN_QKV + DN_V:], 0, L, tn=LANES)
            qkv, gates = _dn_inputs(p, pg, L, od_conv_w[j], od_a_log[j], od_dt_bias[j])
            o_f, o_b = _delta_rule(qkv, gates, L)
            xs = _out_proj_odd(o_f, o_b, p, DN_QKV, od_norm_g[j], od_w_out[j].astype(BF16), xs, g_post_mix,
                               mods, 2, L)
        h = _norm_mod_rows(xs, g_pre_ffn, mods, 3, L)
        moe = functools.partial(_ec_moe, router_w=router_w[layer], w_gate=moe_w_gate[layer],
                                w_up=moe_w_up[layer], w_down=moe_w_down[layer])
        y_ctx = jnp.zeros((B, L, D), F32) if last else moe(h[:, :L])
        y = jnp.concatenate([y_ctx, moe(h[:, L:])], axis=1)
        xs = _post_residual(xs, y, g_post_ffn, mods, 5, L)
    return xs[:, L:]
```

```python
import functools
import math

import jax
import jax.numpy as jnp
from jax import lax
from jax.experimental import pallas as pl
from jax.experimental.pallas import tpu as pltpu

D_MODEL = 2048
DEPTH = 2
GRID_W = 64
N_MOD = 6
EPS = 1e-6

HG_WIDTH = D_MODEL // 2
HG_HEAD_DIM = 128
HG_HEADS = HG_WIDTH // HG_HEAD_DIM
HG_CHUNK = 64
S5_WIDTH = D_MODEL - HG_WIDTH
S5_GROUP = 16
S5_GROUPS = S5_WIDTH // S5_GROUP
S5_STATE = 64

DN_QK_HEADS = 16
DN_V_HEADS = 32
DN_HEAD_DIM = 128
DN_QK = DN_QK_HEADS * DN_HEAD_DIM
DN_V = DN_V_HEADS * DN_HEAD_DIM
DN_QKV = 2 * DN_QK + DN_V
DN_CHUNK = 64

N_EXPERTS = 16
EC_CAPACITY = 2

LANES = 128
VMEM_LIMIT = 56 * 1024 * 1024

S5_LC = 16
S5_TILES = S5_WIDTH // LANES
S5_GPT = LANES // S5_GROUP
S5_ST = S5_GPT * S5_STATE
S5_CAT = S5_LC * LANES
S5_NSPLIT = 2

F32 = jnp.float32
BF16 = jnp.bfloat16


def _s5_weights(a_re, a_im, log_dt, b_re, b_im, c_re, c_im, d_skip, glu_w, glu_b):
    G, P, C, LC, NT, GT = S5_GROUPS, S5_STATE, S5_GROUP, S5_LC, S5_TILES, S5_GPT
    ar, ai = a_re.astype(F32), a_im.astype(F32)
    dt = jnp.exp(log_dt.astype(F32))[..., None]
    steps = jnp.arange(LC, dtype=F32)
    cmul = lambda x, y: (x[0] * y[0] - x[1] * y[1], x[0] * y[1] + x[1] * y[0])

    def apow(e):
        ex = e[:, None, None, None]
        mag = jnp.exp(ex * (ar * dt)[None])
        ang = ex * (ai * dt)[None]
        return mag * jnp.cos(ang), mag * jnp.sin(ang)

    one = apow(jnp.ones((1,), F32))
    nr, ni = one[0][0] - 1.0, one[1][0]
    den = ar * ar + ai * ai
    zoh = ((nr * ar + ni * ai) / den, (ni * ar - nr * ai) / den)
    bb = cmul((zoh[0][..., None], zoh[1][..., None]), (b_re.astype(F32), b_im.astype(F32)))
    cm = (c_re.astype(F32), c_im.astype(F32))
    hi = lax.Precision.HIGHEST
    at = lambda z, idx: (z[0][idx], z[1][idx])
    lane_grp = jnp.arange(LANES) // C

    def spread(a, width):
        keep = lane_grp[:, None] == (jnp.arange(GT * width) // width)[None, :]
        return jnp.where(keep, jnp.concatenate([a] * GT, axis=-1), 0.0)

    cp = cmul((cm[0][None], cm[1][None]), at(apow(steps), (slice(None), slice(None), slice(None), None)))
    kk = (jnp.einsum('tdgcp,dgpk->dtgck', cp[0], bb[0], precision=hi)
          - jnp.einsum('tdgcp,dgpk->dtgck', cp[1], bb[1], precision=hi))
    bd = spread(jnp.swapaxes(kk, -1, -2).reshape(2, LC, NT, LANES, C), C).astype(BF16)
    s_in = jnp.arange(LC)[:, None, None]
    s_out = jnp.arange(LC)[None, :, None]
    lag = jnp.arange(LC)[None, None, :]
    sel = jnp.concatenate([s_out - s_in == lag, s_in - s_out == lag], axis=-1).astype(BF16)
    toe = jnp.einsum('abt,tjrc->jarbc', sel, bd.reshape(2 * LC, NT, LANES, LANES), preferred_element_type=F32)
    toe = toe.astype(BF16).reshape(NT, S5_CAT, S5_CAT)

    def inject(e):
        er = jnp.stack(e, 0).reshape(2, LC, NT, GT, P, C)
        er = jnp.moveaxis(er, -1, -2).reshape(2, LC, NT, LANES, P)
        m = spread(er, P).astype(BF16)
        m = jnp.concatenate([m[0], m[1]], axis=-1)
        return jnp.swapaxes(m, 0, 1).reshape(NT, S5_CAT, 2 * S5_ST)

    def pw_dir(e, d):
        z = apow(e)
        return z[0][:, d][..., None], z[1][:, d][..., None]

    inj_f = inject(cmul(pw_dir(LC - 1 - steps, 0), (bb[0][0][None], bb[1][0][None])))
    inj_b = inject(cmul(pw_dir(steps, 1), (bb[0][1][None], bb[1][1][None])))

    def readout(w):
        wr = jnp.stack([w[0], -w[1]], 0).reshape(2, LC, NT, LANES, P)
        n = spread(wr, P).astype(BF16)
        n = jnp.concatenate([n[0], n[1]], axis=-1)
        return jnp.transpose(n, (1, 3, 0, 2)).reshape(NT, 2 * S5_ST, S5_CAT)

    def pw_row(e, d):
        z = apow(e)
        return z[0][:, d][:, :, None, :], z[1][:, d][:, :, None, :]

    rd_f = readout(cmul((cm[0][0][None], cm[1][0][None]), pw_row(steps + 1.0, 0)))
    rd_b = readout(cmul((cm[0][1][None], cm[1][1][None]), pw_row(LC - steps, 1)))

    w1 = jnp.concatenate([toe, inj_f, inj_b], axis=-1)
    w2 = jnp.concatenate([rd_f, rd_b], axis=1)
    a_lc = apow(jnp.full((1,), LC, F32))
    a16 = jnp.stack([a_lc[0][0, 0], a_lc[1][0, 0], a_lc[0][0, 1], a_lc[1][0, 1]], 0)
    a16 = a16.reshape(4, NT, S5_ST).transpose(1, 0, 2)
    gw = spread(glu_w.astype(F32).reshape(NT, LANES, C), C).astype(BF16)
    dk = d_skip.astype(F32).reshape(NT, 1, LANES)
    gb = glu_b.astype(F32).reshape(NT, 1, LANES)
    return w1, w2, a16, dk, gw, gb


def _s5_inject_kernel(u_ref, w_ref, r_ref, ucat_ref, *, nb, nch):
    @pl.when(pl.program_id(1) == 0)
    def _():
        for b in range(nb):
            for s in range(S5_LC):
                ucat_ref[b * nch:(b + 1) * nch, s * LANES:(s + 1) * LANES] = (
                    u_ref[b, pl.ds(s, nch, stride=S5_LC), :].astype(BF16))

    r = jnp.dot(ucat_ref[...], w_ref[...], preferred_element_type=F32)
    for b in range(nb):
        r_ref[b] = r[b * nch:(b + 1) * nch]


def _s5_scan_kernel(s_ref, a_ref, x_ref, *, nch, nctx):
    st = S5_ST
    arf, aif, arb, aib = a_ref[0:1, :], a_ref[1:2, :], a_ref[2:3, :], a_ref[3:4, :]

    def body(i, carry):
        fr, fi, br, bi = carry
        kf = i
        kb = jnp.where(i < nctx, nctx - 1 - i, nch + nctx - 1 - i)
        x_ref[pl.ds(kf, 1), 0:st] = fr
        x_ref[pl.ds(kf, 1), st:2 * st] = fi
        x_ref[pl.ds(kb, 1), 2 * st:3 * st] = br
        x_ref[pl.ds(kb, 1), 3 * st:4 * st] = bi
        sfr = s_ref[pl.ds(kf, 1), 0:st]
        sfi = s_ref[pl.ds(kf, 1), st:2 * st]
        sbr = s_ref[pl.ds(kb, 1), 2 * st:3 * st]
        sbi = s_ref[pl.ds(kb, 1), 3 * st:4 * st]
        return (arf * fr - aif * fi + sfr, arf * fi + aif * fr + sfi,
                arb * br - aib * bi + sbr, arb * bi + aib * br + sbi)

    z = jnp.zeros((1, st), F32)
    lax.fori_loop(0, nch, body, (z, z, z, z))


def _s5_readout_kernel(yi_ref, x_ref, w_ref, u_ref, d_ref, gw_ref, gb_ref, o_ref, *, nch):
    half = pl.program_id(2)
    y = yi_ref[...] + jnp.dot(x_ref[...].astype(BF16), w_ref[...], preferred_element_type=F32)
    per = S5_LC // S5_NSPLIT
    for sl in range(per):
        s = half * per + sl
        ys = y[:, sl * LANES:(sl + 1) * LANES] + d_ref[...] * u_ref[pl.ds(s, nch, stride=S5_LC), :]
        ys = jax.nn.gelu(ys)
        z = jnp.dot(ys.astype(BF16), gw_ref[...], preferred_element_type=F32) + gb_ref[...]
        o_ref[pl.ds(s, nch, stride=S5_LC), :] = ys * jax.nn.sigmoid(z)


def _s5_mixer(p, u_col0, L, weights):
    w1, w2, a16, dk, gw, gb = weights
    B, T, _ = p.shape
    nch = T // S5_LC
    nctx = L // S5_LC
    assert T % S5_LC == 0 and L % S5_LC == 0 and u_col0 % LANES == 0
    ucol = u_col0 // LANES
    n1 = w1.shape[-1]
    nblk1 = 1024
    cparams = functools.partial(pltpu.CompilerParams, vmem_limit_bytes=VMEM_LIMIT)

    r = pl.pallas_call(
        functools.partial(_s5_inject_kernel, nb=B, nch=nch),
        grid=(S5_TILES, n1 // nblk1),
        in_specs=[pl.BlockSpec((B, T, LANES), lambda j, n: (0, 0, ucol + j)),
                  pl.BlockSpec((None, S5_CAT, nblk1), lambda j, n: (j, 0, n))],
        out_specs=pl.BlockSpec((B, None, nch, nblk1), lambda j, n: (0, j, 0, n)),
        out_shape=jax.ShapeDtypeStruct((B, S5_TILES, nch, n1), F32),
        scratch_shapes=[pltpu.VMEM((B * nch, S5_CAT), BF16)],
        compiler_params=cparams(dimension_semantics=("arbitrary", "arbitrary")),
    )(p, w1)

    sblk = S5_CAT // (4 * S5_ST)
    assert S5_CAT % (4 * S5_ST) == 0
    xin = pl.pallas_call(
        functools.partial(_s5_scan_kernel, nch=nch, nctx=nctx),
        grid=(B, S5_TILES),
        in_specs=[pl.BlockSpec((None, None, nch, 4 * S5_ST), lambda b, j: (b, j, 0, sblk)),
                  pl.BlockSpec((None, 4, S5_ST), lambda b, j: (j, 0, 0))],
        out_specs=pl.BlockSpec((None, None, nch, 4 * S5_ST), lambda b, j: (b, j, 0, 0)),
        out_shape=jax.ShapeDtypeStruct((B, S5_TILES, nch, 4 * S5_ST), F32),
        compiler_params=cparams(dimension_semantics=("arbitrary", "arbitrary")),
    )(r, a16)

    ncol = S5_CAT // S5_NSPLIT
    return pl.pallas_call(
        functools.partial(_s5_readout_kernel, nch=nch),
        grid=(S5_TILES, B, S5_NSPLIT),
        in_specs=[pl.BlockSpec((None, None, nch, ncol), lambda j, b, h: (b, j, 0, h)),
                  pl.BlockSpec((None, None, nch, 4 * S5_ST), lambda j, b, h: (b, j, 0, 0)),
                  pl.BlockSpec((None, 4 * S5_ST, ncol), lambda j, b, h: (j, 0, h)),
                  pl.BlockSpec((None, T, LANES), lambda j, b, h: (b, 0, ucol + j)),
                  pl.BlockSpec((None, 1, LANES), lambda j, b, h: (j, 0, 0)),
                  pl.BlockSpec((None, LANES, LANES), lambda j, b, h: (j, 0, 0)),
                  pl.BlockSpec((None, 1, LANES), lambda j, b, h: (j, 0, 0))],
        out_specs=pl.BlockSpec((None, T, LANES), lambda j, b, h: (b, 0, j)),
        out_shape=jax.ShapeDtypeStruct((B, T, S5_WIDTH), F32),
        compiler_params=cparams(dimension_semantics=("arbitrary", "arbitrary", "arbitrary")),
    )(r, xin, w2, p, dk, gw, gb)


DN_SC = 256
DN_NC = DN_SC // DN_CHUNK
DN_REP = DN_V_HEADS // DN_QK_HEADS
DN_LEVELS = DN_CHUNK.bit_length() - 1
DN_SCAN_HG = 4


def _dn_masks():
    out = []
    for reverse in (False, True):
        i = jnp.arange(DN_SC)[:, None]
        j = jnp.arange(DN_SC)[None, :]
        same = (i // DN_CHUNK) == (j // DN_CHUNK)
        if reverse:
            i, j = j, i
        ms = [same & (i >= j), same & (i > j)]
        for lv in range(DN_LEVELS):
            ms.append(same & ((i >> (lv + 1)) == (j >> (lv + 1))) & (((i >> lv) & 1) == 1) & (((j >> lv) & 1) == 0))
        out.append(jnp.stack(ms))
    return jnp.stack(out).astype(F32)


def _lane_pick(x, col):
    hot = (lax.broadcasted_iota(jnp.int32, (1, LANES), 1) == col).astype(F32)
    return jnp.sum(x * hot, axis=1, keepdims=True)


def _dn_prep_kernel(q_ref, k_ref, v_ref, g_ref, m_ref, uwf_ref, uwb_ref, qkf_ref, qkb_ref, gc_ref,
                    gcs_ref, gct_ref):
    C, D = DN_CHUNK, DN_HEAD_DIM
    hq = pl.program_id(2)
    gb = g_ref[...]

    @pl.when(hq == 0)
    def _():
        lane = lax.broadcasted_iota(jnp.int32, (1, LANES), 1)
        out = gb
        for d in range(2):
            cs = jnp.dot(m_ref[d, 0], gb, preferred_element_type=F32, precision=lax.Precision.HIGHEST)
            gcs_ref[d] = cs
            gct_ref[d] = cs.T
            lo = 2 * DN_V_HEADS + d * DN_V_HEADS
            out = jnp.where((lane >= lo) & (lane < lo + DN_V_HEADS), cs, out)
        gc_ref[...] = out

    q = q_ref[...]
    k = k_ref[...]
    kk = lax.dot_general(k, k, (((1,), (1,)), ((), ())), preferred_element_type=F32)
    qk = lax.dot_general(q, k, (((1,), (1,)), ((), ())), preferred_element_type=F32)
    kf = k.astype(F32)
    uw_refs = (uwf_ref, uwb_ref)
    qk_refs = (qkf_ref, qkb_ref)
    insts = [(d, r) for d in range(2) for r in range(DN_REP)]
    bcol, gcol, dec, a_mat, m = {}, {}, {}, {}, {}
    for i in insts:
        d, r = i
        hv = hq * DN_REP + r
        incl = m_ref[d, 0]
        strict = m_ref[d, 1]
        bcol[i] = _lane_pick(gb, d * DN_V_HEADS + hv)
        gcol[i] = _lane_pick(gcs_ref[d], 2 * DN_V_HEADS + d * DN_V_HEADS + hv)
        grow = gct_ref[d, pl.ds(2 * DN_V_HEADS + d * DN_V_HEADS + hv, 1), :]
        dec[i] = incl * jnp.exp(jnp.where(incl > 0, gcol[i] - grow, 0.0))
        a_mat[i] = strict * (bcol[i] * (kk * dec[i]))
        m[i] = (incl - strict) - a_mat[i] * m_ref[d, 2]
    for lv in range(1, DN_LEVELS):
        mb = {i: m[i].astype(BF16) for i in insts}
        x = {i: jnp.dot(mb[i], (a_mat[i] * m_ref[i[0], 2 + lv]).astype(BF16), preferred_element_type=F32)
             for i in insts}
        for i in insts:
            m[i] = m[i] - jnp.dot(x[i].astype(BF16), mb[i], preferred_element_type=F32)
    for i in insts:
        d, r = i
        egc = jnp.exp(gcol[i])
        v = v_ref[:, r * D:(r + 1) * D].astype(F32)
        rhs = jnp.concatenate([(v * bcol[i]).astype(BF16), (kf * (bcol[i] * egc)).astype(BF16)], axis=1)
        uw = jnp.dot(m[i].astype(BF16), rhs, preferred_element_type=F32)
        uw_refs[d][:, r * 2 * D:(r + 1) * 2 * D] = uw.astype(BF16)
    for d in range(2):
        parts = []
        for r in range(DN_REP):
            qd = qk * dec[(d, r)]
            parts.append(jnp.concatenate([qd[c * C:(c + 1) * C, c * C:(c + 1) * C] for c in range(DN_NC)],
                                         axis=0))
        qk_refs[d][...] = jnp.concatenate(parts, axis=1).astype(BF16)


def _dn_scan_kernel(uw_ref, qkd_ref, q_ref, k_ref, gc_ref, o_ref, s_ref, *, reverse):
    C, D = DN_CHUNK, DN_HEAD_DIM
    dirn = 1 if reverse else 0
    hgrp = pl.program_id(1)
    last = 0 if reverse else C - 1

    @pl.when(pl.program_id(2) == 0)
    def _():
        s_ref[...] = jnp.zeros_like(s_ref)

    heads = range(DN_SCAN_HG * DN_REP)
    for ci in range(DN_NC):
        c = (DN_NC - 1 - ci) if reverse else ci
        r0 = c * C
        gcb = gc_ref[r0:r0 + C, :]
        gcol, g_end, s_old, ws = {}, {}, {}, {}
        for hl in heads:
            hq = hl // DN_REP
            hv = hgrp * (DN_SCAN_HG * DN_REP) + hl
            gcol[hl] = _lane_pick(gcb, 2 * DN_V_HEADS + dirn * DN_V_HEADS + hv)
            g_end[hl] = gcol[hl][last:last + 1, :]
            qf = q_ref[r0:r0 + C, hq * D:(hq + 1) * D].astype(F32)
            w = uw_ref[r0:r0 + C, hl * 2 * D + D:(hl + 1) * 2 * D]
            s_old[hl] = s_ref[hl]
            wq = jnp.concatenate([w, (qf * jnp.exp(gcol[hl])).astype(BF16)], axis=0)
            ws[hl] = jnp.dot(wq, s_old[hl].astype(BF16), preferred_element_type=F32)
        for hl in heads:
            hq = hl // DN_REP
            u = uw_ref[r0:r0 + C, hl * 2 * D:hl * 2 * D + D].astype(F32)
            vnb = (u - ws[hl][:C]).astype(BF16)
            qkd = qkd_ref[r0:r0 + C, hl * C:(hl + 1) * C]
            o_ref[r0:r0 + C, hl * D:(hl + 1) * D] = ws[hl][C:] + jnp.dot(qkd, vnb, preferred_element_type=F32)
            kf = k_ref[r0:r0 + C, hq * D:(hq + 1) * D].astype(F32)
            k_dec_t = (kf * jnp.exp(g_end[hl] - gcol[hl])).T.astype(BF16)
            s_ref[hl] = s_old[hl] * jnp.exp(g_end[hl]) + jnp.dot(k_dec_t, vnb, preferred_element_type=F32)


def _delta_rule(qkv, gates, L):
    B, T, _ = qkv.shape
    nsc = T // DN_SC
    assert T % DN_SC == 0 and L == DN_SC and DN_QK_HEADS % DN_SCAN_HG == 0
    D, C = DN_HEAD_DIM, DN_CHUNK
    masks = _dn_masks()
    cp = functools.partial(pltpu.CompilerParams, vmem_limit_bytes=VMEM_LIMIT)
    blk = lambda w, off=0: pl.BlockSpec((None, DN_SC, w), lambda b, s, h: (b, s, off // w + h))
    uwf, uwb, qkf, qkb, gc = pl.pallas_call(
        _dn_prep_kernel,
        grid=(B, nsc, DN_QK_HEADS),
        in_specs=[blk(D), blk(D, DN_QK), blk(DN_REP * D, 2 * DN_QK),
                  pl.BlockSpec((None, DN_SC, LANES), lambda b, s, h: (b, s, 0)),
                  pl.BlockSpec(masks.shape, lambda b, s, h: (0, 0, 0, 0))],
        out_specs=[blk(DN_REP * 2 * D), blk(DN_REP * 2 * D), blk(DN_REP * C), blk(DN_REP * C),
                   pl.BlockSpec((None, DN_SC, LANES), lambda b, s, h: (b, s, 0))],
        out_shape=[jax.ShapeDtypeStruct((B, T, DN_V_HEADS * 2 * D), BF16)] * 2
        + [jax.ShapeDtypeStruct((B, T, DN_V_HEADS * C), BF16)] * 2
        + [jax.ShapeDtypeStruct((B, T, LANES), F32)],
        scratch_shapes=[pltpu.VMEM((2, DN_SC, LANES), F32), pltpu.VMEM((2, LANES, DN_SC), F32)],
        compiler_params=cp(dimension_semantics=("arbitrary", "arbitrary", "arbitrary")),
    )(qkv, qkv, qkv, gates, masks)

    outs = []
    G = DN_SCAN_HG
    for reverse, uw, qkd in ((False, uwf, qkf), (True, uwb, qkb)):
        if reverse:
            sc_of = lambda i: jnp.where(i == 0, 0, nsc - i)
        else:
            sc_of = lambda i: i
        sblk = lambda w, off=0, f=sc_of: pl.BlockSpec((None, DN_SC, w), lambda b, h, i: (b, f(i), off // w + h))
        outs.append(pl.pallas_call(
            functools.partial(_dn_scan_kernel, reverse=reverse),
            grid=(B, DN_QK_HEADS // G, nsc),
            in_specs=[sblk(G * DN_REP * 2 * D), sblk(G * DN_REP * C), sblk(G * D), sblk(G * D, DN_QK),
                      pl.BlockSpec((None, DN_SC, LANES), lambda b, h, i, f=sc_of: (b, f(i), 0))],
            out_specs=sblk(G * DN_REP * D),
            out_shape=jax.ShapeDtypeStruct((B, T, DN_V), F32),
            scratch_shapes=[pltpu.VMEM((G * DN_REP, D, D), F32)],
            compiler_params=cp(dimension_semantics=("arbitrary", "arbitrary", "arbitrary")),
        )(uw, qkd, qkv, qkv, gc))
    return outs


HG_SC = DN_SC
HG_NC = HG_SC // HG_CHUNK
HG_GRP = 4
assert HG_CHUNK == DN_CHUNK


def _hg_scan_kernel(*refs, reverse, final):
    if final:
        q_ref, f_ref, i_ref, lb_ref, tri_ref, of_ref, g_ref, ng_ref, o_ref, st_ref = refs
    else:
        q_ref, f_ref, i_ref, lb_ref, tri_ref, o_ref, st_ref = refs
    C, D = HG_CHUNK, HG_HEAD_DIM
    mid = (C - 1 - (C // 2 - 1)) if reverse else (C // 2 - 1)
    last = 0 if reverse else C - 1

    @pl.when(pl.program_id(2) == 0)
    def _():
        st_ref[...] = jnp.zeros_like(st_ref)

    tri = tri_ref[...]
    tri_c = tri[0:C, 0:C]
    heads = range(HG_GRP)
    qs, ks, bs = {}, {}, {}
    for h in heads:
        cols = slice(h * D, (h + 1) * D)
        lb = lb_ref[:, cols]
        f = lb + (1.0 - lb) * jax.nn.sigmoid(f_ref[:, cols])
        ks[h] = 1.0 - f
        qs[h] = q_ref[:, cols]
        bs[h] = jnp.dot(tri, jnp.log(f), preferred_element_type=F32, precision=lax.Precision.HIGHEST)
    for ci in range(HG_NC):
        c = (HG_NC - 1 - ci) if reverse else ci
        r0 = c * C
        rows = slice(r0, r0 + C)
        att, qd, kd, vb, dl = {}, {}, {}, {}, {}
        for h in heads:
            b = bs[h][rows]
            ref = b[mid:mid + 1, :]
            b_last = b[last:last + 1, :]
            q = qs[h][rows]
            k = ks[h][rows]
            qa = (q * jnp.exp(b - ref)).astype(BF16)
            ka = (k * jnp.exp(ref - b)).astype(BF16)
            att[h] = lax.dot_general(qa, ka, (((1,), (1,)), ((), ())), preferred_element_type=F32) * tri_c
            qd[h] = (q * jnp.exp(b)).astype(BF16)
            kd[h] = (k * jnp.exp(b_last - b)).astype(BF16)
            dl[h] = jnp.exp(b_last)
            vb[h] = i_ref[rows, h * D:(h + 1) * D]
        for h in heads:
            cols = slice(h * D, (h + 1) * D)
            st = st_ref[h]
            v16 = vb[h].astype(BF16)
            o = (jnp.dot(att[h].astype(BF16), v16, preferred_element_type=F32)
                 + lax.dot_general(qd[h], st.astype(BF16), (((1,), (1,)), ((), ())), preferred_element_type=F32))
            st_ref[h] = st * dl[h] + jnp.dot(vb[h].T.astype(BF16), kd[h], preferred_element_type=F32)
            if final:
                o = o + of_ref[rows, cols]
                y = o * lax.rsqrt(jnp.mean(o * o, axis=-1, keepdims=True) + EPS) * ng_ref[...]
                o_ref[rows, cols] = (y * jax.nn.silu(g_ref[rows, cols])).astype(o_ref.dtype)
            else:
                o_ref[rows, cols] = o


def _hgrn2_mixer(p, L, lb, norm_g):
    B, T, _ = p.shape
    nsc = T // HG_SC
    assert T % HG_SC == 0 and L == HG_SC and HG_HEADS % HG_GRP == 0
    gw = HG_GRP * HG_HEAD_DIM
    per = HG_WIDTH // gw
    tri = _dn_masks()[:, 0]
    ng = norm_g.astype(F32).reshape(1, HG_HEAD_DIM)
    lbf = lb.astype(F32)
    cp = pltpu.CompilerParams(dimension_semantics=("arbitrary", "arbitrary", "arbitrary"),
                              vmem_limit_bytes=VMEM_LIMIT)
    o_prev = None
    for reverse in (False, True):
        d = 1 if reverse else 0
        if reverse:
            sc_of = lambda i: jnp.where(i == 0, 0, nsc - i)
        else:
            sc_of = lambda i: i
        col = lambda sec, f=sc_of: pl.BlockSpec((None, HG_SC, gw), lambda b, h, i: (b, f(i), sec * per + h))
        in_specs = [col(0), col(1 + d), col(3),
                    pl.BlockSpec((None, 1, gw), lambda b, h, i: (d, 0, h)),
                    pl.BlockSpec((None, HG_SC, HG_SC), lambda b, h, i: (d, 0, 0))]
        args = [p, p, p, lbf.reshape(2, 1, HG_WIDTH), tri]
        final = reverse
        if final:
            in_specs += [pl.BlockSpec((None, HG_SC, gw), lambda b, h, i, f=sc_of: (b, f(i), h)), col(4),
                         pl.BlockSpec((1, HG_HEAD_DIM), lambda b, h, i: (0, 0))]
            args += [o_prev, p, ng]
        o_prev = pl.pallas_call(
            functools.partial(_hg_scan_kernel, reverse=reverse, final=final),
            grid=(B, HG_HEADS // HG_GRP, nsc),
            in_specs=in_specs,
            out_specs=pl.BlockSpec((None, HG_SC, gw), lambda b, h, i, f=sc_of: (b, f(i), h)),
            out_shape=jax.ShapeDtypeStruct((B, T, HG_WIDTH), BF16 if final else F32),
            scratch_shapes=[pltpu.VMEM((HG_GRP, HG_HEAD_DIM, HG_HEAD_DIM), F32)],
            compiler_params=cp,
        )(*args)
    return o_prev


CV_CW = 512
CV_PAD = 8


def _dn_conv_kernel(xm_ref, xp_ref, xn_ref, w_ref, o_ref, xs_ref, *, nsc):
    s = pl.program_id(1)
    cb = pl.program_id(2)
    base = CV_PAD + GRID_W
    ext = DN_SC + 2 * GRID_W
    is_ctx = s == 0
    has_up = s > 1
    has_dn = jnp.logical_and(s > 0, s < nsc - 1)
    xs_ref[1, 0:CV_PAD, :] = jnp.zeros((CV_PAD, CV_CW), F32)
    xs_ref[1, CV_PAD + ext:, :] = jnp.zeros((CV_PAD, CV_CW), F32)
    xs_ref[1, CV_PAD:base, :] = jnp.where(has_up, xp_ref[...], 0.0)
    xs_ref[1, base:base + DN_SC, :] = xm_ref[...]
    xs_ref[1, base + DN_SC:base + DN_SC + GRID_W, :] = jnp.where(has_dn, xn_ref[...], 0.0)
    t = lax.broadcasted_iota(jnp.int32, (ext, 1), 0) - GRID_W
    pos = jnp.where(is_ctx, t, t % GRID_W)
    ok_lf = pos > 0
    ok_rt = pos < jnp.where(is_ctx, DN_SC - 1, GRID_W - 1)
    xs_ref[0, CV_PAD:CV_PAD + ext, :] = jnp.where(ok_lf, xs_ref[1, CV_PAD - 1:CV_PAD - 1 + ext, :], 0.0)
    xs_ref[2, CV_PAD:CV_PAD + ext, :] = jnp.where(ok_rt, xs_ref[1, CV_PAD + 1:CV_PAD + 1 + ext, :], 0.0)
    rows_on = jnp.where(is_ctx, 0.0, 1.0)
    acc = jnp.zeros((DN_SC, CV_CW), F32)
    for dr in (-1, 0, 1):
        for dc in (-1, 0, 1):
            k = (dr + 1) * 3 + (dc + 1)
            w = w_ref[k:k + 1, :]
            if dr != 0:
                w = w * rows_on
            start = base + dr * GRID_W
            acc = acc + xs_ref[dc + 1, start:start + DN_SC, :] * w
    y = jax.nn.silu(acc)
    nq = DN_QK // CV_CW
    scale = jnp.where(cb < nq, DN_HEAD_DIM ** -0.5, 1.0)
    is_qk = cb < 2 * nq
    for h in range(CV_CW // LANES):
        yh = y[:, h * LANES:(h + 1) * LANES]
        rs = lax.rsqrt(jnp.sum(yh * yh, axis=-1, keepdims=True) + EPS) * scale
        o_ref[:, h * LANES:(h + 1) * LANES] = (yh * jnp.where(is_qk, rs, 1.0)).astype(o_ref.dtype)


def _dn_gate_kernel(x_ref, na_ref, dtb_ref, o_ref):
    x = x_ref[...]
    lane = lax.broadcasted_iota(jnp.int32, (1, LANES), 1)
    o_ref[...] = jnp.where(lane < 2 * DN_V_HEADS, jax.nn.sigmoid(x),
                           na_ref[...] * jax.nn.softplus(x + dtb_ref[...]))


def _dn_inputs(p, pg, L, conv_w, a_log, dt_bias):
    B, T, _ = p.shape
    nsc = T // DN_SC
    gpb = DN_SC // GRID_W
    nrow = T // GRID_W
    assert L == DN_SC and DN_QK % CV_CW == 0 and DN_QKV % CV_CW == 0 and DN_QKV % LANES == 0
    w9 = conv_w.astype(F32).reshape(9, DN_QKV)
    cp = functools.partial(pltpu.CompilerParams, vmem_limit_bytes=VMEM_LIMIT)
    qkv = pl.pallas_call(
        functools.partial(_dn_conv_kernel, nsc=nsc),
        grid=(B, nsc, DN_QKV // CV_CW),
        in_specs=[pl.BlockSpec((None, DN_SC, CV_CW), lambda b, s, c: (b, s, c)),
                  pl.BlockSpec((None, GRID_W, CV_CW), lambda b, s, c: (b, jnp.maximum(s * gpb - 1, 0), c)),
                  pl.BlockSpec((None, GRID_W, CV_CW), lambda b, s, c: (b, jnp.minimum(s * gpb + gpb, nrow - 1), c)),
                  pl.BlockSpec((9, CV_CW), lambda b, s, c: (0, c))],
        out_specs=pl.BlockSpec((None, DN_SC, CV_CW), lambda b, s, c: (b, s, c)),
        out_shape=jax.ShapeDtypeStruct((B, T, DN_QKV), BF16),
        scratch_shapes=[pltpu.VMEM((3, 2 * CV_PAD + 2 * GRID_W + DN_SC, CV_CW), F32)],
        compiler_params=cp(dimension_semantics=("arbitrary", "arbitrary", "arbitrary")),
    )(p, p, p, w9)

    zeros = jnp.zeros((2 * DN_V_HEADS,), F32)
    na = jnp.concatenate([zeros, -jnp.exp(a_log.astype(F32)).reshape(-1)]).reshape(1, LANES)
    dtb = jnp.concatenate([zeros, dt_bias.astype(F32).reshape(-1)]).reshape(1, LANES)
    gates = pl.pallas_call(
        _dn_gate_kernel,
        grid=(B, nsc),
        in_specs=[pl.BlockSpec((None, DN_SC, LANES), lambda b, s: (b, s, 0)),
                  pl.BlockSpec((1, LANES), lambda b, s: (0, 0)),
                  pl.BlockSpec((1, LANES), lambda b, s: (0, 0))],
        out_specs=pl.BlockSpec((None, DN_SC, LANES), lambda b, s: (b, s, 0)),
        out_shape=jax.ShapeDtypeStruct((B, T, LANES), F32),
        compiler_params=cp(dimension_semantics=("arbitrary", "arbitrary")),
    )(pg, na, dtb)
    return qkv, gates


PJ_TM = 768
PJ_TN = 512
PO_TM = 384
PO_TK = 1024


def _row_mod(mods_ref, slot, row0, nrows, n_ctx):
    is_ctx = (row0 + lax.broadcasted_iota(jnp.int32, (nrows, 1), 0)) < n_ctx
    return jnp.where(is_ctx, mods_ref[0, slot:slot + 1, :], mods_ref[1, slot:slot + 1, :])


def _norm_mod(x, g, mods_ref, shift_slot, row0, n_ctx):
    y = x * lax.rsqrt(jnp.mean(x * x, axis=-1, keepdims=True) + EPS) * g
    n = x.shape[0]
    return y * (1.0 + _row_mod(mods_ref, shift_slot + 1, row0, n, n_ctx)) + _row_mod(mods_ref, shift_slot, row0, n, n_ctx)


def _in_proj_kernel(x_ref, g_ref, mods_ref, w_ref, o_ref, h_ref, *, slot, n_ctx):
    @pl.when(pl.program_id(2) == 0)
    def _():
        row0 = pl.program_id(1) * x_ref.shape[0]
        h_ref[...] = _norm_mod(x_ref[...], g_ref[...], mods_ref, slot, row0, n_ctx).astype(BF16)

    o_ref[...] = jnp.dot(h_ref[...], w_ref[...], preferred_element_type=F32)


def _in_proj(xs, g, mods, w, slot, n_ctx, tn=PJ_TN):
    B, T, D = xs.shape
    N = w.shape[1]
    assert T % PJ_TM == 0 and N % tn == 0
    return pl.pallas_call(
        functools.partial(_in_proj_kernel, slot=slot, n_ctx=n_ctx),
        grid=(B, T // PJ_TM, N // tn),
        in_specs=[pl.BlockSpec((None, PJ_TM, D), lambda b, i, n: (b, i, 0)),
                  pl.BlockSpec((1, D), lambda b, i, n: (0, 0)),
                  pl.BlockSpec((None, 2, N_MOD, D), lambda b, i, n: (b, 0, 0, 0)),
                  pl.BlockSpec((D, tn), lambda b, i, n: (0, n))],
        out_specs=pl.BlockSpec((None, PJ_TM, tn), lambda b, i, n: (b, i, n)),
        out_shape=jax.ShapeDtypeStruct((B, T, N), F32),
        scratch_shapes=[pltpu.VMEM((PJ_TM, D), BF16)],
        compiler_params=pltpu.CompilerParams(dimension_semantics=("arbitrary", "arbitrary", "arbitrary"),
                                             vmem_limit_bytes=VMEM_LIMIT),
    )(xs, g.astype(F32).reshape(1, D), mods, w)


def _residual(x, o, g, mods_ref, slot, row0, n_ctx):
    on = o * lax.rsqrt(jnp.mean(o * o, axis=-1, keepdims=True) + EPS) * g
    return x + _row_mod(mods_ref, slot, row0, x.shape[0], n_ctx) * on


def _out_proj_even_kernel(a_ref, b_ref, w_ref, x_ref, g_ref, mods_ref, o_ref, *, slot, n_ctx):
    lhs = jnp.concatenate([a_ref[...], b_ref[...].astype(BF16)], axis=1)
    o = jnp.dot(lhs, w_ref[...], preferred_element_type=F32)
    row0 = pl.program_id(1) * x_ref.shape[0]
    o_ref[...] = _residual(x_ref[...], o, g_ref[...], mods_ref, slot, row0, n_ctx)


def _out_proj_even(o_hg, o_s5, w, xs, g, mods, slot, n_ctx):
    B, T, D = xs.shape
    assert T % PO_TM == 0
    row = lambda wd: pl.BlockSpec((None, PO_TM, wd), lambda b, i: (b, i, 0))
    return pl.pallas_call(
        functools.partial(_out_proj_even_kernel, slot=slot, n_ctx=n_ctx),
        grid=(B, T // PO_TM),
        in_specs=[row(HG_WIDTH), row(S5_WIDTH), pl.BlockSpec((D, D), lambda b, i: (0, 0)), row(D),
                  pl.BlockSpec((1, D), lambda b, i: (0, 0)),
                  pl.BlockSpec((None, 2, N_MOD, D), lambda b, i: (b, 0, 0, 0))],
        out_specs=row(D),
        out_shape=jax.ShapeDtypeStruct((B, T, D), F32),
        compiler_params=pltpu.CompilerParams(dimension_semantics=("arbitrary", "arbitrary"),
                                             vmem_limit_bytes=VMEM_LIMIT),
    )(o_hg, o_s5, w, xs, g.astype(F32).reshape(1, D), mods)


def _out_proj_odd_kernel(of_ref, ob_ref, z_ref, ng_ref, w_ref, x_ref, g_ref, mods_ref, o_ref, acc_ref, *, slot, n_ctx):
    k = pl.program_id(2)

    @pl.when(k == 0)
    def _():
        acc_ref[...] = jnp.zeros_like(acc_ref)

    parts = []
    for h in range(of_ref.shape[1] // DN_HEAD_DIM):
        cols = slice(h * DN_HEAD_DIM, (h + 1) * DN_HEAD_DIM)
        o = of_ref[:, cols] + ob_ref[:, cols]
        y = o * lax.rsqrt(jnp.mean(o * o, axis=-1, keepdims=True) + EPS) * ng_ref[...]
        parts.append((y * jax.nn.silu(z_ref[:, cols])).astype(BF16))
    acc_ref[...] += jnp.dot(jnp.concatenate(parts, axis=1), w_ref[...], preferred_element_type=F32)

    @pl.when(k == pl.num_programs(2) - 1)
    def _():
        row0 = pl.program_id(1) * x_ref.shape[0]
        o_ref[...] = _residual(x_ref[...], acc_ref[...], g_ref[...], mods_ref, slot, row0, n_ctx)


def _out_proj_odd(o_f, o_b, p, z_col0, norm_g, w, xs, g, mods, slot, n_ctx):
    B, T, D = xs.shape
    assert T % PO_TM == 0 and DN_V % PO_TK == 0 and z_col0 % PO_TK == 0
    zb = z_col0 // PO_TK
    return pl.pallas_call(
        functools.partial(_out_proj_odd_kernel, slot=slot, n_ctx=n_ctx),
        grid=(B, T // PO_TM, DN_V // PO_TK),
        in_specs=[pl.BlockSpec((None, PO_TM, PO_TK), lambda b, i, k: (b, i, k)),
                  pl.BlockSpec((None, PO_TM, PO_TK), lambda b, i, k: (b, i, k)),
                  pl.BlockSpec((None, PO_TM, PO_TK), lambda b, i, k: (b, i, zb + k)),
                  pl.BlockSpec((1, DN_HEAD_DIM), lambda b, i, k: (0, 0)),
                  pl.BlockSpec((PO_TK, D), lambda b, i, k: (k, 0)),
                  pl.BlockSpec((None, PO_TM, D), lambda b, i, k: (b, i, 0)),
                  pl.BlockSpec((1, D), lambda b, i, k: (0, 0)),
                  pl.BlockSpec((None, 2, N_MOD, D), lambda b, i, k: (b, 0, 0, 0))],
        out_specs=pl.BlockSpec((None, PO_TM, D), lambda b, i, k: (b, i, 0)),
        out_shape=jax.ShapeDtypeStruct((B, T, D), F32),
        scratch_shapes=[pltpu.VMEM((PO_TM, D), F32)],
        compiler_params=pltpu.CompilerParams(dimension_semantics=("arbitrary", "arbitrary", "arbitrary"),
                                             vmem_limit_bytes=VMEM_LIMIT),
    )(o_f, o_b, p, norm_g.astype(F32).reshape(1, DN_HEAD_DIM), w, xs, g.astype(F32).reshape(1, D), mods)


def _post_residual_kernel(x_ref, y_ref, g_ref, mods_ref, o_ref, *, slot, n_ctx):
    row0 = pl.program_id(1) * x_ref.shape[0]
    o_ref[...] = _residual(x_ref[...], y_ref[...], g_ref[...], mods_ref, slot, row0, n_ctx)


def _post_residual(xs, y, g, mods, slot, n_ctx):
    B, T, D = xs.shape
    row = pl.BlockSpec((None, PJ_TM, D), lambda b, i: (b, i, 0))
    return pl.pallas_call(
        functools.partial(_post_residual_kernel, slot=slot, n_ctx=n_ctx),
        grid=(B, T // PJ_TM),
        in_specs=[row, row, pl.BlockSpec((1, D), lambda b, i: (0, 0)),
                  pl.BlockSpec((None, 2, N_MOD, D), lambda b, i: (b, 0, 0, 0))],
        out_specs=row,
        out_shape=jax.ShapeDtypeStruct((B, T, D), F32),
        compiler_params=pltpu.CompilerParams(dimension_semantics=("arbitrary", "arbitrary"),
                                             vmem_limit_bytes=VMEM_LIMIT),
    )(xs, y, g.astype(F32).reshape(1, D), mods)


MOE_FT = 256


def _router_kernel(x_ref, g_ref, mods_ref, rw_ref, h_ref, aff_ref, *, slot, n_ctx):
    row0 = pl.program_id(1) * x_ref.shape[0]
    h = _norm_mod(x_ref[...], g_ref[...], mods_ref, slot, row0, n_ctx)
    h_ref[...] = h
    logits = lax.dot_general(rw_ref[...], h.astype(BF16), (((1,), (1,)), ((), ())), preferred_element_type=F32)
    aff_ref[...] = jax.nn.softmax(logits, axis=0)


def _router(xs, g, mods, router_w, slot, n_ctx):
    B, T, D = xs.shape
    E = router_w.shape[1]
    row = pl.BlockSpec((None, PJ_TM, D), lambda b, i: (b, i, 0))
    return pl.pallas_call(
        functools.partial(_router_kernel, slot=slot, n_ctx=n_ctx),
        grid=(B, T // PJ_TM),
        in_specs=[row, pl.BlockSpec((1, D), lambda b, i: (0, 0)),
                  pl.BlockSpec((None, 2, N_MOD, D), lambda b, i: (b, 0, 0, 0)),
                  pl.BlockSpec((E, D), lambda b, i: (0, 0))],
        out_specs=[row, pl.BlockSpec((None, E, PJ_TM), lambda b, i: (b, 0, i))],
        out_shape=[jax.ShapeDtypeStruct((B, T, D), F32), jax.ShapeDtypeStruct((B, E, T), F32)],
        compiler_params=pltpu.CompilerParams(dimension_semantics=("arbitrary", "arbitrary"),
                                             vmem_limit_bytes=VMEM_LIMIT),
    )(xs, g.astype(F32).reshape(1, D), mods, router_w.T.astype(BF16))


def _expert_ffn_kernel(idx_ref, gate_ref, h_ref, wg_ref, wu_ref, wd_ref, o_ref, xs_ref, sem, *, row0, cap):
    b = pl.program_id(1)
    f = pl.program_id(2)

    def row_copy(s, tok):
        return pltpu.make_async_copy(h_ref.at[b, pl.ds(row0 + tok, 1), :], xs_ref.at[pl.ds(s, 1), :], sem)

    @pl.when(f == 0)
    def _():
        def issue(s, carry):
            row_copy(s, idx_ref[0, s]).start()
            return carry

        def drain(s, carry):
            row_copy(s, 0).wait()
            return carry

        lax.fori_loop(0, cap, issue, 0)
        lax.fori_loop(0, cap, drain, 0)

    x = xs_ref[...].astype(BF16)
    gt = jnp.dot(x, wg_ref[...].astype(BF16), preferred_element_type=F32)
    up = jnp.dot(x, wu_ref[...].astype(BF16), preferred_element_type=F32)
    hid = (jax.nn.silu(gt) * up).astype(BF16)
    part = jnp.dot(hid, wd_ref[...].astype(BF16), preferred_element_type=F32)

    @pl.when(f == 0)
    def _():
        o_ref[...] = part

    @pl.when(f > 0)
    def _():
        o_ref[...] += part

    @pl.when(f == pl.num_programs(2) - 1)
    def _():
        o_ref[...] = o_ref[...] * gate_ref[...]


def _expert_ffn(h, idx, gate, w_gate, w_up, w_down, row0):
    B, T, D = h.shape
    E, _, FF = w_gate.shape
    cap = idx.shape[-1]
    ft = min(MOE_FT, FF)
    assert FF % ft == 0
    return pl.pallas_call(
        functools.partial(_expert_ffn_kernel, row0=row0, cap=cap),
        grid=(E, B, FF // ft),
        in_specs=[pl.BlockSpec((None, 1, cap), lambda e, b, f: (b * E + e, 0, 0), memory_space=pltpu.SMEM),
                  pl.BlockSpec((None, None, cap, 1), lambda e, b, f: (b, e, 0, 0)),
                  pl.BlockSpec(memory_space=pl.ANY),
                  pl.BlockSpec((None, D, ft), lambda e, b, f: (e, 0, f)),
                  pl.BlockSpec((None, D, ft), lambda e, b, f: (e, 0, f)),
                  pl.BlockSpec((None, ft, D), lambda e, b, f: (e, f, 0))],
        out_specs=pl.BlockSpec((None, None, cap, D), lambda e, b, f: (b, e, 0, 0)),
        out_shape=jax.ShapeDtypeStruct((B, E, cap, D), F32),
        scratch_shapes=[pltpu.VMEM((cap, D), F32), pltpu.SemaphoreType.DMA],
        compiler_params=pltpu.CompilerParams(dimension_semantics=("arbitrary", "arbitrary", "arbitrary"),
                                             vmem_limit_bytes=VMEM_LIMIT),
    )(idx.reshape(B * E, 1, cap), gate.reshape(B, E, cap, 1), h, w_gate, w_up, w_down)


def _ec_moe(h, aff, row0, n, w_gate, w_up, w_down):
    B, T, D = h.shape
    cap = EC_CAPACITY * n // N_EXPERTS
    gate, idx = lax.top_k(aff[:, :, row0:row0 + n], cap)
    y = _expert_ffn(h, idx.astype(jnp.int32), gate, w_gate, w_up, w_down, row0)
    return jax.vmap(lambda ib, yb: jnp.zeros((n, D), yb.dtype).at[ib.reshape(-1)].add(yb.reshape(-1, D)))(idx, y)


def kernel(x, c, ctx, c_ctx, mod_w, mod_b, norm_g, ev_w_in, ev_w_out, hg_lb, hg_norm_g,
           s5_a_re, s5_a_im, s5_log_dt, s5_b_re, s5_b_im, s5_c_re, s5_c_im, s5_d, s5_glu_w, s5_glu_b,
           od_w_in, od_conv_w, od_a_log, od_dt_bias, od_norm_g, od_w_out,
           router_w, moe_w_gate, moe_w_up, moe_w_down):
    B, L, D = ctx.shape
    lb_all = jnp.cumsum(jax.nn.softmax(hg_lb.astype(F32), axis=0), axis=0)
    sc = jax.nn.silu(c)
    scc = jax.nn.silu(c_ctx)
    xs = jnp.concatenate([ctx, x], axis=1)
    for layer in range(DEPTH):
        last = layer == DEPTH - 1
        j = layer // 2
        m_lat = (sc @ mod_w[layer] + mod_b[layer]).reshape(B, N_MOD, D)
        m_ctx = (scc @ mod_w[layer] + mod_b[layer]).reshape(N_MOD, D)
        mods = jnp.stack([jnp.broadcast_to(m_ctx, (B, N_MOD, D)), m_lat], axis=1)
        g_pre_mix, g_post_mix, g_pre_ffn, g_post_ffn = norm_g[layer]
        if layer % 2 == 0:
            p = _in_proj(xs, g_pre_mix, mods, ev_w_in[j].astype(BF16), 0, L)
            o_hg = _hgrn2_mixer(p, L, lb_all[j], hg_norm_g[j])
            s5w = _s5_weights(s5_a_re[j], s5_a_im[j], s5_log_dt[j], s5_b_re[j], s5_b_im[j],
                              s5_c_re[j], s5_c_im[j], s5_d[j], s5_glu_w[j], s5_glu_b[j])
            o_s5 = _s5_mixer(p, 5 * HG_WIDTH, L, s5w)
            xs = _out_proj_even(o_hg, o_s5, ev_w_out[j].astype(BF16), xs, g_post_mix, mods, 2, L)
        else:
            w_in = od_w_in[j].astype(BF16)
            p = _in_proj(xs, g_pre_mix, mods, w_in[:, :DN_QKV + DN_V], 0, L)
            pg = _in_proj(xs, g_pre_mix, mods, w_in[:, DN_QKV + DN_V:], 0, L, tn=LANES)
            qkv, gates = _dn_inputs(p, pg, L, od_conv_w[j], od_a_log[j], od_dt_bias[j])
            o_f, o_b = _delta_rule(qkv, gates, L)
            xs = _out_proj_odd(o_f, o_b, p, DN_QKV, od_norm_g[j], od_w_out[j].astype(BF16), xs, g_post_mix,
                               mods, 2, L)
        h, aff = _router(xs, g_pre_ffn, mods, router_w[layer], 3, L)
        moe = functools.partial(_ec_moe, h, aff, w_gate=moe_w_gate[layer], w_up=moe_w_up[layer],
                                w_down=moe_w_down[layer])
        y_ctx = jnp.zeros((B, L, D), F32) if last else moe(0, L)
        y = jnp.concatenate([y_ctx, moe(L, xs.shape[1] - L)], axis=1)
        xs = _post_residual(xs, y, g_post_ffn, mods, 5, L)
    return xs[:, L:]
```

```python
import functools
import math

import jax
import jax.numpy as jnp
from jax import lax
from jax.experimental import pallas as pl
from jax.experimental.pallas import tpu as pltpu

D_MODEL = 2048
DEPTH = 2
GRID_W = 64
N_MOD = 6
EPS = 1e-6

HG_WIDTH = D_MODEL // 2
HG_HEAD_DIM = 128
HG_HEADS = HG_WIDTH // HG_HEAD_DIM
HG_CHUNK = 64
S5_WIDTH = D_MODEL - HG_WIDTH
S5_GROUP = 16
S5_GROUPS = S5_WIDTH // S5_GROUP
S5_STATE = 64

DN_QK_HEADS = 16
DN_V_HEADS = 32
DN_HEAD_DIM = 128
DN_QK = DN_QK_HEADS * DN_HEAD_DIM
DN_V = DN_V_HEADS * DN_HEAD_DIM
DN_QKV = 2 * DN_QK + DN_V
DN_CHUNK = 64

N_EXPERTS = 16
EC_CAPACITY = 2

LANES = 128
VMEM_LIMIT = 56 * 1024 * 1024

S5_LC = 16
S5_TILES = S5_WIDTH // LANES
S5_GPT = LANES // S5_GROUP
S5_ST = S5_GPT * S5_STATE
S5_CAT = S5_LC * LANES
S5_NSPLIT = 2

F32 = jnp.float32
BF16 = jnp.bfloat16


def _s5_weights(a_re, a_im, log_dt, b_re, b_im, c_re, c_im, d_skip, glu_w, glu_b):
    G, P, C, LC, NT, GT = S5_GROUPS, S5_STATE, S5_GROUP, S5_LC, S5_TILES, S5_GPT
    ar, ai = a_re.astype(F32), a_im.astype(F32)
    dt = jnp.exp(log_dt.astype(F32))[..., None]
    steps = jnp.arange(LC, dtype=F32)
    cmul = lambda x, y: (x[0] * y[0] - x[1] * y[1], x[0] * y[1] + x[1] * y[0])

    def apow(e):
        ex = e[:, None, None, None]
        mag = jnp.exp(ex * (ar * dt)[None])
        ang = ex * (ai * dt)[None]
        return mag * jnp.cos(ang), mag * jnp.sin(ang)

    one = apow(jnp.ones((1,), F32))
    nr, ni = one[0][0] - 1.0, one[1][0]
    den = ar * ar + ai * ai
    zoh = ((nr * ar + ni * ai) / den, (ni * ar - nr * ai) / den)
    bb = cmul((zoh[0][..., None], zoh[1][..., None]), (b_re.astype(F32), b_im.astype(F32)))
    cm = (c_re.astype(F32), c_im.astype(F32))
    hi = lax.Precision.HIGHEST
    at = lambda z, idx: (z[0][idx], z[1][idx])
    lane_grp = jnp.arange(LANES) // C

    def spread(a, width):
        keep = lane_grp[:, None] == (jnp.arange(GT * width) // width)[None, :]
        return jnp.where(keep, jnp.concatenate([a] * GT, axis=-1), 0.0)

    cp = cmul((cm[0][None], cm[1][None]), at(apow(steps), (slice(None), slice(None), slice(None), None)))
    kk = (jnp.einsum('tdgcp,dgpk->dtgck', cp[0], bb[0], precision=hi)
          - jnp.einsum('tdgcp,dgpk->dtgck', cp[1], bb[1], precision=hi))
    bd = spread(jnp.swapaxes(kk, -1, -2).reshape(2, LC, NT, LANES, C), C).astype(BF16)
    s_in = jnp.arange(LC)[:, None, None]
    s_out = jnp.arange(LC)[None, :, None]
    lag = jnp.arange(LC)[None, None, :]
    sel = jnp.concatenate([s_out - s_in == lag, s_in - s_out == lag], axis=-1).astype(BF16)
    toe = jnp.einsum('abt,tjrc->jarbc', sel, bd.reshape(2 * LC, NT, LANES, LANES), preferred_element_type=F32)
    toe = toe.astype(BF16).reshape(NT, S5_CAT, S5_CAT)

    def inject(e):
        er = jnp.stack(e, 0).reshape(2, LC, NT, GT, P, C)
        er = jnp.moveaxis(er, -1, -2).reshape(2, LC, NT, LANES, P)
        m = spread(er, P).astype(BF16)
        m = jnp.concatenate([m[0], m[1]], axis=-1)
        return jnp.swapaxes(m, 0, 1).reshape(NT, S5_CAT, 2 * S5_ST)

    def pw_dir(e, d):
        z = apow(e)
        return z[0][:, d][..., None], z[1][:, d][..., None]

    inj_f = inject(cmul(pw_dir(LC - 1 - steps, 0), (bb[0][0][None], bb[1][0][None])))
    inj_b = inject(cmul(pw_dir(steps, 1), (bb[0][1][None], bb[1][1][None])))

    def readout(w):
        wr = jnp.stack([w[0], -w[1]], 0).reshape(2, LC, NT, LANES, P)
        n = spread(wr, P).astype(BF16)
        n = jnp.concatenate([n[0], n[1]], axis=-1)
        return jnp.transpose(n, (1, 3, 0, 2)).reshape(NT, 2 * S5_ST, S5_CAT)

    def pw_row(e, d):
        z = apow(e)
        return z[0][:, d][:, :, None, :], z[1][:, d][:, :, None, :]

    rd_f = readout(cmul((cm[0][0][None], cm[1][0][None]), pw_row(steps + 1.0, 0)))
    rd_b = readout(cmul((cm[0][1][None], cm[1][1][None]), pw_row(LC - steps, 1)))

    w1 = jnp.concatenate([toe, inj_f, inj_b], axis=-1)
    w2 = jnp.concatenate([rd_f, rd_b], axis=1)
    a_lc = apow(jnp.full((1,), LC, F32))
    a16 = jnp.stack([a_lc[0][0, 0], a_lc[1][0, 0], a_lc[0][0, 1], a_lc[1][0, 1]], 0)
    a16 = a16.reshape(4, NT, S5_ST).transpose(1, 0, 2)
    gw = spread(glu_w.astype(F32).reshape(NT, LANES, C), C).astype(BF16)
    dk = d_skip.astype(F32).reshape(NT, 1, LANES)
    gb = glu_b.astype(F32).reshape(NT, 1, LANES)
    return w1, w2, a16, dk, gw, gb


def _s5_inject_kernel(u_ref, w_ref, r_ref, ucat_ref, *, nb, nch):
    @pl.when(pl.program_id(1) == 0)
    def _():
        for b in range(nb):
            for s in range(S5_LC):
                ucat_ref[b * nch:(b + 1) * nch, s * LANES:(s + 1) * LANES] = (
                    u_ref[b, pl.ds(s, nch, stride=S5_LC), :].astype(BF16))

    r = jnp.dot(ucat_ref[...], w_ref[...], preferred_element_type=F32)
    for b in range(nb):
        r_ref[b] = r[b * nch:(b + 1) * nch]


def _s5_scan_kernel(s_ref, a_ref, x_ref, *, nch, nctx):
    st = S5_ST
    arf, aif, arb, aib = a_ref[0:1, :], a_ref[1:2, :], a_ref[2:3, :], a_ref[3:4, :]

    def body(i, carry):
        fr, fi, br, bi = carry
        kf = i
        kb = jnp.where(i < nctx, nctx - 1 - i, nch + nctx - 1 - i)
        x_ref[pl.ds(kf, 1), 0:st] = fr
        x_ref[pl.ds(kf, 1), st:2 * st] = fi
        x_ref[pl.ds(kb, 1), 2 * st:3 * st] = br
        x_ref[pl.ds(kb, 1), 3 * st:4 * st] = bi
        sfr = s_ref[pl.ds(kf, 1), 0:st]
        sfi = s_ref[pl.ds(kf, 1), st:2 * st]
        sbr = s_ref[pl.ds(kb, 1), 2 * st:3 * st]
        sbi = s_ref[pl.ds(kb, 1), 3 * st:4 * st]
        return (arf * fr - aif * fi + sfr, arf * fi + aif * fr + sfi,
                arb * br - aib * bi + sbr, arb * bi + aib * br + sbi)

    z = jnp.zeros((1, st), F32)
    lax.fori_loop(0, nch, body, (z, z, z, z))


def _s5_readout_kernel(yi_ref, x_ref, w_ref, u_ref, d_ref, gw_ref, gb_ref, o_ref, *, nch):
    half = pl.program_id(2)
    y = yi_ref[...] + jnp.dot(x_ref[...].astype(BF16), w_ref[...], preferred_element_type=F32)
    per = S5_LC // S5_NSPLIT
    for sl in range(per):
        s = half * per + sl
        ys = y[:, sl * LANES:(sl + 1) * LANES] + d_ref[...] * u_ref[pl.ds(s, nch, stride=S5_LC), :]
        ys = jax.nn.gelu(ys)
        z = jnp.dot(ys.astype(BF16), gw_ref[...], preferred_element_type=F32) + gb_ref[...]
        o_ref[pl.ds(s, nch, stride=S5_LC), :] = ys * jax.nn.sigmoid(z)


def _s5_mixer(p, u_col0, L, weights):
    w1, w2, a16, dk, gw, gb = weights
    B, T, _ = p.shape
    nch = T // S5_LC
    nctx = L // S5_LC
    assert T % S5_LC == 0 and L % S5_LC == 0 and u_col0 % LANES == 0
    ucol = u_col0 // LANES
    n1 = w1.shape[-1]
    nblk1 = 1024
    cparams = functools.partial(pltpu.CompilerParams, vmem_limit_bytes=VMEM_LIMIT)

    r = pl.pallas_call(
        functools.partial(_s5_inject_kernel, nb=B, nch=nch),
        grid=(S5_TILES, n1 // nblk1),
        in_specs=[pl.BlockSpec((B, T, LANES), lambda j, n: (0, 0, ucol + j)),
                  pl.BlockSpec((None, S5_CAT, nblk1), lambda j, n: (j, 0, n))],
        out_specs=pl.BlockSpec((B, None, nch, nblk1), lambda j, n: (0, j, 0, n)),
        out_shape=jax.ShapeDtypeStruct((B, S5_TILES, nch, n1), F32),
        scratch_shapes=[pltpu.VMEM((B * nch, S5_CAT), BF16)],
        compiler_params=cparams(dimension_semantics=("arbitrary", "arbitrary")),
    )(p, w1)

    sblk = S5_CAT // (4 * S5_ST)
    assert S5_CAT % (4 * S5_ST) == 0
    xin = pl.pallas_call(
        functools.partial(_s5_scan_kernel, nch=nch, nctx=nctx),
        grid=(B, S5_TILES),
        in_specs=[pl.BlockSpec((None, None, nch, 4 * S5_ST), lambda b, j: (b, j, 0, sblk)),
                  pl.BlockSpec((None, 4, S5_ST), lambda b, j: (j, 0, 0))],
        out_specs=pl.BlockSpec((None, None, nch, 4 * S5_ST), lambda b, j: (b, j, 0, 0)),
        out_shape=jax.ShapeDtypeStruct((B, S5_TILES, nch, 4 * S5_ST), F32),
        compiler_params=cparams(dimension_semantics=("arbitrary", "arbitrary")),
    )(r, a16)

    ncol = S5_CAT // S5_NSPLIT
    return pl.pallas_call(
        functools.partial(_s5_readout_kernel, nch=nch),
        grid=(S5_TILES, B, S5_NSPLIT),
        in_specs=[pl.BlockSpec((None, None, nch, ncol), lambda j, b, h: (b, j, 0, h)),
                  pl.BlockSpec((None, None, nch, 4 * S5_ST), lambda j, b, h: (b, j, 0, 0)),
                  pl.BlockSpec((None, 4 * S5_ST, ncol), lambda j, b, h: (j, 0, h)),
                  pl.BlockSpec((None, T, LANES), lambda j, b, h: (b, 0, ucol + j)),
                  pl.BlockSpec((None, 1, LANES), lambda j, b, h: (j, 0, 0)),
                  pl.BlockSpec((None, LANES, LANES), lambda j, b, h: (j, 0, 0)),
                  pl.BlockSpec((None, 1, LANES), lambda j, b, h: (j, 0, 0))],
        out_specs=pl.BlockSpec((None, T, LANES), lambda j, b, h: (b, 0, j)),
        out_shape=jax.ShapeDtypeStruct((B, T, S5_WIDTH), F32),
        compiler_params=cparams(dimension_semantics=("arbitrary", "arbitrary", "arbitrary")),
    )(r, xin, w2, p, dk, gw, gb)


DN_SC = 256
DN_NC = DN_SC // DN_CHUNK
DN_REP = DN_V_HEADS // DN_QK_HEADS
DN_LEVELS = DN_CHUNK.bit_length() - 1
DN_SCAN_HG = 4


def _dn_masks():
    out = []
    for reverse in (False, True):
        i = jnp.arange(DN_SC)[:, None]
        j = jnp.arange(DN_SC)[None, :]
        same = (i // DN_CHUNK) == (j // DN_CHUNK)
        if reverse:
            i, j = j, i
        ms = [same & (i >= j), same & (i > j)]
        for lv in range(DN_LEVELS):
            ms.append(same & ((i >> (lv + 1)) == (j >> (lv + 1))) & (((i >> lv) & 1) == 1) & (((j >> lv) & 1) == 0))
        out.append(jnp.stack(ms))
    return jnp.stack(out).astype(F32)


def _lane_pick(x, col):
    hot = (lax.broadcasted_iota(jnp.int32, (1, LANES), 1) == col).astype(F32)
    return jnp.sum(x * hot, axis=1, keepdims=True)


def _dn_prep_kernel(q_ref, k_ref, v_ref, g_ref, m_ref, uwf_ref, uwb_ref, qkf_ref, qkb_ref, gc_ref,
                    gcs_ref, gct_ref):
    C, D = DN_CHUNK, DN_HEAD_DIM
    hq = pl.program_id(2)
    gb = g_ref[...]

    @pl.when(hq == 0)
    def _():
        lane = lax.broadcasted_iota(jnp.int32, (1, LANES), 1)
        out = gb
        for d in range(2):
            cs = jnp.dot(m_ref[d, 0], gb, preferred_element_type=F32, precision=lax.Precision.HIGHEST)
            gcs_ref[d] = cs
            gct_ref[d] = cs.T
            lo = 2 * DN_V_HEADS + d * DN_V_HEADS
            out = jnp.where((lane >= lo) & (lane < lo + DN_V_HEADS), cs, out)
        gc_ref[...] = out

    q = q_ref[...]
    k = k_ref[...]
    kk = lax.dot_general(k, k, (((1,), (1,)), ((), ())), preferred_element_type=F32)
    qk = lax.dot_general(q, k, (((1,), (1,)), ((), ())), preferred_element_type=F32)
    kf = k.astype(F32)
    uw_refs = (uwf_ref, uwb_ref)
    qk_refs = (qkf_ref, qkb_ref)
    insts = [(d, r) for d in range(2) for r in range(DN_REP)]
    bcol, gcol, dec, a_mat, m = {}, {}, {}, {}, {}
    for i in insts:
        d, r = i
        hv = hq * DN_REP + r
        incl = m_ref[d, 0]
        strict = m_ref[d, 1]
        bcol[i] = _lane_pick(gb, d * DN_V_HEADS + hv)
        gcol[i] = _lane_pick(gcs_ref[d], 2 * DN_V_HEADS + d * DN_V_HEADS + hv)
        grow = gct_ref[d, pl.ds(2 * DN_V_HEADS + d * DN_V_HEADS + hv, 1), :]
        dec[i] = incl * jnp.exp(jnp.where(incl > 0, gcol[i] - grow, 0.0))
        a_mat[i] = strict * (bcol[i] * (kk * dec[i]))
        m[i] = (incl - strict) - a_mat[i] * m_ref[d, 2]
    for lv in range(1, DN_LEVELS):
        mb = {i: m[i].astype(BF16) for i in insts}
        x = {i: jnp.dot(mb[i], (a_mat[i] * m_ref[i[0], 2 + lv]).astype(BF16), preferred_element_type=F32)
             for i in insts}
        for i in insts:
            m[i] = m[i] - jnp.dot(x[i].astype(BF16), mb[i], preferred_element_type=F32)
    for i in insts:
        d, r = i
        egc = jnp.exp(gcol[i])
        v = v_ref[:, r * D:(r + 1) * D].astype(F32)
        rhs = jnp.concatenate([(v * bcol[i]).astype(BF16), (kf * (bcol[i] * egc)).astype(BF16)], axis=1)
        uw = jnp.dot(m[i].astype(BF16), rhs, preferred_element_type=F32)
        uw_refs[d][:, r * 2 * D:(r + 1) * 2 * D] = uw.astype(BF16)
    for d in range(2):
        parts = []
        for r in range(DN_REP):
            qd = qk * dec[(d, r)]
            parts.append(jnp.concatenate([qd[c * C:(c + 1) * C, c * C:(c + 1) * C] for c in range(DN_NC)],
                                         axis=0))
        qk_refs[d][...] = jnp.concatenate(parts, axis=1).astype(BF16)


def _dn_scan_kernel(uw_ref, qkd_ref, q_ref, k_ref, gc_ref, o_ref, s_ref, *, reverse):
    C, D = DN_CHUNK, DN_HEAD_DIM
    dirn = 1 if reverse else 0
    hgrp = pl.program_id(1)
    last = 0 if reverse else C - 1

    @pl.when(pl.program_id(2) == 0)
    def _():
        s_ref[...] = jnp.zeros_like(s_ref)

    heads = range(DN_SCAN_HG * DN_REP)
    for ci in range(DN_NC):
        c = (DN_NC - 1 - ci) if reverse else ci
        r0 = c * C
        gcb = gc_ref[r0:r0 + C, :]
        gcol, g_end, s_old, ws = {}, {}, {}, {}
        for hl in heads:
            hq = hl // DN_REP
            hv = hgrp * (DN_SCAN_HG * DN_REP) + hl
            gcol[hl] = _lane_pick(gcb, 2 * DN_V_HEADS + dirn * DN_V_HEADS + hv)
            g_end[hl] = gcol[hl][last:last + 1, :]
            qf = q_ref[r0:r0 + C, hq * D:(hq + 1) * D].astype(F32)
            w = uw_ref[r0:r0 + C, hl * 2 * D + D:(hl + 1) * 2 * D]
            s_old[hl] = s_ref[hl]
            wq = jnp.concatenate([w, (qf * jnp.exp(gcol[hl])).astype(BF16)], axis=0)
            ws[hl] = jnp.dot(wq, s_old[hl].astype(BF16), preferred_element_type=F32)
        for hl in heads:
            hq = hl // DN_REP
            u = uw_ref[r0:r0 + C, hl * 2 * D:hl * 2 * D + D].astype(F32)
            vnb = (u - ws[hl][:C]).astype(BF16)
            qkd = qkd_ref[r0:r0 + C, hl * C:(hl + 1) * C]
            o_ref[r0:r0 + C, hl * D:(hl + 1) * D] = ws[hl][C:] + jnp.dot(qkd, vnb, preferred_element_type=F32)
            kf = k_ref[r0:r0 + C, hq * D:(hq + 1) * D].astype(F32)
            k_dec_t = (kf * jnp.exp(g_end[hl] - gcol[hl])).T.astype(BF16)
            s_ref[hl] = s_old[hl] * jnp.exp(g_end[hl]) + jnp.dot(k_dec_t, vnb, preferred_element_type=F32)


def _delta_rule(qkv, gates, L):
    B, T, _ = qkv.shape
    nsc = T // DN_SC
    assert T % DN_SC == 0 and L == DN_SC and DN_QK_HEADS % DN_SCAN_HG == 0
    D, C = DN_HEAD_DIM, DN_CHUNK
    masks = _dn_masks()
    cp = functools.partial(pltpu.CompilerParams, vmem_limit_bytes=VMEM_LIMIT)
    blk = lambda w, off=0: pl.BlockSpec((None, DN_SC, w), lambda b, s, h: (b, s, off // w + h))
    uwf, uwb, qkf, qkb, gc = pl.pallas_call(
        _dn_prep_kernel,
        grid=(B, nsc, DN_QK_HEADS),
        in_specs=[blk(D), blk(D, DN_QK), blk(DN_REP * D, 2 * DN_QK),
                  pl.BlockSpec((None, DN_SC, LANES), lambda b, s, h: (b, s, 0)),
                  pl.BlockSpec(masks.shape, lambda b, s, h: (0, 0, 0, 0))],
        out_specs=[blk(DN_REP * 2 * D), blk(DN_REP * 2 * D), blk(DN_REP * C), blk(DN_REP * C),
                   pl.BlockSpec((None, DN_SC, LANES), lambda b, s, h: (b, s, 0))],
        out_shape=[jax.ShapeDtypeStruct((B, T, DN_V_HEADS * 2 * D), BF16)] * 2
        + [jax.ShapeDtypeStruct((B, T, DN_V_HEADS * C), BF16)] * 2
        + [jax.ShapeDtypeStruct((B, T, LANES), F32)],
        scratch_shapes=[pltpu.VMEM((2, DN_SC, LANES), F32), pltpu.VMEM((2, LANES, DN_SC), F32)],
        compiler_params=cp(dimension_semantics=("arbitrary", "arbitrary", "arbitrary")),
    )(qkv, qkv, qkv, gates, masks)

    outs = []
    G = DN_SCAN_HG
    for reverse, uw, qkd in ((False, uwf, qkf), (True, uwb, qkb)):
        if reverse:
            sc_of = lambda i: jnp.where(i == 0, 0, nsc - i)
        else:
            sc_of = lambda i: i
        sblk = lambda w, off=0, f=sc_of: pl.BlockSpec((None, DN_SC, w), lambda b, h, i: (b, f(i), off // w + h))
        outs.append(pl.pallas_call(
            functools.partial(_dn_scan_kernel, reverse=reverse),
            grid=(B, DN_QK_HEADS // G, nsc),
            in_specs=[sblk(G * DN_REP * 2 * D), sblk(G * DN_REP * C), sblk(G * D), sblk(G * D, DN_QK),
                      pl.BlockSpec((None, DN_SC, LANES), lambda b, h, i, f=sc_of: (b, f(i), 0))],
            out_specs=sblk(G * DN_REP * D),
            out_shape=jax.ShapeDtypeStruct((B, T, DN_V), F32),
            scratch_shapes=[pltpu.VMEM((G * DN_REP, D, D), F32)],
            compiler_params=cp(dimension_semantics=("arbitrary", "arbitrary", "arbitrary")),
        )(uw, qkd, qkv, qkv, gc))
    return outs


HG_SC = DN_SC
HG_NC = HG_SC // HG_CHUNK
HG_GRP = 4
assert HG_CHUNK == DN_CHUNK


def _hg_scan_kernel(*refs, reverse, final):
    if final:
        q_ref, f_ref, i_ref, lb_ref, tri_ref, of_ref, g_ref, ng_ref, o_ref, st_ref = refs
    else:
        q_ref, f_ref, i_ref, lb_ref, tri_ref, o_ref, st_ref = refs
    C, D = HG_CHUNK, HG_HEAD_DIM
    mid = (C - 1 - (C // 2 - 1)) if reverse else (C // 2 - 1)
    last = 0 if reverse else C - 1

    @pl.when(pl.program_id(2) == 0)
    def _():
        st_ref[...] = jnp.zeros_like(st_ref)

    tri = tri_ref[...]
    tri_c = tri[0:C, 0:C]
    heads = range(HG_GRP)
    qs, ks, bs = {}, {}, {}
    for h in heads:
        cols = slice(h * D, (h + 1) * D)
        lb = lb_ref[:, cols]
        f = lb + (1.0 - lb) * jax.nn.sigmoid(f_ref[:, cols])
        ks[h] = 1.0 - f
        qs[h] = q_ref[:, cols]
        bs[h] = jnp.dot(tri, jnp.log(f), preferred_element_type=F32, precision=lax.Precision.HIGHEST)
    for ci in range(HG_NC):
        c = (HG_NC - 1 - ci) if reverse else ci
        r0 = c * C
        rows = slice(r0, r0 + C)
        att, qd, kd, vb, dl = {}, {}, {}, {}, {}
        for h in heads:
            b = bs[h][rows]
            ref = b[mid:mid + 1, :]
            b_last = b[last:last + 1, :]
            q = qs[h][rows]
            k = ks[h][rows]
            qa = (q * jnp.exp(b - ref)).astype(BF16)
            ka = (k * jnp.exp(ref - b)).astype(BF16)
            att[h] = lax.dot_general(qa, ka, (((1,), (1,)), ((), ())), preferred_element_type=F32) * tri_c
            qd[h] = (q * jnp.exp(b)).astype(BF16)
            kd[h] = (k * jnp.exp(b_last - b)).astype(BF16)
            dl[h] = jnp.exp(b_last)
            vb[h] = i_ref[rows, h * D:(h + 1) * D]
        for h in heads:
            cols = slice(h * D, (h + 1) * D)
            st = st_ref[h]
            v16 = vb[h].astype(BF16)
            o = (jnp.dot(att[h].astype(BF16), v16, preferred_element_type=F32)
                 + lax.dot_general(qd[h], st.astype(BF16), (((1,), (1,)), ((), ())), preferred_element_type=F32))
            st_ref[h] = st * dl[h] + jnp.dot(vb[h].T.astype(BF16), kd[h], preferred_element_type=F32)
            if final:
                o = o + of_ref[rows, cols]
                y = o * lax.rsqrt(jnp.mean(o * o, axis=-1, keepdims=True) + EPS) * ng_ref[...]
                o_ref[rows, cols] = (y * jax.nn.silu(g_ref[rows, cols])).astype(o_ref.dtype)
            else:
                o_ref[rows, cols] = o


def _hgrn2_mixer(p, L, lb, norm_g):
    B, T, _ = p.shape
    nsc = T // HG_SC
    assert T % HG_SC == 0 and L == HG_SC and HG_HEADS % HG_GRP == 0
    gw = HG_GRP * HG_HEAD_DIM
    per = HG_WIDTH // gw
    tri = _dn_masks()[:, 0]
    ng = norm_g.astype(F32).reshape(1, HG_HEAD_DIM)
    lbf = lb.astype(F32)
    cp = pltpu.CompilerParams(dimension_semantics=("arbitrary", "arbitrary", "arbitrary"),
                              vmem_limit_bytes=VMEM_LIMIT)
    o_prev = None
    for reverse in (False, True):
        d = 1 if reverse else 0
        if reverse:
            sc_of = lambda i: jnp.where(i == 0, 0, nsc - i)
        else:
            sc_of = lambda i: i
        col = lambda sec, f=sc_of: pl.BlockSpec((None, HG_SC, gw), lambda b, h, i: (b, f(i), sec * per + h))
        in_specs = [col(0), col(1 + d), col(3),
                    pl.BlockSpec((None, 1, gw), lambda b, h, i: (d, 0, h)),
                    pl.BlockSpec((None, HG_SC, HG_SC), lambda b, h, i: (d, 0, 0))]
        args = [p, p, p, lbf.reshape(2, 1, HG_WIDTH), tri]
        final = reverse
        if final:
            in_specs += [pl.BlockSpec((None, HG_SC, gw), lambda b, h, i, f=sc_of: (b, f(i), h)), col(4),
                         pl.BlockSpec((1, HG_HEAD_DIM), lambda b, h, i: (0, 0))]
            args += [o_prev, p, ng]
        o_prev = pl.pallas_call(
            functools.partial(_hg_scan_kernel, reverse=reverse, final=final),
            grid=(B, HG_HEADS // HG_GRP, nsc),
            in_specs=in_specs,
            out_specs=pl.BlockSpec((None, HG_SC, gw), lambda b, h, i, f=sc_of: (b, f(i), h)),
            out_shape=jax.ShapeDtypeStruct((B, T, HG_WIDTH), BF16 if final else F32),
            scratch_shapes=[pltpu.VMEM((HG_GRP, HG_HEAD_DIM, HG_HEAD_DIM), F32)],
            compiler_params=cp,
        )(*args)
    return o_prev


CV_CW = 512
CV_PAD = 8


def _dn_conv_kernel(xm_ref, xp_ref, xn_ref, w_ref, o_ref, xs_ref, *, nsc):
    s = pl.program_id(1)
    cb = pl.program_id(2)
    base = CV_PAD + GRID_W
    ext = DN_SC + 2 * GRID_W
    is_ctx = s == 0
    has_up = s > 1
    has_dn = jnp.logical_and(s > 0, s < nsc - 1)
    xs_ref[1, 0:CV_PAD, :] = jnp.zeros((CV_PAD, CV_CW), F32)
    xs_ref[1, CV_PAD + ext:, :] = jnp.zeros((CV_PAD, CV_CW), F32)
    xs_ref[1, CV_PAD:base, :] = jnp.where(has_up, xp_ref[...], 0.0)
    xs_ref[1, base:base + DN_SC, :] = xm_ref[...]
    xs_ref[1, base + DN_SC:base + DN_SC + GRID_W, :] = jnp.where(has_dn, xn_ref[...], 0.0)
    t = lax.broadcasted_iota(jnp.int32, (ext, 1), 0) - GRID_W
    pos = jnp.where(is_ctx, t, t % GRID_W)
    ok_lf = pos > 0
    ok_rt = pos < jnp.where(is_ctx, DN_SC - 1, GRID_W - 1)
    xs_ref[0, CV_PAD:CV_PAD + ext, :] = jnp.where(ok_lf, xs_ref[1, CV_PAD - 1:CV_PAD - 1 + ext, :], 0.0)
    xs_ref[2, CV_PAD:CV_PAD + ext, :] = jnp.where(ok_rt, xs_ref[1, CV_PAD + 1:CV_PAD + 1 + ext, :], 0.0)
    rows_on = jnp.where(is_ctx, 0.0, 1.0)
    acc = jnp.zeros((DN_SC, CV_CW), F32)
    for dr in (-1, 0, 1):
        for dc in (-1, 0, 1):
            k = (dr + 1) * 3 + (dc + 1)
            w = w_ref[k:k + 1, :]
            if dr != 0:
                w = w * rows_on
            start = base + dr * GRID_W
            acc = acc + xs_ref[dc + 1, start:start + DN_SC, :] * w
    y = jax.nn.silu(acc)
    nq = DN_QK // CV_CW
    scale = jnp.where(cb < nq, DN_HEAD_DIM ** -0.5, 1.0)
    is_qk = cb < 2 * nq
    for h in range(CV_CW // LANES):
        yh = y[:, h * LANES:(h + 1) * LANES]
        rs = lax.rsqrt(jnp.sum(yh * yh, axis=-1, keepdims=True) + EPS) * scale
        o_ref[:, h * LANES:(h + 1) * LANES] = (yh * jnp.where(is_qk, rs, 1.0)).astype(o_ref.dtype)


def _dn_gate_kernel(x_ref, na_ref, dtb_ref, o_ref):
    x = x_ref[...]
    lane = lax.broadcasted_iota(jnp.int32, (1, LANES), 1)
    o_ref[...] = jnp.where(lane < 2 * DN_V_HEADS, jax.nn.sigmoid(x),
                           na_ref[...] * jax.nn.softplus(x + dtb_ref[...]))


def _dn_inputs(p, pg, L, conv_w, a_log, dt_bias):
    B, T, _ = p.shape
    nsc = T // DN_SC
    gpb = DN_SC // GRID_W
    nrow = T // GRID_W
    assert L == DN_SC and DN_QK % CV_CW == 0 and DN_QKV % CV_CW == 0 and DN_QKV % LANES == 0
    w9 = conv_w.astype(F32).reshape(9, DN_QKV)
    cp = functools.partial(pltpu.CompilerParams, vmem_limit_bytes=VMEM_LIMIT)
    qkv = pl.pallas_call(
        functools.partial(_dn_conv_kernel, nsc=nsc),
        grid=(B, nsc, DN_QKV // CV_CW),
        in_specs=[pl.BlockSpec((None, DN_SC, CV_CW), lambda b, s, c: (b, s, c)),
                  pl.BlockSpec((None, GRID_W, CV_CW), lambda b, s, c: (b, jnp.maximum(s * gpb - 1, 0), c)),
                  pl.BlockSpec((None, GRID_W, CV_CW), lambda b, s, c: (b, jnp.minimum(s * gpb + gpb, nrow - 1), c)),
                  pl.BlockSpec((9, CV_CW), lambda b, s, c: (0, c))],
        out_specs=pl.BlockSpec((None, DN_SC, CV_CW), lambda b, s, c: (b, s, c)),
        out_shape=jax.ShapeDtypeStruct((B, T, DN_QKV), BF16),
        scratch_shapes=[pltpu.VMEM((3, 2 * CV_PAD + 2 * GRID_W + DN_SC, CV_CW), F32)],
        compiler_params=cp(dimension_semantics=("arbitrary", "arbitrary", "arbitrary")),
    )(p, p, p, w9)

    zeros = jnp.zeros((2 * DN_V_HEADS,), F32)
    na = jnp.concatenate([zeros, -jnp.exp(a_log.astype(F32)).reshape(-1)]).reshape(1, LANES)
    dtb = jnp.concatenate([zeros, dt_bias.astype(F32).reshape(-1)]).reshape(1, LANES)
    gates = pl.pallas_call(
        _dn_gate_kernel,
        grid=(B, nsc),
        in_specs=[pl.BlockSpec((None, DN_SC, LANES), lambda b, s: (b, s, 0)),
                  pl.BlockSpec((1, LANES), lambda b, s: (0, 0)),
                  pl.BlockSpec((1, LANES), lambda b, s: (0, 0))],
        out_specs=pl.BlockSpec((None, DN_SC, LANES), lambda b, s: (b, s, 0)),
        out_shape=jax.ShapeDtypeStruct((B, T, LANES), F32),
        compiler_params=cp(dimension_semantics=("arbitrary", "arbitrary")),
    )(pg, na, dtb)
    return qkv, gates


PJ_TM = 768
PJ_TN = 1024
PO_TM = 384
PO_TK = 1024


def _row_mod(mods_ref, slot, row0, nrows, n_ctx):
    is_ctx = (row0 + lax.broadcasted_iota(jnp.int32, (nrows, 1), 0)) < n_ctx
    return jnp.where(is_ctx, mods_ref[0, slot:slot + 1, :], mods_ref[1, slot:slot + 1, :])


def _norm_mod(x, g, mods_ref, shift_slot, row0, n_ctx):
    y = x * lax.rsqrt(jnp.mean(x * x, axis=-1, keepdims=True) + EPS) * g
    n = x.shape[0]
    return y * (1.0 + _row_mod(mods_ref, shift_slot + 1, row0, n, n_ctx)) + _row_mod(mods_ref, shift_slot, row0, n, n_ctx)


def _in_proj_kernel(x_ref, g_ref, mods_ref, w_ref, o_ref, h_ref, *, slot, n_ctx):
    @pl.when(pl.program_id(2) == 0)
    def _():
        row0 = pl.program_id(1) * x_ref.shape[0]
        h_ref[...] = _norm_mod(x_ref[...], g_ref[...], mods_ref, slot, row0, n_ctx).astype(BF16)

    o_ref[...] = jnp.dot(h_ref[...], w_ref[...], preferred_element_type=F32)


def _in_proj(xs, g, mods, w, slot, n_ctx, col0=0, ncols=None, tn=PJ_TN):
    B, T, D = xs.shape
    N = w.shape[1] - col0 if ncols is None else ncols
    assert T % PJ_TM == 0 and N % tn == 0 and col0 % tn == 0
    cb0 = col0 // tn
    return pl.pallas_call(
        functools.partial(_in_proj_kernel, slot=slot, n_ctx=n_ctx),
        grid=(B, T // PJ_TM, N // tn),
        in_specs=[pl.BlockSpec((None, PJ_TM, D), lambda b, i, n: (b, i, 0)),
                  pl.BlockSpec((1, D), lambda b, i, n: (0, 0)),
                  pl.BlockSpec((None, 2, N_MOD, D), lambda b, i, n: (b, 0, 0, 0)),
                  pl.BlockSpec((D, tn), lambda b, i, n: (0, cb0 + n))],
        out_specs=pl.BlockSpec((None, PJ_TM, tn), lambda b, i, n: (b, i, n)),
        out_shape=jax.ShapeDtypeStruct((B, T, N), F32),
        scratch_shapes=[pltpu.VMEM((PJ_TM, D), BF16)],
        compiler_params=pltpu.CompilerParams(dimension_semantics=("arbitrary", "arbitrary", "arbitrary"),
                                             vmem_limit_bytes=VMEM_LIMIT),
    )(xs, g.astype(F32).reshape(1, D), mods, w)


def _residual(x, o, g, mods_ref, slot, row0, n_ctx):
    on = o * lax.rsqrt(jnp.mean(o * o, axis=-1, keepdims=True) + EPS) * g
    return x + _row_mod(mods_ref, slot, row0, x.shape[0], n_ctx) * on


def _out_proj_even_kernel(a_ref, b_ref, w_ref, x_ref, g_ref, mods_ref, o_ref, *, slot, n_ctx):
    lhs = jnp.concatenate([a_ref[...], b_ref[...].astype(BF16)], axis=1)
    o = jnp.dot(lhs, w_ref[...], preferred_element_type=F32)
    row0 = pl.program_id(1) * x_ref.shape[0]
    o_ref[...] = _residual(x_ref[...], o, g_ref[...], mods_ref, slot, row0, n_ctx)


def _out_proj_even(o_hg, o_s5, w, xs, g, mods, slot, n_ctx):
    B, T, D = xs.shape
    assert T % PO_TM == 0
    row = lambda wd: pl.BlockSpec((None, PO_TM, wd), lambda b, i: (b, i, 0))
    return pl.pallas_call(
        functools.partial(_out_proj_even_kernel, slot=slot, n_ctx=n_ctx),
        grid=(B, T // PO_TM),
        in_specs=[row(HG_WIDTH), row(S5_WIDTH), pl.BlockSpec((D, D), lambda b, i: (0, 0)), row(D),
                  pl.BlockSpec((1, D), lambda b, i: (0, 0)),
                  pl.BlockSpec((None, 2, N_MOD, D), lambda b, i: (b, 0, 0, 0))],
        out_specs=row(D),
        out_shape=jax.ShapeDtypeStruct((B, T, D), F32),
        compiler_params=pltpu.CompilerParams(dimension_semantics=("arbitrary", "arbitrary"),
                                             vmem_limit_bytes=VMEM_LIMIT),
    )(o_hg, o_s5, w, xs, g.astype(F32).reshape(1, D), mods)


def _out_proj_odd_kernel(of_ref, ob_ref, z_ref, ng_ref, w_ref, x_ref, g_ref, mods_ref, o_ref, acc_ref, *, slot, n_ctx):
    k = pl.program_id(2)

    @pl.when(k == 0)
    def _():
        acc_ref[...] = jnp.zeros_like(acc_ref)

    parts = []
    for h in range(of_ref.shape[1] // DN_HEAD_DIM):
        cols = slice(h * DN_HEAD_DIM, (h + 1) * DN_HEAD_DIM)
        o = of_ref[:, cols] + ob_ref[:, cols]
        y = o * lax.rsqrt(jnp.mean(o * o, axis=-1, keepdims=True) + EPS) * ng_ref[...]
        parts.append((y * jax.nn.silu(z_ref[:, cols])).astype(BF16))
    acc_ref[...] += jnp.dot(jnp.concatenate(parts, axis=1), w_ref[...], preferred_element_type=F32)

    @pl.when(k == pl.num_programs(2) - 1)
    def _():
        row0 = pl.program_id(1) * x_ref.shape[0]
        o_ref[...] = _residual(x_ref[...], acc_ref[...], g_ref[...], mods_ref, slot, row0, n_ctx)


def _out_proj_odd(o_f, o_b, p, z_col0, norm_g, w, xs, g, mods, slot, n_ctx):
    B, T, D = xs.shape
    assert T % PO_TM == 0 and DN_V % PO_TK == 0 and z_col0 % PO_TK == 0
    zb = z_col0 // PO_TK
    return pl.pallas_call(
        functools.partial(_out_proj_odd_kernel, slot=slot, n_ctx=n_ctx),
        grid=(B, T // PO_TM, DN_V // PO_TK),
        in_specs=[pl.BlockSpec((None, PO_TM, PO_TK), lambda b, i, k: (b, i, k)),
                  pl.BlockSpec((None, PO_TM, PO_TK), lambda b, i, k: (b, i, k)),
                  pl.BlockSpec((None, PO_TM, PO_TK), lambda b, i, k: (b, i, zb + k)),
                  pl.BlockSpec((1, DN_HEAD_DIM), lambda b, i, k: (0, 0)),
                  pl.BlockSpec((PO_TK, D), lambda b, i, k: (k, 0)),
                  pl.BlockSpec((None, PO_TM, D), lambda b, i, k: (b, i, 0)),
                  pl.BlockSpec((1, D), lambda b, i, k: (0, 0)),
                  pl.BlockSpec((None, 2, N_MOD, D), lambda b, i, k: (b, 0, 0, 0))],
        out_specs=pl.BlockSpec((None, PO_TM, D), lambda b, i, k: (b, i, 0)),
        out_shape=jax.ShapeDtypeStruct((B, T, D), F32),
        scratch_shapes=[pltpu.VMEM((PO_TM, D), F32)],
        compiler_params=pltpu.CompilerParams(dimension_semantics=("arbitrary", "arbitrary", "arbitrary"),
                                             vmem_limit_bytes=VMEM_LIMIT),
    )(o_f, o_b, p, norm_g.astype(F32).reshape(1, DN_HEAD_DIM), w, xs, g.astype(F32).reshape(1, D), mods)


def _post_residual_kernel(x_ref, y_ref, g_ref, mods_ref, o_ref, *, slot, n_ctx):
    row0 = pl.program_id(1) * x_ref.shape[0]
    o_ref[...] = _residual(x_ref[...], y_ref[...], g_ref[...], mods_ref, slot, row0, n_ctx)


def _post_residual(xs, y, g, mods, slot, n_ctx):
    B, T, D = xs.shape
    row = pl.BlockSpec((None, PJ_TM, D), lambda b, i: (b, i, 0))
    return pl.pallas_call(
        functools.partial(_post_residual_kernel, slot=slot, n_ctx=n_ctx),
        grid=(B, T // PJ_TM),
        in_specs=[row, row, pl.BlockSpec((1, D), lambda b, i: (0, 0)),
                  pl.BlockSpec((None, 2, N_MOD, D), lambda b, i: (b, 0, 0, 0))],
        out_specs=row,
        out_shape=jax.ShapeDtypeStruct((B, T, D), F32),
        compiler_params=pltpu.CompilerParams(dimension_semantics=("arbitrary", "arbitrary"),
                                             vmem_limit_bytes=VMEM_LIMIT),
    )(xs, y, g.astype(F32).reshape(1, D), mods)


MOE_FT = 256


def _router_kernel(x_ref, g_ref, mods_ref, rw_ref, h_ref, aff_ref, *, slot, n_ctx):
    row0 = pl.program_id(1) * x_ref.shape[0]
    h = _norm_mod(x_ref[...], g_ref[...], mods_ref, slot, row0, n_ctx)
    h_ref[...] = h
    logits = lax.dot_general(rw_ref[...], h.astype(BF16), (((1,), (1,)), ((), ())), preferred_element_type=F32)
    aff_ref[...] = jax.nn.softmax(logits, axis=0)


def _router(xs, g, mods, router_w, slot, n_ctx):
    B, T, D = xs.shape
    E = router_w.shape[1]
    row = pl.BlockSpec((None, PJ_TM, D), lambda b, i: (b, i, 0))
    return pl.pallas_call(
        functools.partial(_router_kernel, slot=slot, n_ctx=n_ctx),
        grid=(B, T // PJ_TM),
        in_specs=[row, pl.BlockSpec((1, D), lambda b, i: (0, 0)),
                  pl.BlockSpec((None, 2, N_MOD, D), lambda b, i: (b, 0, 0, 0)),
                  pl.BlockSpec((E, D), lambda b, i: (0, 0))],
        out_specs=[row, pl.BlockSpec((None, E, PJ_TM), lambda b, i: (b, 0, i))],
        out_shape=[jax.ShapeDtypeStruct((B, T, D), F32), jax.ShapeDtypeStruct((B, E, T), F32)],
        compiler_params=pltpu.CompilerParams(dimension_semantics=("arbitrary", "arbitrary"),
                                             vmem_limit_bytes=VMEM_LIMIT),
    )(xs, g.astype(F32).reshape(1, D), mods, router_w.T.astype(BF16))


def _expert_ffn_kernel(idx_ref, gate_ref, h_ref, wg_ref, wu_ref, wd_ref, o_ref, stage_ref, xs_ref, sem, *, cap):
    f = pl.program_id(2)

    @pl.when(f == 0)
    def _():
        def issue(s, carry):
            pltpu.make_async_copy(h_ref.at[pl.ds(idx_ref[0, s], 1), :], stage_ref.at[pl.ds(s, 1), :], sem).start()
            return carry

        lax.fori_loop(0, cap, issue, 0, unroll=8)
        pltpu.make_async_copy(h_ref.at[pl.ds(0, cap), :], stage_ref, sem).wait()
        xs_ref[...] = stage_ref[...].astype(BF16)

    x = xs_ref[...]
    gt = jnp.dot(x, wg_ref[...].astype(BF16), preferred_element_type=F32)
    up = jnp.dot(x, wu_ref[...].astype(BF16), preferred_element_type=F32)
    hid = (jax.nn.silu(gt) * up).astype(BF16)
    part = jnp.dot(hid, wd_ref[...].astype(BF16), preferred_element_type=F32)

    @pl.when(f == 0)
    def _():
        o_ref[...] = part

    @pl.when(f > 0)
    def _():
        o_ref[...] += part

    @pl.when(f == pl.num_programs(2) - 1)
    def _():
        o_ref[...] = o_ref[...] * gate_ref[...]


def _expert_ffn(h, rows, gate, w_gate, w_up, w_down):
    R, D = h.shape
    E, _, FF = w_gate.shape
    B, _, cap = rows.shape
    ft = min(MOE_FT, FF)
    assert FF % ft == 0 and cap <= R
    return pl.pallas_call(
        functools.partial(_expert_ffn_kernel, cap=cap),
        grid=(E, B, FF // ft),
        in_specs=[pl.BlockSpec((None, 1, cap), lambda e, b, f: (b * E + e, 0, 0), memory_space=pltpu.SMEM),
                  pl.BlockSpec((None, None, cap, 1), lambda e, b, f: (b, e, 0, 0)),
                  pl.BlockSpec(memory_space=pl.ANY),
                  pl.BlockSpec((None, D, ft), lambda e, b, f: (e, 0, f)),
                  pl.BlockSpec((None, D, ft), lambda e, b, f: (e, 0, f)),
                  pl.BlockSpec((None, ft, D), lambda e, b, f: (e, f, 0))],
        out_specs=pl.BlockSpec((None, None, cap, D), lambda e, b, f: (b, e, 0, 0)),
        out_shape=jax.ShapeDtypeStruct((B, E, cap, D), F32),
        scratch_shapes=[pltpu.VMEM((cap, D), F32), pltpu.VMEM((cap, D), BF16), pltpu.SemaphoreType.DMA],
        compiler_params=pltpu.CompilerParams(dimension_semantics=("arbitrary", "arbitrary", "arbitrary"),
                                             vmem_limit_bytes=VMEM_LIMIT),
    )(rows.reshape(B * E, 1, cap), gate.reshape(B, E, cap, 1), h, w_gate, w_up, w_down)


def _ec_moe(h, aff, row0, n, w_gate, w_up, w_down):
    B, T, D = h.shape
    cap = EC_CAPACITY * n // N_EXPERTS
    gate, idx = lax.top_k(aff[:, :, row0:row0 + n], cap)
    rows = idx.astype(jnp.int32) + (jnp.arange(B, dtype=jnp.int32) * T + row0)[:, None, None]
    y = _expert_ffn(h.reshape(B * T, D), rows, gate, w_gate, w_up, w_down)
    return jax.vmap(lambda ib, yb: jnp.zeros((n, D), yb.dtype).at[ib.reshape(-1)].add(yb.reshape(-1, D)))(idx, y)


def kernel(x, c, ctx, c_ctx, mod_w, mod_b, norm_g, ev_w_in, ev_w_out, hg_lb, hg_norm_g,
           s5_a_re, s5_a_im, s5_log_dt, s5_b_re, s5_b_im, s5_c_re, s5_c_im, s5_d, s5_glu_w, s5_glu_b,
           od_w_in, od_conv_w, od_a_log, od_dt_bias, od_norm_g, od_w_out,
           router_w, moe_w_gate, moe_w_up, moe_w_down):
    B, L, D = ctx.shape
    lb_all = jnp.cumsum(jax.nn.softmax(hg_lb.astype(F32), axis=0), axis=0)
    sc = jax.nn.silu(c)
    scc = jax.nn.silu(c_ctx)
    xs = jnp.concatenate([ctx, x], axis=1)
    for layer in range(DEPTH):
        last = layer == DEPTH - 1
        j = layer // 2
        m_lat = (sc @ mod_w[layer] + mod_b[layer]).reshape(B, N_MOD, D)
        m_ctx = (scc @ mod_w[layer] + mod_b[layer]).reshape(N_MOD, D)
        mods = jnp.stack([jnp.broadcast_to(m_ctx, (B, N_MOD, D)), m_lat], axis=1)
        g_pre_mix, g_post_mix, g_pre_ffn, g_post_ffn = norm_g[layer]
        if layer % 2 == 0:
            p = _in_proj(xs, g_pre_mix, mods, ev_w_in[j].astype(BF16), 0, L)
            o_hg = _hgrn2_mixer(p, L, lb_all[j], hg_norm_g[j])
            s5w = _s5_weights(s5_a_re[j], s5_a_im[j], s5_log_dt[j], s5_b_re[j], s5_b_im[j],
                              s5_c_re[j], s5_c_im[j], s5_d[j], s5_glu_w[j], s5_glu_b[j])
            o_s5 = _s5_mixer(p, 5 * HG_WIDTH, L, s5w)
            xs = _out_proj_even(o_hg, o_s5, ev_w_out[j].astype(BF16), xs, g_post_mix, mods, 2, L)
        else:
            w_in = od_w_in[j].astype(BF16)
            p = _in_proj(xs, g_pre_mix, mods, w_in, 0, L, ncols=DN_QKV + DN_V)
            pg = _in_proj(xs, g_pre_mix, mods, w_in, 0, L, col0=DN_QKV + DN_V, tn=LANES)
            qkv, gates = _dn_inputs(p, pg, L, od_conv_w[j], od_a_log[j], od_dt_bias[j])
            o_f, o_b = _delta_rule(qkv, gates, L)
            xs = _out_proj_odd(o_f, o_b, p, DN_QKV, od_norm_g[j], od_w_out[j].astype(BF16), xs, g_post_mix,
                               mods, 2, L)
        h, aff = _router(xs, g_pre_ffn, mods, router_w[layer], 3, L)
        moe = functools.partial(_ec_moe, h, aff, w_gate=moe_w_gate[layer], w_up=moe_w_up[layer],
                                w_down=moe_w_down[layer])
        y_ctx = jnp.zeros((B, L, D), F32) if last else moe(0, L)
        y = jnp.concatenate([y_ctx, moe(L, xs.shape[1] - L)], axis=1)
        xs = _post_residual(xs, y, g_post_ffn, mods, 5, L)
    return xs[:, L:]
```

```python
import functools
import math

import jax
import jax.numpy as jnp
from jax import lax
from jax.experimental import pallas as pl
from jax.experimental.pallas import tpu as pltpu

D_MODEL = 2048
DEPTH = 2
GRID_W = 64
N_MOD = 6
EPS = 1e-6

HG_WIDTH = D_MODEL // 2
HG_HEAD_DIM = 128
HG_HEADS = HG_WIDTH // HG_HEAD_DIM
HG_CHUNK = 64
S5_WIDTH = D_MODEL - HG_WIDTH
S5_GROUP = 16
S5_GROUPS = S5_WIDTH // S5_GROUP
S5_STATE = 64

DN_QK_HEADS = 16
DN_V_HEADS = 32
DN_HEAD_DIM = 128
DN_QK = DN_QK_HEADS * DN_HEAD_DIM
DN_V = DN_V_HEADS * DN_HEAD_DIM
DN_QKV = 2 * DN_QK + DN_V
DN_CHUNK = 64

N_EXPERTS = 16
EC_CAPACITY = 2

LANES = 128
VMEM_LIMIT = 56 * 1024 * 1024

S5_LC = 16
S5_TILES = S5_WIDTH // LANES
S5_GPT = LANES // S5_GROUP
S5_ST = S5_GPT * S5_STATE
S5_CAT = S5_LC * LANES
S5_NSPLIT = 2

F32 = jnp.float32
BF16 = jnp.bfloat16


def _s5_weights(a_re, a_im, log_dt, b_re, b_im, c_re, c_im, d_skip, glu_w, glu_b):
    G, P, C, LC, NT, GT = S5_GROUPS, S5_STATE, S5_GROUP, S5_LC, S5_TILES, S5_GPT
    ar, ai = a_re.astype(F32), a_im.astype(F32)
    dt = jnp.exp(log_dt.astype(F32))[..., None]
    steps = jnp.arange(LC, dtype=F32)
    cmul = lambda x, y: (x[0] * y[0] - x[1] * y[1], x[0] * y[1] + x[1] * y[0])

    def apow(e):
        ex = e[:, None, None, None]
        mag = jnp.exp(ex * (ar * dt)[None])
        ang = ex * (ai * dt)[None]
        return mag * jnp.cos(ang), mag * jnp.sin(ang)

    one = apow(jnp.ones((1,), F32))
    nr, ni = one[0][0] - 1.0, one[1][0]
    den = ar * ar + ai * ai
    zoh = ((nr * ar + ni * ai) / den, (ni * ar - nr * ai) / den)
    bb = cmul((zoh[0][..., None], zoh[1][..., None]), (b_re.astype(F32), b_im.astype(F32)))
    cm = (c_re.astype(F32), c_im.astype(F32))
    hi = lax.Precision.HIGHEST
    at = lambda z, idx: (z[0][idx], z[1][idx])
    lane_grp = jnp.arange(LANES) // C

    def spread(a, width):
        keep = lane_grp[:, None] == (jnp.arange(GT * width) // width)[None, :]
        return jnp.where(keep, jnp.concatenate([a] * GT, axis=-1), 0.0)

    cp = cmul((cm[0][None], cm[1][None]), at(apow(steps), (slice(None), slice(None), slice(None), None)))
    kk = (jnp.einsum('tdgcp,dgpk->dtgck', cp[0], bb[0], precision=hi)
          - jnp.einsum('tdgcp,dgpk->dtgck', cp[1], bb[1], precision=hi))
    bd = spread(jnp.swapaxes(kk, -1, -2).reshape(2, LC, NT, LANES, C), C).astype(BF16)
    s_in = jnp.arange(LC)[:, None, None]
    s_out = jnp.arange(LC)[None, :, None]
    lag = jnp.arange(LC)[None, None, :]
    sel = jnp.concatenate([s_out - s_in == lag, s_in - s_out == lag], axis=-1).astype(BF16)
    toe = jnp.einsum('abt,tjrc->jarbc', sel, bd.reshape(2 * LC, NT, LANES, LANES), preferred_element_type=F32)
    toe = toe.astype(BF16).reshape(NT, S5_CAT, S5_CAT)

    def inject(e):
        er = jnp.stack(e, 0).reshape(2, LC, NT, GT, P, C)
        er = jnp.moveaxis(er, -1, -2).reshape(2, LC, NT, LANES, P)
        m = spread(er, P).astype(BF16)
        m = jnp.concatenate([m[0], m[1]], axis=-1)
        return jnp.swapaxes(m, 0, 1).reshape(NT, S5_CAT, 2 * S5_ST)

    def pw_dir(e, d):
        z = apow(e)
        return z[0][:, d][..., None], z[1][:, d][..., None]

    inj_f = inject(cmul(pw_dir(LC - 1 - steps, 0), (bb[0][0][None], bb[1][0][None])))
    inj_b = inject(cmul(pw_dir(steps, 1), (bb[0][1][None], bb[1][1][None])))

    def readout(w):
        wr = jnp.stack([w[0], -w[1]], 0).reshape(2, LC, NT, LANES, P)
        n = spread(wr, P).astype(BF16)
        n = jnp.concatenate([n[0], n[1]], axis=-1)
        return jnp.transpose(n, (1, 3, 0, 2)).reshape(NT, 2 * S5_ST, S5_CAT)

    def pw_row(e, d):
        z = apow(e)
        return z[0][:, d][:, :, None, :], z[1][:, d][:, :, None, :]

    rd_f = readout(cmul((cm[0][0][None], cm[1][0][None]), pw_row(steps + 1.0, 0)))
    rd_b = readout(cmul((cm[0][1][None], cm[1][1][None]), pw_row(LC - steps, 1)))

    w1 = jnp.concatenate([toe, inj_f, inj_b], axis=-1)
    w2 = jnp.concatenate([rd_f, rd_b], axis=1)
    a_lc = apow(jnp.full((1,), LC, F32))
    a16 = jnp.stack([a_lc[0][0, 0], a_lc[1][0, 0], a_lc[0][0, 1], a_lc[1][0, 1]], 0)
    a16 = a16.reshape(4, NT, S5_ST).transpose(1, 0, 2)
    gw = spread(glu_w.astype(F32).reshape(NT, LANES, C), C).astype(BF16)
    dk = d_skip.astype(F32).reshape(NT, 1, LANES)
    gb = glu_b.astype(F32).reshape(NT, 1, LANES)
    return w1, w2, a16, dk, gw, gb


def _s5_inject_kernel(u_ref, w_ref, r_ref, ucat_ref, *, nb, nch):
    @pl.when(pl.program_id(1) == 0)
    def _():
        for b in range(nb):
            for s in range(S5_LC):
                ucat_ref[b * nch:(b + 1) * nch, s * LANES:(s + 1) * LANES] = (
                    u_ref[b, pl.ds(s, nch, stride=S5_LC), :].astype(BF16))

    r = jnp.dot(ucat_ref[...], w_ref[...], preferred_element_type=F32)
    for b in range(nb):
        r_ref[b] = r[b * nch:(b + 1) * nch]


def _s5_scan_kernel(s_ref, a_ref, x_ref, *, nch, nctx):
    st = S5_ST
    arf, aif, arb, aib = a_ref[0:1, :], a_ref[1:2, :], a_ref[2:3, :], a_ref[3:4, :]

    def body(i, carry):
        fr, fi, br, bi = carry
        kf = i
        kb = jnp.where(i < nctx, nctx - 1 - i, nch + nctx - 1 - i)
        x_ref[pl.ds(kf, 1), 0:st] = fr
        x_ref[pl.ds(kf, 1), st:2 * st] = fi
        x_ref[pl.ds(kb, 1), 2 * st:3 * st] = br
        x_ref[pl.ds(kb, 1), 3 * st:4 * st] = bi
        sfr = s_ref[pl.ds(kf, 1), 0:st]
        sfi = s_ref[pl.ds(kf, 1), st:2 * st]
        sbr = s_ref[pl.ds(kb, 1), 2 * st:3 * st]
        sbi = s_ref[pl.ds(kb, 1), 3 * st:4 * st]
        return (arf * fr - aif * fi + sfr, arf * fi + aif * fr + sfi,
                arb * br - aib * bi + sbr, arb * bi + aib * br + sbi)

    z = jnp.zeros((1, st), F32)
    lax.fori_loop(0, nch, body, (z, z, z, z))


def _s5_readout_kernel(yi_ref, x_ref, w_ref, u_ref, d_ref, gw_ref, gb_ref, o_ref, *, nch):
    half = pl.program_id(2)
    y = yi_ref[...] + jnp.dot(x_ref[...].astype(BF16), w_ref[...], preferred_element_type=F32)
    per = S5_LC // S5_NSPLIT
    for sl in range(per):
        s = half * per + sl
        ys = y[:, sl * LANES:(sl + 1) * LANES] + d_ref[...] * u_ref[pl.ds(s, nch, stride=S5_LC), :]
        ys = jax.nn.gelu(ys)
        z = jnp.dot(ys.astype(BF16), gw_ref[...], preferred_element_type=F32) + gb_ref[...]
        o_ref[pl.ds(s, nch, stride=S5_LC), :] = ys * jax.nn.sigmoid(z)


def _s5_mixer(p, u_col0, L, weights):
    w1, w2, a16, dk, gw, gb = weights
    B, T, _ = p.shape
    nch = T // S5_LC
    nctx = L // S5_LC
    assert T % S5_LC == 0 and L % S5_LC == 0 and u_col0 % LANES == 0
    ucol = u_col0 // LANES
    n1 = w1.shape[-1]
    nblk1 = 1024
    cparams = functools.partial(pltpu.CompilerParams, vmem_limit_bytes=VMEM_LIMIT)

    r = pl.pallas_call(
        functools.partial(_s5_inject_kernel, nb=B, nch=nch),
        grid=(S5_TILES, n1 // nblk1),
        in_specs=[pl.BlockSpec((B, T, LANES), lambda j, n: (0, 0, ucol + j)),
                  pl.BlockSpec((None, S5_CAT, nblk1), lambda j, n: (j, 0, n))],
        out_specs=pl.BlockSpec((B, None, nch, nblk1), lambda j, n: (0, j, 0, n)),
        out_shape=jax.ShapeDtypeStruct((B, S5_TILES, nch, n1), F32),
        scratch_shapes=[pltpu.VMEM((B * nch, S5_CAT), BF16)],
        compiler_params=cparams(dimension_semantics=("arbitrary", "arbitrary")),
    )(p, w1)

    sblk = S5_CAT // (4 * S5_ST)
    assert S5_CAT % (4 * S5_ST) == 0
    xin = pl.pallas_call(
        functools.partial(_s5_scan_kernel, nch=nch, nctx=nctx),
        grid=(B, S5_TILES),
        in_specs=[pl.BlockSpec((None, None, nch, 4 * S5_ST), lambda b, j: (b, j, 0, sblk)),
                  pl.BlockSpec((None, 4, S5_ST), lambda b, j: (j, 0, 0))],
        out_specs=pl.BlockSpec((None, None, nch, 4 * S5_ST), lambda b, j: (b, j, 0, 0)),
        out_shape=jax.ShapeDtypeStruct((B, S5_TILES, nch, 4 * S5_ST), F32),
        compiler_params=cparams(dimension_semantics=("arbitrary", "arbitrary")),
    )(r, a16)

    ncol = S5_CAT // S5_NSPLIT
    return pl.pallas_call(
        functools.partial(_s5_readout_kernel, nch=nch),
        grid=(S5_TILES, B, S5_NSPLIT),
        in_specs=[pl.BlockSpec((None, None, nch, ncol), lambda j, b, h: (b, j, 0, h)),
                  pl.BlockSpec((None, None, nch, 4 * S5_ST), lambda j, b, h: (b, j, 0, 0)),
                  pl.BlockSpec((None, 4 * S5_ST, ncol), lambda j, b, h: (j, 0, h)),
                  pl.BlockSpec((None, T, LANES), lambda j, b, h: (b, 0, ucol + j)),
                  pl.BlockSpec((None, 1, LANES), lambda j, b, h: (j, 0, 0)),
                  pl.BlockSpec((None, LANES, LANES), lambda j, b, h: (j, 0, 0)),
                  pl.BlockSpec((None, 1, LANES), lambda j, b, h: (j, 0, 0))],
        out_specs=pl.BlockSpec((None, T, LANES), lambda j, b, h: (b, 0, j)),
        out_shape=jax.ShapeDtypeStruct((B, T, S5_WIDTH), F32),
        compiler_params=cparams(dimension_semantics=("arbitrary", "arbitrary", "arbitrary")),
    )(r, xin, w2, p, dk, gw, gb)


DN_SC = 256
DN_NC = DN_SC // DN_CHUNK
DN_REP = DN_V_HEADS // DN_QK_HEADS
DN_LEVELS = DN_CHUNK.bit_length() - 1
DN_SCAN_HG = 4


def _dn_masks():
    out = []
    for reverse in (False, True):
        i = jnp.arange(DN_SC)[:, None]
        j = jnp.arange(DN_SC)[None, :]
        same = (i // DN_CHUNK) == (j // DN_CHUNK)
        if reverse:
            i, j = j, i
        ms = [same & (i >= j), same & (i > j)]
        for lv in range(DN_LEVELS):
            ms.append(same & ((i >> (lv + 1)) == (j >> (lv + 1))) & (((i >> lv) & 1) == 1) & (((j >> lv) & 1) == 0))
        out.append(jnp.stack(ms))
    return jnp.stack(out).astype(F32)


def _lane_pick(x, col):
    hot = (lax.broadcasted_iota(jnp.int32, (1, LANES), 1) == col).astype(F32)
    return jnp.sum(x * hot, axis=1, keepdims=True)


def _dn_prep_kernel(q_ref, k_ref, v_ref, g_ref, m_ref, uwf_ref, uwb_ref, qkf_ref, qkb_ref, gc_ref,
                    gcs_ref, gct_ref):
    C, D = DN_CHUNK, DN_HEAD_DIM
    hq = pl.program_id(2)
    gb = g_ref[...]

    @pl.when(hq == 0)
    def _():
        lane = lax.broadcasted_iota(jnp.int32, (1, LANES), 1)
        out = gb
        for d in range(2):
            cs = jnp.dot(m_ref[d, 0], gb, preferred_element_type=F32, precision=lax.Precision.HIGHEST)
            gcs_ref[d] = cs
            gct_ref[d] = cs.T
            lo = 2 * DN_V_HEADS + d * DN_V_HEADS
            out = jnp.where((lane >= lo) & (lane < lo + DN_V_HEADS), cs, out)
        gc_ref[...] = out

    q = q_ref[...]
    k = k_ref[...]
    kk = lax.dot_general(k, k, (((1,), (1,)), ((), ())), preferred_element_type=F32)
    qk = lax.dot_general(q, k, (((1,), (1,)), ((), ())), preferred_element_type=F32)
    kf = k.astype(F32)
    uw_refs = (uwf_ref, uwb_ref)
    qk_refs = (qkf_ref, qkb_ref)
    insts = [(d, r) for d in range(2) for r in range(DN_REP)]
    bcol, gcol, dec, a_mat, m = {}, {}, {}, {}, {}
    for i in insts:
        d, r = i
        hv = hq * DN_REP + r
        incl = m_ref[d, 0]
        strict = m_ref[d, 1]
        bcol[i] = _lane_pick(gb, d * DN_V_HEADS + hv)
        gcol[i] = _lane_pick(gcs_ref[d], 2 * DN_V_HEADS + d * DN_V_HEADS + hv)
        grow = gct_ref[d, pl.ds(2 * DN_V_HEADS + d * DN_V_HEADS + hv, 1), :]
        dec[i] = incl * jnp.exp(jnp.where(incl > 0, gcol[i] - grow, 0.0))
        a_mat[i] = strict * (bcol[i] * (kk * dec[i]))
        m[i] = (incl - strict) - a_mat[i] * m_ref[d, 2]
    for lv in range(1, DN_LEVELS):
        mb = {i: m[i].astype(BF16) for i in insts}
        x = {i: jnp.dot(mb[i], (a_mat[i] * m_ref[i[0], 2 + lv]).astype(BF16), preferred_element_type=F32)
             for i in insts}
        for i in insts:
            m[i] = m[i] - jnp.dot(x[i].astype(BF16), mb[i], preferred_element_type=F32)
    for i in insts:
        d, r = i
        egc = jnp.exp(gcol[i])
        v = v_ref[:, r * D:(r + 1) * D].astype(F32)
        rhs = jnp.concatenate([(v * bcol[i]).astype(BF16), (kf * (bcol[i] * egc)).astype(BF16)], axis=1)
        uw = jnp.dot(m[i].astype(BF16), rhs, preferred_element_type=F32)
        uw_refs[d][:, r * 2 * D:(r + 1) * 2 * D] = uw.astype(BF16)
    for d in range(2):
        parts = []
        for r in range(DN_REP):
            qd = qk * dec[(d, r)]
            parts.append(jnp.concatenate([qd[c * C:(c + 1) * C, c * C:(c + 1) * C] for c in range(DN_NC)],
                                         axis=0))
        qk_refs[d][...] = jnp.concatenate(parts, axis=1).astype(BF16)


def _dn_scan_kernel(uw_ref, qkd_ref, q_ref, k_ref, gc_ref, o_ref, s_ref, *, reverse):
    C, D = DN_CHUNK, DN_HEAD_DIM
    dirn = 1 if reverse else 0
    hgrp = pl.program_id(1)
    last = 0 if reverse else C - 1

    @pl.when(pl.program_id(2) == 0)
    def _():
        s_ref[...] = jnp.zeros_like(s_ref)

    heads = range(DN_SCAN_HG * DN_REP)
    for ci in range(DN_NC):
        c = (DN_NC - 1 - ci) if reverse else ci
        r0 = c * C
        gcb = gc_ref[r0:r0 + C, :]
        gcol, g_end, s_old, ws = {}, {}, {}, {}
        for hl in heads:
            hq = hl // DN_REP
            hv = hgrp * (DN_SCAN_HG * DN_REP) + hl
            gcol[hl] = _lane_pick(gcb, 2 * DN_V_HEADS + dirn * DN_V_HEADS + hv)
            g_end[hl] = gcol[hl][last:last + 1, :]
            qf = q_ref[r0:r0 + C, hq * D:(hq + 1) * D].astype(F32)
            w = uw_ref[r0:r0 + C, hl * 2 * D + D:(hl + 1) * 2 * D]
            s_old[hl] = s_ref[hl]
            wq = jnp.concatenate([w, (qf * jnp.exp(gcol[hl])).astype(BF16)], axis=0)
            ws[hl] = jnp.dot(wq, s_old[hl].astype(BF16), preferred_element_type=F32)
        for hl in heads:
            hq = hl // DN_REP
            u = uw_ref[r0:r0 + C, hl * 2 * D:hl * 2 * D + D].astype(F32)
            vnb = (u - ws[hl][:C]).astype(BF16)
            qkd = qkd_ref[r0:r0 + C, hl * C:(hl + 1) * C]
            o_ref[r0:r0 + C, hl * D:(hl + 1) * D] = ws[hl][C:] + jnp.dot(qkd, vnb, preferred_element_type=F32)
            kf = k_ref[r0:r0 + C, hq * D:(hq + 1) * D].astype(F32)
            k_dec_t = (kf * jnp.exp(g_end[hl] - gcol[hl])).T.astype(BF16)
            s_ref[hl] = s_old[hl] * jnp.exp(g_end[hl]) + jnp.dot(k_dec_t, vnb, preferred_element_type=F32)


def _delta_rule(qkv, gates, L):
    B, T, _ = qkv.shape
    nsc = T // DN_SC
    assert T % DN_SC == 0 and L == DN_SC and DN_QK_HEADS % DN_SCAN_HG == 0
    D, C = DN_HEAD_DIM, DN_CHUNK
    masks = _dn_masks()
    cp = functools.partial(pltpu.CompilerParams, vmem_limit_bytes=VMEM_LIMIT)
    blk = lambda w, off=0: pl.BlockSpec((None, DN_SC, w), lambda b, s, h: (b, s, off // w + h))
    uwf, uwb, qkf, qkb, gc = pl.pallas_call(
        _dn_prep_kernel,
        grid=(B, nsc, DN_QK_HEADS),
        in_specs=[blk(D), blk(D, DN_QK), blk(DN_REP * D, 2 * DN_QK),
                  pl.BlockSpec((None, DN_SC, LANES), lambda b, s, h: (b, s, 0)),
                  pl.BlockSpec(masks.shape, lambda b, s, h: (0, 0, 0, 0))],
        out_specs=[blk(DN_REP * 2 * D), blk(DN_REP * 2 * D), blk(DN_REP * C), blk(DN_REP * C),
                   pl.BlockSpec((None, DN_SC, LANES), lambda b, s, h: (b, s, 0))],
        out_shape=[jax.ShapeDtypeStruct((B, T, DN_V_HEADS * 2 * D), BF16)] * 2
        + [jax.ShapeDtypeStruct((B, T, DN_V_HEADS * C), BF16)] * 2
        + [jax.ShapeDtypeStruct((B, T, LANES), F32)],
        scratch_shapes=[pltpu.VMEM((2, DN_SC, LANES), F32), pltpu.VMEM((2, LANES, DN_SC), F32)],
        compiler_params=cp(dimension_semantics=("arbitrary", "arbitrary", "arbitrary")),
    )(qkv, qkv, qkv, gates, masks)

    outs = []
    G = DN_SCAN_HG
    for reverse, uw, qkd in ((False, uwf, qkf), (True, uwb, qkb)):
        if reverse:
            sc_of = lambda i: jnp.where(i == 0, 0, nsc - i)
        else:
            sc_of = lambda i: i
        sblk = lambda w, off=0, f=sc_of: pl.BlockSpec((None, DN_SC, w), lambda b, h, i: (b, f(i), off // w + h))
        outs.append(pl.pallas_call(
            functools.partial(_dn_scan_kernel, reverse=reverse),
            grid=(B, DN_QK_HEADS // G, nsc),
            in_specs=[sblk(G * DN_REP * 2 * D), sblk(G * DN_REP * C), sblk(G * D), sblk(G * D, DN_QK),
                      pl.BlockSpec((None, DN_SC, LANES), lambda b, h, i, f=sc_of: (b, f(i), 0))],
            out_specs=sblk(G * DN_REP * D),
            out_shape=jax.ShapeDtypeStruct((B, T, DN_V), F32),
            scratch_shapes=[pltpu.VMEM((G * DN_REP, D, D), F32)],
            compiler_params=cp(dimension_semantics=("arbitrary", "arbitrary", "arbitrary")),
        )(uw, qkd, qkv, qkv, gc))
    return outs


HG_SC = DN_SC
HG_NC = HG_SC // HG_CHUNK
HG_GRP = 4
assert HG_CHUNK == DN_CHUNK


def _hg_scan_kernel(*refs, reverse, final):
    if final:
        q_ref, f_ref, i_ref, lb_ref, tri_ref, of_ref, g_ref, ng_ref, o_ref, st_ref = refs
    else:
        q_ref, f_ref, i_ref, lb_ref, tri_ref, o_ref, st_ref = refs
    C, D = HG_CHUNK, HG_HEAD_DIM
    mid = (C - 1 - (C // 2 - 1)) if reverse else (C // 2 - 1)
    last = 0 if reverse else C - 1

    @pl.when(pl.program_id(2) == 0)
    def _():
        st_ref[...] = jnp.zeros_like(st_ref)

    tri = tri_ref[...]
    tri_c = tri[0:C, 0:C]
    heads = range(HG_GRP)
    qs, ks, bs = {}, {}, {}
    for h in heads:
        cols = slice(h * D, (h + 1) * D)
        lb = lb_ref[:, cols]
        f = lb + (1.0 - lb) * jax.nn.sigmoid(f_ref[:, cols])
        ks[h] = 1.0 - f
        qs[h] = q_ref[:, cols]
        bs[h] = jnp.dot(tri, jnp.log(f), preferred_element_type=F32, precision=lax.Precision.HIGHEST)
    for ci in range(HG_NC):
        c = (HG_NC - 1 - ci) if reverse else ci
        r0 = c * C
        rows = slice(r0, r0 + C)
        att, qd, kd, vb, dl = {}, {}, {}, {}, {}
        for h in heads:
            b = bs[h][rows]
            ref = b[mid:mid + 1, :]
            b_last = b[last:last + 1, :]
            q = qs[h][rows]
            k = ks[h][rows]
            qa = (q * jnp.exp(b - ref)).astype(BF16)
            ka = (k * jnp.exp(ref - b)).astype(BF16)
            att[h] = lax.dot_general(qa, ka, (((1,), (1,)), ((), ())), preferred_element_type=F32) * tri_c
            qd[h] = (q * jnp.exp(b)).astype(BF16)
            kd[h] = (k * jnp.exp(b_last - b)).astype(BF16)
            dl[h] = jnp.exp(b_last)
            vb[h] = i_ref[rows, h * D:(h + 1) * D]
        for h in heads:
            cols = slice(h * D, (h + 1) * D)
            st = st_ref[h]
            v16 = vb[h].astype(BF16)
            o = (jnp.dot(att[h].astype(BF16), v16, preferred_element_type=F32)
                 + lax.dot_general(qd[h], st.astype(BF16), (((1,), (1,)), ((), ())), preferred_element_type=F32))
            st_ref[h] = st * dl[h] + jnp.dot(vb[h].T.astype(BF16), kd[h], preferred_element_type=F32)
            if final:
                o = o + of_ref[rows, cols]
                y = o * lax.rsqrt(jnp.mean(o * o, axis=-1, keepdims=True) + EPS) * ng_ref[...]
                o_ref[rows, cols] = (y * jax.nn.silu(g_ref[rows, cols])).astype(o_ref.dtype)
            else:
                o_ref[rows, cols] = o


def _hgrn2_mixer(p, L, lb, norm_g):
    B, T, _ = p.shape
    nsc = T // HG_SC
    assert T % HG_SC == 0 and L == HG_SC and HG_HEADS % HG_GRP == 0
    gw = HG_GRP * HG_HEAD_DIM
    per = HG_WIDTH // gw
    tri = _dn_masks()[:, 0]
    ng = norm_g.astype(F32).reshape(1, HG_HEAD_DIM)
    lbf = lb.astype(F32)
    cp = pltpu.CompilerParams(dimension_semantics=("arbitrary", "arbitrary", "arbitrary"),
                              vmem_limit_bytes=VMEM_LIMIT)
    o_prev = None
    for reverse in (False, True):
        d = 1 if reverse else 0
        if reverse:
            sc_of = lambda i: jnp.where(i == 0, 0, nsc - i)
        else:
            sc_of = lambda i: i
        col = lambda sec, f=sc_of: pl.BlockSpec((None, HG_SC, gw), lambda b, h, i: (b, f(i), sec * per + h))
        in_specs = [col(0), col(1 + d), col(3),
                    pl.BlockSpec((None, 1, gw), lambda b, h, i: (d, 0, h)),
                    pl.BlockSpec((None, HG_SC, HG_SC), lambda b, h, i: (d, 0, 0))]
        args = [p, p, p, lbf.reshape(2, 1, HG_WIDTH), tri]
        final = reverse
        if final:
            in_specs += [pl.BlockSpec((None, HG_SC, gw), lambda b, h, i, f=sc_of: (b, f(i), h)), col(4),
                         pl.BlockSpec((1, HG_HEAD_DIM), lambda b, h, i: (0, 0))]
            args += [o_prev, p, ng]
        o_prev = pl.pallas_call(
            functools.partial(_hg_scan_kernel, reverse=reverse, final=final),
            grid=(B, HG_HEADS // HG_GRP, nsc),
            in_specs=in_specs,
            out_specs=pl.BlockSpec((None, HG_SC, gw), lambda b, h, i, f=sc_of: (b, f(i), h)),
            out_shape=jax.ShapeDtypeStruct((B, T, HG_WIDTH), BF16 if final else F32),
            scratch_shapes=[pltpu.VMEM((HG_GRP, HG_HEAD_DIM, HG_HEAD_DIM), F32)],
            compiler_params=cp,
        )(*args)
    return o_prev


CV_CW = 512
CV_PAD = 8


def _dn_conv_kernel(xm_ref, xp_ref, xn_ref, w_ref, o_ref, xs_ref, *, nsc):
    s = pl.program_id(1)
    cb = pl.program_id(2)
    base = CV_PAD + GRID_W
    ext = DN_SC + 2 * GRID_W
    is_ctx = s == 0
    has_up = s > 1
    has_dn = jnp.logical_and(s > 0, s < nsc - 1)
    xs_ref[1, 0:CV_PAD, :] = jnp.zeros((CV_PAD, CV_CW), F32)
    xs_ref[1, CV_PAD + ext:, :] = jnp.zeros((CV_PAD, CV_CW), F32)
    xs_ref[1, CV_PAD:base, :] = jnp.where(has_up, xp_ref[...], 0.0)
    xs_ref[1, base:base + DN_SC, :] = xm_ref[...]
    xs_ref[1, base + DN_SC:base + DN_SC + GRID_W, :] = jnp.where(has_dn, xn_ref[...], 0.0)
    t = lax.broadcasted_iota(jnp.int32, (ext, 1), 0) - GRID_W
    pos = jnp.where(is_ctx, t, t % GRID_W)
    ok_lf = pos > 0
    ok_rt = pos < jnp.where(is_ctx, DN_SC - 1, GRID_W - 1)
    xs_ref[0, CV_PAD:CV_PAD + ext, :] = jnp.where(ok_lf, xs_ref[1, CV_PAD - 1:CV_PAD - 1 + ext, :], 0.0)
    xs_ref[2, CV_PAD:CV_PAD + ext, :] = jnp.where(ok_rt, xs_ref[1, CV_PAD + 1:CV_PAD + 1 + ext, :], 0.0)
    rows_on = jnp.where(is_ctx, 0.0, 1.0)
    acc = jnp.zeros((DN_SC, CV_CW), F32)
    for dr in (-1, 0, 1):
        for dc in (-1, 0, 1):
            k = (dr + 1) * 3 + (dc + 1)
            w = w_ref[k:k + 1, :]
            if dr != 0:
                w = w * rows_on
            start = base + dr * GRID_W
            acc = acc + xs_ref[dc + 1, start:start + DN_SC, :] * w
    y = jax.nn.silu(acc)
    nq = DN_QK // CV_CW
    scale = jnp.where(cb < nq, DN_HEAD_DIM ** -0.5, 1.0)
    is_qk = cb < 2 * nq
    for h in range(CV_CW // LANES):
        yh = y[:, h * LANES:(h + 1) * LANES]
        rs = lax.rsqrt(jnp.sum(yh * yh, axis=-1, keepdims=True) + EPS) * scale
        o_ref[:, h * LANES:(h + 1) * LANES] = (yh * jnp.where(is_qk, rs, 1.0)).astype(o_ref.dtype)


def _dn_gate_kernel(x_ref, na_ref, dtb_ref, o_ref):
    x = x_ref[...]
    lane = lax.broadcasted_iota(jnp.int32, (1, LANES), 1)
    o_ref[...] = jnp.where(lane < 2 * DN_V_HEADS, jax.nn.sigmoid(x),
                           na_ref[...] * jax.nn.softplus(x + dtb_ref[...]))


def _dn_inputs(p, pg, L, conv_w, a_log, dt_bias):
    B, T, _ = p.shape
    nsc = T // DN_SC
    gpb = DN_SC // GRID_W
    nrow = T // GRID_W
    assert L == DN_SC and DN_QK % CV_CW == 0 and DN_QKV % CV_CW == 0 and DN_QKV % LANES == 0
    w9 = conv_w.astype(F32).reshape(9, DN_QKV)
    cp = functools.partial(pltpu.CompilerParams, vmem_limit_bytes=VMEM_LIMIT)
    qkv = pl.pallas_call(
        functools.partial(_dn_conv_kernel, nsc=nsc),
        grid=(B, nsc, DN_QKV // CV_CW),
        in_specs=[pl.BlockSpec((None, DN_SC, CV_CW), lambda b, s, c: (b, s, c)),
                  pl.BlockSpec((None, GRID_W, CV_CW), lambda b, s, c: (b, jnp.maximum(s * gpb - 1, 0), c)),
                  pl.BlockSpec((None, GRID_W, CV_CW), lambda b, s, c: (b, jnp.minimum(s * gpb + gpb, nrow - 1), c)),
                  pl.BlockSpec((9, CV_CW), lambda b, s, c: (0, c))],
        out_specs=pl.BlockSpec((None, DN_SC, CV_CW), lambda b, s, c: (b, s, c)),
        out_shape=jax.ShapeDtypeStruct((B, T, DN_QKV), BF16),
        scratch_shapes=[pltpu.VMEM((3, 2 * CV_PAD + 2 * GRID_W + DN_SC, CV_CW), F32)],
        compiler_params=cp(dimension_semantics=("arbitrary", "arbitrary", "arbitrary")),
    )(p, p, p, w9)

    zeros = jnp.zeros((2 * DN_V_HEADS,), F32)
    na = jnp.concatenate([zeros, -jnp.exp(a_log.astype(F32)).reshape(-1)]).reshape(1, LANES)
    dtb = jnp.concatenate([zeros, dt_bias.astype(F32).reshape(-1)]).reshape(1, LANES)
    gates = pl.pallas_call(
        _dn_gate_kernel,
        grid=(B, nsc),
        in_specs=[pl.BlockSpec((None, DN_SC, LANES), lambda b, s: (b, s, 0)),
                  pl.BlockSpec((1, LANES), lambda b, s: (0, 0)),
                  pl.BlockSpec((1, LANES), lambda b, s: (0, 0))],
        out_specs=pl.BlockSpec((None, DN_SC, LANES), lambda b, s: (b, s, 0)),
        out_shape=jax.ShapeDtypeStruct((B, T, LANES), F32),
        compiler_params=cp(dimension_semantics=("arbitrary", "arbitrary")),
    )(pg, na, dtb)
    return qkv, gates


PJ_TM = 768
PJ_TN = 1024
PO_TM = 384
PO_TK = 1024


def _row_mod(mods_ref, slot, row0, nrows, n_ctx):
    is_ctx = (row0 + lax.broadcasted_iota(jnp.int32, (nrows, 1), 0)) < n_ctx
    return jnp.where(is_ctx, mods_ref[0, slot:slot + 1, :], mods_ref[1, slot:slot + 1, :])


def _norm_mod(x, g, mods_ref, shift_slot, row0, n_ctx):
    y = x * lax.rsqrt(jnp.mean(x * x, axis=-1, keepdims=True) + EPS) * g
    n = x.shape[0]
    return y * (1.0 + _row_mod(mods_ref, shift_slot + 1, row0, n, n_ctx)) + _row_mod(mods_ref, shift_slot, row0, n, n_ctx)


def _in_proj_kernel(x_ref, g_ref, mods_ref, w_ref, o_ref, h_ref, *, slot, n_ctx):
    @pl.when(pl.program_id(2) == 0)
    def _():
        row0 = pl.program_id(1) * x_ref.shape[0]
        h_ref[...] = _norm_mod(x_ref[...], g_ref[...], mods_ref, slot, row0, n_ctx).astype(BF16)

    o_ref[...] = jnp.dot(h_ref[...], w_ref[...], preferred_element_type=F32)


def _in_proj(xs, g, mods, w, slot, n_ctx, col0=0, ncols=None, tn=PJ_TN):
    B, T, D = xs.shape
    N = w.shape[1] - col0 if ncols is None else ncols
    assert T % PJ_TM == 0 and N % tn == 0 and col0 % tn == 0
    cb0 = col0 // tn
    return pl.pallas_call(
        functools.partial(_in_proj_kernel, slot=slot, n_ctx=n_ctx),
        grid=(B, T // PJ_TM, N // tn),
        in_specs=[pl.BlockSpec((None, PJ_TM, D), lambda b, i, n: (b, i, 0)),
                  pl.BlockSpec((1, D), lambda b, i, n: (0, 0)),
                  pl.BlockSpec((None, 2, N_MOD, D), lambda b, i, n: (b, 0, 0, 0)),
                  pl.BlockSpec((D, tn), lambda b, i, n: (0, cb0 + n))],
        out_specs=pl.BlockSpec((None, PJ_TM, tn), lambda b, i, n: (b, i, n)),
        out_shape=jax.ShapeDtypeStruct((B, T, N), F32),
        scratch_shapes=[pltpu.VMEM((PJ_TM, D), BF16)],
        compiler_params=pltpu.CompilerParams(dimension_semantics=("arbitrary", "arbitrary", "arbitrary"),
                                             vmem_limit_bytes=VMEM_LIMIT),
    )(xs, g.astype(F32).reshape(1, D), mods, w)


def _residual(x, o, g, mods_ref, slot, row0, n_ctx):
    on = o * lax.rsqrt(jnp.mean(o * o, axis=-1, keepdims=True) + EPS) * g
    return x + _row_mod(mods_ref, slot, row0, x.shape[0], n_ctx) * on


def _out_proj_even_kernel(a_ref, b_ref, w_ref, x_ref, g_ref, mods_ref, o_ref, *, slot, n_ctx):
    lhs = jnp.concatenate([a_ref[...], b_ref[...].astype(BF16)], axis=1)
    o = jnp.dot(lhs, w_ref[...], preferred_element_type=F32)
    row0 = pl.program_id(1) * x_ref.shape[0]
    o_ref[...] = _residual(x_ref[...], o, g_ref[...], mods_ref, slot, row0, n_ctx)


def _out_proj_even(o_hg, o_s5, w, xs, g, mods, slot, n_ctx):
    B, T, D = xs.shape
    assert T % PO_TM == 0
    row = lambda wd: pl.BlockSpec((None, PO_TM, wd), lambda b, i: (b, i, 0))
    return pl.pallas_call(
        functools.partial(_out_proj_even_kernel, slot=slot, n_ctx=n_ctx),
        grid=(B, T // PO_TM),
        in_specs=[row(HG_WIDTH), row(S5_WIDTH), pl.BlockSpec((D, D), lambda b, i: (0, 0)), row(D),
                  pl.BlockSpec((1, D), lambda b, i: (0, 0)),
                  pl.BlockSpec((None, 2, N_MOD, D), lambda b, i: (b, 0, 0, 0))],
        out_specs=row(D),
        out_shape=jax.ShapeDtypeStruct((B, T, D), F32),
        compiler_params=pltpu.CompilerParams(dimension_semantics=("arbitrary", "arbitrary"),
                                             vmem_limit_bytes=VMEM_LIMIT),
    )(o_hg, o_s5, w, xs, g.astype(F32).reshape(1, D), mods)


def _out_proj_odd_kernel(of_ref, ob_ref, z_ref, ng_ref, w_ref, x_ref, g_ref, mods_ref, o_ref, acc_ref, *, slot, n_ctx):
    k = pl.program_id(2)

    @pl.when(k == 0)
    def _():
        acc_ref[...] = jnp.zeros_like(acc_ref)

    parts = []
    for h in range(of_ref.shape[1] // DN_HEAD_DIM):
        cols = slice(h * DN_HEAD_DIM, (h + 1) * DN_HEAD_DIM)
        o = of_ref[:, cols] + ob_ref[:, cols]
        y = o * lax.rsqrt(jnp.mean(o * o, axis=-1, keepdims=True) + EPS) * ng_ref[...]
        parts.append((y * jax.nn.silu(z_ref[:, cols])).astype(BF16))
    acc_ref[...] += jnp.dot(jnp.concatenate(parts, axis=1), w_ref[...], preferred_element_type=F32)

    @pl.when(k == pl.num_programs(2) - 1)
    def _():
        row0 = pl.program_id(1) * x_ref.shape[0]
        o_ref[...] = _residual(x_ref[...], acc_ref[...], g_ref[...], mods_ref, slot, row0, n_ctx)


def _out_proj_odd(o_f, o_b, p, z_col0, norm_g, w, xs, g, mods, slot, n_ctx):
    B, T, D = xs.shape
    assert T % PO_TM == 0 and DN_V % PO_TK == 0 and z_col0 % PO_TK == 0
    zb = z_col0 // PO_TK
    return pl.pallas_call(
        functools.partial(_out_proj_odd_kernel, slot=slot, n_ctx=n_ctx),
        grid=(B, T // PO_TM, DN_V // PO_TK),
        in_specs=[pl.BlockSpec((None, PO_TM, PO_TK), lambda b, i, k: (b, i, k)),
                  pl.BlockSpec((None, PO_TM, PO_TK), lambda b, i, k: (b, i, k)),
                  pl.BlockSpec((None, PO_TM, PO_TK), lambda b, i, k: (b, i, zb + k)),
                  pl.BlockSpec((1, DN_HEAD_DIM), lambda b, i, k: (0, 0)),
                  pl.BlockSpec((PO_TK, D), lambda b, i, k: (k, 0)),
                  pl.BlockSpec((None, PO_TM, D), lambda b, i, k: (b, i, 0)),
                  pl.BlockSpec((1, D), lambda b, i, k: (0, 0)),
                  pl.BlockSpec((None, 2, N_MOD, D), lambda b, i, k: (b, 0, 0, 0))],
        out_specs=pl.BlockSpec((None, PO_TM, D), lambda b, i, k: (b, i, 0)),
        out_shape=jax.ShapeDtypeStruct((B, T, D), F32),
        scratch_shapes=[pltpu.VMEM((PO_TM, D), F32)],
        compiler_params=pltpu.CompilerParams(dimension_semantics=("arbitrary", "arbitrary", "arbitrary"),
                                             vmem_limit_bytes=VMEM_LIMIT),
    )(o_f, o_b, p, norm_g.astype(F32).reshape(1, DN_HEAD_DIM), w, xs, g.astype(F32).reshape(1, D), mods)


def _post_residual_kernel(x_ref, yc_ref, yl_ref, g_ref, mods_ref, o_ref, *, slot, n_ctx):
    i = pl.program_id(1)
    row0 = i * x_ref.shape[0]
    y = jnp.where(row0 < n_ctx, yc_ref[...], yl_ref[...])
    o_ref[...] = _residual(x_ref[...], y, g_ref[...], mods_ref, slot, row0, n_ctx)


def _post_residual(xs, y_ctx, y_lat, g, mods, slot, n_ctx, latent_only):
    B, T, D = xs.shape
    tm = n_ctx
    assert T % tm == 0
    if y_ctx is None:
        y_ctx = y_lat
    row = pl.BlockSpec((None, tm, D), lambda b, i: (b, i, 0))
    lat = pl.BlockSpec((None, tm, D), lambda b, i: (b, jnp.maximum(i - 1, 0), 0))
    return pl.pallas_call(
        functools.partial(_post_residual_kernel, slot=slot, n_ctx=n_ctx),
        grid=(B, T // tm),
        in_specs=[row, pl.BlockSpec((None, tm, D), lambda b, i: (b, 0, 0)), lat,
                  pl.BlockSpec((1, D), lambda b, i: (0, 0)),
                  pl.BlockSpec((None, 2, N_MOD, D), lambda b, i: (b, 0, 0, 0))],
        out_specs=lat if latent_only else row,
        out_shape=jax.ShapeDtypeStruct((B, T - n_ctx if latent_only else T, D), F32),
        compiler_params=pltpu.CompilerParams(dimension_semantics=("arbitrary", "arbitrary"),
                                             vmem_limit_bytes=VMEM_LIMIT),
    )(xs, y_ctx, y_lat, g.astype(F32).reshape(1, D), mods)


MOE_FT = 256


def _router_kernel(x_ref, g_ref, mods_ref, rw_ref, h_ref, aff_ref, *, slot, n_ctx):
    row0 = pl.program_id(1) * x_ref.shape[0]
    h = _norm_mod(x_ref[...], g_ref[...], mods_ref, slot, row0, n_ctx)
    h_ref[...] = h
    logits = lax.dot_general(rw_ref[...], h.astype(BF16), (((1,), (1,)), ((), ())), preferred_element_type=F32)
    aff_ref[...] = jax.nn.softmax(logits, axis=0)


def _router(xs, g, mods, router_w, slot, n_ctx):
    B, T, D = xs.shape
    E = router_w.shape[1]
    row = pl.BlockSpec((None, PJ_TM, D), lambda b, i: (b, i, 0))
    return pl.pallas_call(
        functools.partial(_router_kernel, slot=slot, n_ctx=n_ctx),
        grid=(B, T // PJ_TM),
        in_specs=[row, pl.BlockSpec((1, D), lambda b, i: (0, 0)),
                  pl.BlockSpec((None, 2, N_MOD, D), lambda b, i: (b, 0, 0, 0)),
                  pl.BlockSpec((E, D), lambda b, i: (0, 0))],
        out_specs=[row, pl.BlockSpec((None, E, PJ_TM), lambda b, i: (b, 0, i))],
        out_shape=[jax.ShapeDtypeStruct((B, T, D), F32), jax.ShapeDtypeStruct((B, E, T), F32)],
        compiler_params=pltpu.CompilerParams(dimension_semantics=("arbitrary", "arbitrary"),
                                             vmem_limit_bytes=VMEM_LIMIT),
    )(xs, g.astype(F32).reshape(1, D), mods, router_w.T.astype(BF16))


def _expert_ffn_kernel(idx_ref, gate_ref, h_ref, wg_ref, wu_ref, wd_ref, o_ref, stage_ref, xs_ref, sem, *, cap):
    f = pl.program_id(2)

    @pl.when(f == 0)
    def _():
        def issue(s, carry):
            pltpu.make_async_copy(h_ref.at[pl.ds(idx_ref[0, s], 1), :], stage_ref.at[pl.ds(s, 1), :], sem).start()
            return carry

        lax.fori_loop(0, cap, issue, 0, unroll=8)
        pltpu.make_async_copy(h_ref.at[pl.ds(0, cap), :], stage_ref, sem).wait()
        xs_ref[...] = stage_ref[...].astype(BF16)

    x = xs_ref[...]
    gt = jnp.dot(x, wg_ref[...].astype(BF16), preferred_element_type=F32)
    up = jnp.dot(x, wu_ref[...].astype(BF16), preferred_element_type=F32)
    hid = (jax.nn.silu(gt) * up).astype(BF16)
    part = jnp.dot(hid, wd_ref[...].astype(BF16), preferred_element_type=F32)

    @pl.when(f == 0)
    def _():
        o_ref[...] = part

    @pl.when(f > 0)
    def _():
        o_ref[...] += part

    @pl.when(f == pl.num_programs(2) - 1)
    def _():
        o_ref[...] = o_ref[...] * gate_ref[...]


def _expert_ffn(h, rows, gate, w_gate, w_up, w_down, layer):
    R, D = h.shape
    _, E, _, FF = w_gate.shape
    B, _, cap = rows.shape
    ft = min(MOE_FT, FF)
    assert FF % ft == 0 and cap <= R
    return pl.pallas_call(
        functools.partial(_expert_ffn_kernel, cap=cap),
        grid=(E, B, FF // ft),
        in_specs=[pl.BlockSpec((None, 1, cap), lambda e, b, f: (b * E + e, 0, 0), memory_space=pltpu.SMEM),
                  pl.BlockSpec((None, None, cap, 1), lambda e, b, f: (b, e, 0, 0)),
                  pl.BlockSpec(memory_space=pl.ANY),
                  pl.BlockSpec((None, None, D, ft), lambda e, b, f: (layer, e, 0, f)),
                  pl.BlockSpec((None, None, D, ft), lambda e, b, f: (layer, e, 0, f)),
                  pl.BlockSpec((None, None, ft, D), lambda e, b, f: (layer, e, f, 0))],
        out_specs=pl.BlockSpec((None, None, cap, D), lambda e, b, f: (b, e, 0, 0)),
        out_shape=jax.ShapeDtypeStruct((B, E, cap, D), F32),
        scratch_shapes=[pltpu.VMEM((cap, D), F32), pltpu.VMEM((cap, D), BF16), pltpu.SemaphoreType.DMA],
        compiler_params=pltpu.CompilerParams(dimension_semantics=("arbitrary", "arbitrary", "arbitrary"),
                                             vmem_limit_bytes=VMEM_LIMIT),
    )(rows.reshape(B * E, 1, cap), gate.reshape(B, E, cap, 1), h, w_gate, w_up, w_down)


def _ec_moe(h, aff, row0, n, w_gate, w_up, w_down, layer):
    B, T, D = h.shape
    cap = EC_CAPACITY * n // N_EXPERTS
    gate, idx = lax.top_k(aff[:, :, row0:row0 + n], cap)
    rows = idx.astype(jnp.int32) + (jnp.arange(B, dtype=jnp.int32) * T + row0)[:, None, None]
    y = _expert_ffn(h.reshape(B * T, D), rows, gate, w_gate, w_up, w_down, layer)
    return jax.vmap(lambda ib, yb: jnp.zeros((n, D), yb.dtype).at[ib.reshape(-1)].add(yb.reshape(-1, D)))(idx, y)


def kernel(x, c, ctx, c_ctx, mod_w, mod_b, norm_g, ev_w_in, ev_w_out, hg_lb, hg_norm_g,
           s5_a_re, s5_a_im, s5_log_dt, s5_b_re, s5_b_im, s5_c_re, s5_c_im, s5_d, s5_glu_w, s5_glu_b,
           od_w_in, od_conv_w, od_a_log, od_dt_bias, od_norm_g, od_w_out,
           router_w, moe_w_gate, moe_w_up, moe_w_down):
    B, L, D = ctx.shape
    lb_all = jnp.cumsum(jax.nn.softmax(hg_lb.astype(F32), axis=0), axis=0)
    sc = jax.nn.silu(c)
    scc = jax.nn.silu(c_ctx)
    xs = jnp.concatenate([ctx, x], axis=1)
    for layer in range(DEPTH):
        last = layer == DEPTH - 1
        j = layer // 2
        m_lat = (sc @ mod_w[layer] + mod_b[layer]).reshape(B, N_MOD, D)
        m_ctx = (scc @ mod_w[layer] + mod_b[layer]).reshape(N_MOD, D)
        mods = jnp.stack([jnp.broadcast_to(m_ctx, (B, N_MOD, D)), m_lat], axis=1)
        g_pre_mix, g_post_mix, g_pre_ffn, g_post_ffn = norm_g[layer]
        if layer % 2 == 0:
            p = _in_proj(xs, g_pre_mix, mods, ev_w_in[j].astype(BF16), 0, L)
            o_hg = _hgrn2_mixer(p, L, lb_all[j], hg_norm_g[j])
            s5w = _s5_weights(s5_a_re[j], s5_a_im[j], s5_log_dt[j], s5_b_re[j], s5_b_im[j],
                              s5_c_re[j], s5_c_im[j], s5_d[j], s5_glu_w[j], s5_glu_b[j])
            o_s5 = _s5_mixer(p, 5 * HG_WIDTH, L, s5w)
            xs = _out_proj_even(o_hg, o_s5, ev_w_out[j].astype(BF16), xs, g_post_mix, mods, 2, L)
        else:
            w_in = od_w_in[j].astype(BF16)
            p = _in_proj(xs, g_pre_mix, mods, w_in, 0, L, ncols=DN_QKV + DN_V)
            pg = _in_proj(xs, g_pre_mix, mods, w_in, 0, L, col0=DN_QKV + DN_V, tn=LANES)
            qkv, gates = _dn_inputs(p, pg, L, od_conv_w[j], od_a_log[j], od_dt_bias[j])
            o_f, o_b = _delta_rule(qkv, gates, L)
            xs = _out_proj_odd(o_f, o_b, p, DN_QKV, od_norm_g[j], od_w_out[j].astype(BF16), xs, g_post_mix,
                               mods, 2, L)
        h, aff = _router(xs, g_pre_ffn, mods, router_w[layer], 3, L)
        moe = functools.partial(_ec_moe, h, aff, w_gate=moe_w_gate, w_up=moe_w_up, w_down=moe_w_down, layer=layer)
        y_ctx = None if last else moe(0, L)
        xs = _post_residual(xs, y_ctx, moe(L, xs.shape[1] - L), g_post_ffn, mods, 5, L, latent_only=last)
    return xs
```

```python
import functools
import math

import jax
import jax.numpy as jnp
from jax import lax
from jax.experimental import pallas as pl
from jax.experimental.pallas import tpu as pltpu

D_MODEL = 2048
DEPTH = 2
GRID_W = 64
N_MOD = 6
EPS = 1e-6

HG_WIDTH = D_MODEL // 2
HG_HEAD_DIM = 128
HG_HEADS = HG_WIDTH // HG_HEAD_DIM
HG_CHUNK = 64
S5_WIDTH = D_MODEL - HG_WIDTH
S5_GROUP = 16
S5_GROUPS = S5_WIDTH // S5_GROUP
S5_STATE = 64

DN_QK_HEADS = 16
DN_V_HEADS = 32
DN_HEAD_DIM = 128
DN_QK = DN_QK_HEADS * DN_HEAD_DIM
DN_V = DN_V_HEADS * DN_HEAD_DIM
DN_QKV = 2 * DN_QK + DN_V
DN_CHUNK = 64

N_EXPERTS = 16
EC_CAPACITY = 2

LANES = 128
VMEM_LIMIT = 56 * 1024 * 1024

S5_LC = 16
S5_TILES = S5_WIDTH // LANES
S5_GPT = LANES // S5_GROUP
S5_ST = S5_GPT * S5_STATE
S5_CAT = S5_LC * LANES
S5_NSPLIT = 2

F32 = jnp.float32
BF16 = jnp.bfloat16


def _s5_weights(a_re, a_im, log_dt, b_re, b_im, c_re, c_im, d_skip, glu_w, glu_b):
    G, P, C, LC, NT, GT = S5_GROUPS, S5_STATE, S5_GROUP, S5_LC, S5_TILES, S5_GPT
    ar, ai = a_re.astype(F32), a_im.astype(F32)
    dt = jnp.exp(log_dt.astype(F32))[..., None]
    steps = jnp.arange(LC, dtype=F32)
    cmul = lambda x, y: (x[0] * y[0] - x[1] * y[1], x[0] * y[1] + x[1] * y[0])

    def apow(e):
        ex = e[:, None, None, None]
        mag = jnp.exp(ex * (ar * dt)[None])
        ang = ex * (ai * dt)[None]
        return mag * jnp.cos(ang), mag * jnp.sin(ang)

    one = apow(jnp.ones((1,), F32))
    nr, ni = one[0][0] - 1.0, one[1][0]
    den = ar * ar + ai * ai
    zoh = ((nr * ar + ni * ai) / den, (ni * ar - nr * ai) / den)
    bb = cmul((zoh[0][..., None], zoh[1][..., None]), (b_re.astype(F32), b_im.astype(F32)))
    cm = (c_re.astype(F32), c_im.astype(F32))
    hi = lax.Precision.HIGHEST
    at = lambda z, idx: (z[0][idx], z[1][idx])
    lane_grp = jnp.arange(LANES) // C

    def spread(a, width):
        keep = lane_grp[:, None] == (jnp.arange(GT * width) // width)[None, :]
        return jnp.where(keep, jnp.concatenate([a] * GT, axis=-1), 0.0)

    cp = cmul((cm[0][None], cm[1][None]), at(apow(steps), (slice(None), slice(None), slice(None), None)))
    kk = (jnp.einsum('tdgcp,dgpk->dtgck', cp[0], bb[0], precision=hi)
          - jnp.einsum('tdgcp,dgpk->dtgck', cp[1], bb[1], precision=hi))
    bd = spread(jnp.swapaxes(kk, -1, -2).reshape(2, LC, NT, LANES, C), C).astype(BF16)
    s_in = jnp.arange(LC)[:, None, None]
    s_out = jnp.arange(LC)[None, :, None]
    lag = jnp.arange(LC)[None, None, :]
    sel = jnp.concatenate([s_out - s_in == lag, s_in - s_out == lag], axis=-1).astype(BF16)
    toe = jnp.einsum('abt,tjrc->jarbc', sel, bd.reshape(2 * LC, NT, LANES, LANES), preferred_element_type=F32)
    toe = toe.astype(BF16).reshape(NT, S5_CAT, S5_CAT)

    def inject(e):
        er = jnp.stack(e, 0).reshape(2, LC, NT, GT, P, C)
        er = jnp.moveaxis(er, -1, -2).reshape(2, LC, NT, LANES, P)
        m = spread(er, P).astype(BF16)
        m = jnp.concatenate([m[0], m[1]], axis=-1)
        return jnp.swapaxes(m, 0, 1).reshape(NT, S5_CAT, 2 * S5_ST)

    def pw_dir(e, d):
        z = apow(e)
        return z[0][:, d][..., None], z[1][:, d][..., None]

    inj_f = inject(cmul(pw_dir(LC - 1 - steps, 0), (bb[0][0][None], bb[1][0][None])))
    inj_b = inject(cmul(pw_dir(steps, 1), (bb[0][1][None], bb[1][1][None])))

    def readout(w):
        wr = jnp.stack([w[0], -w[1]], 0).reshape(2, LC, NT, LANES, P)
        n = spread(wr, P).astype(BF16)
        n = jnp.concatenate([n[0], n[1]], axis=-1)
        return jnp.transpose(n, (1, 3, 0, 2)).reshape(NT, 2 * S5_ST, S5_CAT)

    def pw_row(e, d):
        z = apow(e)
        return z[0][:, d][:, :, None, :], z[1][:, d][:, :, None, :]

    rd_f = readout(cmul((cm[0][0][None], cm[1][0][None]), pw_row(steps + 1.0, 0)))
    rd_b = readout(cmul((cm[0][1][None], cm[1][1][None]), pw_row(LC - steps, 1)))

    w1 = jnp.concatenate([toe, inj_f, inj_b], axis=-1)
    w2 = jnp.concatenate([rd_f, rd_b], axis=1)
    a_lc = apow(jnp.full((1,), LC, F32))
    a16 = jnp.stack([a_lc[0][0, 0], a_lc[1][0, 0], a_lc[0][0, 1], a_lc[1][0, 1]], 0)
    a16 = a16.reshape(4, NT, S5_ST).transpose(1, 0, 2)
    gw = spread(glu_w.astype(F32).reshape(NT, LANES, C), C).astype(BF16)
    dk = d_skip.astype(F32).reshape(NT, 1, LANES)
    gb = glu_b.astype(F32).reshape(NT, 1, LANES)
    return w1, w2, a16, dk, gw, gb


def _s5_inject_kernel(u_ref, w_ref, r_ref, ucat_ref, *, nb, nch):
    @pl.when(pl.program_id(1) == 0)
    def _():
        for b in range(nb):
            for s in range(S5_LC):
                ucat_ref[b * nch:(b + 1) * nch, s * LANES:(s + 1) * LANES] = (
                    u_ref[b, pl.ds(s, nch, stride=S5_LC), :].astype(BF16))

    r = jnp.dot(ucat_ref[...], w_ref[...], preferred_element_type=F32)
    for b in range(nb):
        r_ref[b] = r[b * nch:(b + 1) * nch]


def _s5_scan_kernel(s_ref, a_ref, x_ref, *, nch, nctx):
    st = S5_ST
    arf, aif, arb, aib = a_ref[0:1, :], a_ref[1:2, :], a_ref[2:3, :], a_ref[3:4, :]

    def body(i, carry):
        fr, fi, br, bi = carry
        kf = i
        kb = jnp.where(i < nctx, nctx - 1 - i, nch + nctx - 1 - i)
        x_ref[pl.ds(kf, 1), 0:st] = fr
        x_ref[pl.ds(kf, 1), st:2 * st] = fi
        x_ref[pl.ds(kb, 1), 2 * st:3 * st] = br
        x_ref[pl.ds(kb, 1), 3 * st:4 * st] = bi
        sfr = s_ref[pl.ds(kf, 1), 0:st]
        sfi = s_ref[pl.ds(kf, 1), st:2 * st]
        sbr = s_ref[pl.ds(kb, 1), 2 * st:3 * st]
        sbi = s_ref[pl.ds(kb, 1), 3 * st:4 * st]
        return (arf * fr - aif * fi + sfr, arf * fi + aif * fr + sfi,
                arb * br - aib * bi + sbr, arb * bi + aib * br + sbi)

    z = jnp.zeros((1, st), F32)
    lax.fori_loop(0, nch, body, (z, z, z, z))


def _s5_readout_kernel(yi_ref, x_ref, w_ref, u_ref, d_ref, gw_ref, gb_ref, o_ref, *, nch):
    half = pl.program_id(2)
    y = yi_ref[...] + jnp.dot(x_ref[...].astype(BF16), w_ref[...], preferred_element_type=F32)
    per = S5_LC // S5_NSPLIT
    for sl in range(per):
        s = half * per + sl
        ys = y[:, sl * LANES:(sl + 1) * LANES] + d_ref[...] * u_ref[pl.ds(s, nch, stride=S5_LC), :]
        ys = jax.nn.gelu(ys)
        z = jnp.dot(ys.astype(BF16), gw_ref[...], preferred_element_type=F32) + gb_ref[...]
        o_ref[pl.ds(s, nch, stride=S5_LC), :] = ys * jax.nn.sigmoid(z)


def _s5_mixer(p, u_col0, L, weights):
    w1, w2, a16, dk, gw, gb = weights
    B, T, _ = p.shape
    nch = T // S5_LC
    nctx = L // S5_LC
    assert T % S5_LC == 0 and L % S5_LC == 0 and u_col0 % LANES == 0
    ucol = u_col0 // LANES
    n1 = w1.shape[-1]
    nblk1 = 1024
    cparams = functools.partial(pltpu.CompilerParams, vmem_limit_bytes=VMEM_LIMIT)

    r = pl.pallas_call(
        functools.partial(_s5_inject_kernel, nb=B, nch=nch),
        grid=(S5_TILES, n1 // nblk1),
        in_specs=[pl.BlockSpec((B, T, LANES), lambda j, n: (0, 0, ucol + j)),
                  pl.BlockSpec((None, S5_CAT, nblk1), lambda j, n: (j, 0, n))],
        out_specs=pl.BlockSpec((B, None, nch, nblk1), lambda j, n: (0, j, 0, n)),
        out_shape=jax.ShapeDtypeStruct((B, S5_TILES, nch, n1), F32),
        scratch_shapes=[pltpu.VMEM((B * nch, S5_CAT), BF16)],
        compiler_params=cparams(dimension_semantics=("arbitrary", "arbitrary")),
    )(p, w1)

    sblk = S5_CAT // (4 * S5_ST)
    assert S5_CAT % (4 * S5_ST) == 0
    xin = pl.pallas_call(
        functools.partial(_s5_scan_kernel, nch=nch, nctx=nctx),
        grid=(B, S5_TILES),
        in_specs=[pl.BlockSpec((None, None, nch, 4 * S5_ST), lambda b, j: (b, j, 0, sblk)),
                  pl.BlockSpec((None, 4, S5_ST), lambda b, j: (j, 0, 0))],
        out_specs=pl.BlockSpec((None, None, nch, 4 * S5_ST), lambda b, j: (b, j, 0, 0)),
        out_shape=jax.ShapeDtypeStruct((B, S5_TILES, nch, 4 * S5_ST), F32),
        compiler_params=cparams(dimension_semantics=("arbitrary", "arbitrary")),
    )(r, a16)

    ncol = S5_CAT // S5_NSPLIT
    return pl.pallas_call(
        functools.partial(_s5_readout_kernel, nch=nch),
        grid=(S5_TILES, B, S5_NSPLIT),
        in_specs=[pl.BlockSpec((None, None, nch, ncol), lambda j, b, h: (b, j, 0, h)),
                  pl.BlockSpec((None, None, nch, 4 * S5_ST), lambda j, b, h: (b, j, 0, 0)),
                  pl.BlockSpec((None, 4 * S5_ST, ncol), lambda j, b, h: (j, 0, h)),
                  pl.BlockSpec((None, T, LANES), lambda j, b, h: (b, 0, ucol + j)),
                  pl.BlockSpec((None, 1, LANES), lambda j, b, h: (j, 0, 0)),
                  pl.BlockSpec((None, LANES, LANES), lambda j, b, h: (j, 0, 0)),
                  pl.BlockSpec((None, 1, LANES), lambda j, b, h: (j, 0, 0))],
        out_specs=pl.BlockSpec((None, T, LANES), lambda j, b, h: (b, 0, j)),
        out_shape=jax.ShapeDtypeStruct((B, T, S5_WIDTH), F32),
        compiler_params=cparams(dimension_semantics=("arbitrary", "arbitrary", "arbitrary")),
    )(r, xin, w2, p, dk, gw, gb)


DN_SC = 256
DN_NC = DN_SC // DN_CHUNK
DN_REP = DN_V_HEADS // DN_QK_HEADS
DN_LEVELS = DN_CHUNK.bit_length() - 1
DN_SCAN_HG = 8
DN_PREP_HG = 2


def _dn_masks():
    out = []
    for reverse in (False, True):
        i = jnp.arange(DN_SC)[:, None]
        j = jnp.arange(DN_SC)[None, :]
        same = (i // DN_CHUNK) == (j // DN_CHUNK)
        if reverse:
            i, j = j, i
        ms = [same & (i >= j), same & (i > j)]
        for lv in range(DN_LEVELS):
            ms.append(same & ((i >> (lv + 1)) == (j >> (lv + 1))) & (((i >> lv) & 1) == 1) & (((j >> lv) & 1) == 0))
        out.append(jnp.stack(ms))
    return jnp.stack(out).astype(F32)


def _lane_pick(x, col):
    hot = (lax.broadcasted_iota(jnp.int32, (1, LANES), 1) == col).astype(F32)
    return jnp.sum(x * hot, axis=1, keepdims=True)


def _dn_prep_kernel(q_ref, k_ref, v_ref, g_ref, m_ref, uwf_ref, uwb_ref, qkf_ref, qkb_ref, gc_ref,
                    gcs_ref, gct_ref):
    C, D = DN_CHUNK, DN_HEAD_DIM
    hgrp = pl.program_id(2)
    gb = g_ref[...]

    @pl.when(hgrp == 0)
    def _():
        lane = lax.broadcasted_iota(jnp.int32, (1, LANES), 1)
        out = gb
        for d in range(2):
            cs = jnp.dot(m_ref[d, 0], gb, preferred_element_type=F32, precision=lax.Precision.HIGHEST)
            gcs_ref[d] = cs
            gct_ref[d] = cs.T
            lo = 2 * DN_V_HEADS + d * DN_V_HEADS
            out = jnp.where((lane >= lo) & (lane < lo + DN_V_HEADS), cs, out)
        gc_ref[...] = out

    kk, qk, kf = {}, {}, {}
    for hq in range(DN_PREP_HG):
        q = q_ref[:, hq * D:(hq + 1) * D]
        k = k_ref[:, hq * D:(hq + 1) * D]
        kk[hq] = lax.dot_general(k, k, (((1,), (1,)), ((), ())), preferred_element_type=F32)
        qk[hq] = lax.dot_general(q, k, (((1,), (1,)), ((), ())), preferred_element_type=F32)
        kf[hq] = k.astype(F32)
    uw_refs = (uwf_ref, uwb_ref)
    qk_refs = (qkf_ref, qkb_ref)
    insts = [(d, r) for d in range(2) for r in range(DN_PREP_HG * DN_REP)]
    bcol, gcol, dec, a_mat, m = {}, {}, {}, {}, {}
    for i in insts:
        d, r = i
        hv = hgrp * (DN_PREP_HG * DN_REP) + r
        incl = m_ref[d, 0]
        strict = m_ref[d, 1]
        bcol[i] = _lane_pick(gb, d * DN_V_HEADS + hv)
        gcol[i] = _lane_pick(gcs_ref[d], 2 * DN_V_HEADS + d * DN_V_HEADS + hv)
        grow = gct_ref[d, pl.ds(2 * DN_V_HEADS + d * DN_V_HEADS + hv, 1), :]
        dec[i] = incl * jnp.exp(jnp.where(incl > 0, gcol[i] - grow, 0.0))
        a_mat[i] = strict * (bcol[i] * (kk[r // DN_REP] * dec[i]))
        m[i] = (incl - strict) - a_mat[i] * m_ref[d, 2]
    for lv in range(1, DN_LEVELS):
        mb = {i: m[i].astype(BF16) for i in insts}
        x = {i: jnp.dot(mb[i], (a_mat[i] * m_ref[i[0], 2 + lv]).astype(BF16), preferred_element_type=F32)
             for i in insts}
        for i in insts:
            m[i] = m[i] - jnp.dot(x[i].astype(BF16), mb[i], preferred_element_type=F32)
    for i in insts:
        d, r = i
        egc = jnp.exp(gcol[i])
        v = v_ref[:, r * D:(r + 1) * D].astype(F32)
        rhs = jnp.concatenate([(v * bcol[i]).astype(BF16), (kf[r // DN_REP] * (bcol[i] * egc)).astype(BF16)], axis=1)
        uw = jnp.dot(m[i].astype(BF16), rhs, preferred_element_type=F32)
        uw_refs[d][:, r * 2 * D:(r + 1) * 2 * D] = uw.astype(BF16)
    for d in range(2):
        parts = []
        for r in range(DN_PREP_HG * DN_REP):
            qd = qk[r // DN_REP] * dec[(d, r)]
            parts.append(jnp.concatenate([qd[c * C:(c + 1) * C, c * C:(c + 1) * C] for c in range(DN_NC)],
                                         axis=0))
        qk_refs[d][...] = jnp.concatenate(parts, axis=1).astype(BF16)


def _dn_scan_kernel(uw_ref, qkd_ref, q_ref, k_ref, gc_ref, o_ref, s_ref, *, reverse):
    C, D = DN_CHUNK, DN_HEAD_DIM
    dirn = 1 if reverse else 0
    hgrp = pl.program_id(1)
    last = 0 if reverse else C - 1

    @pl.when(pl.program_id(2) == 0)
    def _():
        s_ref[...] = jnp.zeros_like(s_ref)

    heads = range(DN_SCAN_HG * DN_REP)
    for ci in range(DN_NC):
        c = (DN_NC - 1 - ci) if reverse else ci
        r0 = c * C
        gcb = gc_ref[r0:r0 + C, :]
        gcol, g_end, s_old, ws = {}, {}, {}, {}
        for hl in heads:
            hq = hl // DN_REP
            hv = hgrp * (DN_SCAN_HG * DN_REP) + hl
            gcol[hl] = _lane_pick(gcb, 2 * DN_V_HEADS + dirn * DN_V_HEADS + hv)
            g_end[hl] = gcol[hl][last:last + 1, :]
            qf = q_ref[r0:r0 + C, hq * D:(hq + 1) * D].astype(F32)
            w = uw_ref[r0:r0 + C, hl * 2 * D + D:(hl + 1) * 2 * D]
            s_old[hl] = s_ref[hl]
            wq = jnp.concatenate([w, (qf * jnp.exp(gcol[hl])).astype(BF16)], axis=0)
            ws[hl] = jnp.dot(wq, s_old[hl].astype(BF16), preferred_element_type=F32)
        for hl in heads:
            hq = hl // DN_REP
            u = uw_ref[r0:r0 + C, hl * 2 * D:hl * 2 * D + D].astype(F32)
            vnb = (u - ws[hl][:C]).astype(BF16)
            qkd = qkd_ref[r0:r0 + C, hl * C:(hl + 1) * C]
            o_ref[r0:r0 + C, hl * D:(hl + 1) * D] = ws[hl][C:] + jnp.dot(qkd, vnb, preferred_element_type=F32)
            kf = k_ref[r0:r0 + C, hq * D:(hq + 1) * D].astype(F32)
            k_dec_t = (kf * jnp.exp(g_end[hl] - gcol[hl])).T.astype(BF16)
            s_ref[hl] = s_old[hl] * jnp.exp(g_end[hl]) + jnp.dot(k_dec_t, vnb, preferred_element_type=F32)


def _delta_rule(qkv, gates, L):
    B, T, _ = qkv.shape
    nsc = T // DN_SC
    assert T % DN_SC == 0 and L == DN_SC and DN_QK_HEADS % DN_SCAN_HG == 0 and DN_QK_HEADS % DN_PREP_HG == 0
    D, C = DN_HEAD_DIM, DN_CHUNK
    masks = _dn_masks()
    cp = functools.partial(pltpu.CompilerParams, vmem_limit_bytes=VMEM_LIMIT)
    P = DN_PREP_HG
    blk = lambda w, off=0: pl.BlockSpec((None, DN_SC, w), lambda b, s, h: (b, s, off // w + h))
    uwf, uwb, qkf, qkb, gc = pl.pallas_call(
        _dn_prep_kernel,
        grid=(B, nsc, DN_QK_HEADS // P),
        in_specs=[blk(P * D), blk(P * D, DN_QK), blk(P * DN_REP * D, 2 * DN_QK),
                  pl.BlockSpec((None, DN_SC, LANES), lambda b, s, h: (b, s, 0)),
                  pl.BlockSpec(masks.shape, lambda b, s, h: (0, 0, 0, 0))],
        out_specs=[blk(P * DN_REP * 2 * D), blk(P * DN_REP * 2 * D), blk(P * DN_REP * C), blk(P * DN_REP * C),
                   pl.BlockSpec((None, DN_SC, LANES), lambda b, s, h: (b, s, 0))],
        out_shape=[jax.ShapeDtypeStruct((B, T, DN_V_HEADS * 2 * D), BF16)] * 2
        + [jax.ShapeDtypeStruct((B, T, DN_V_HEADS * C), BF16)] * 2
        + [jax.ShapeDtypeStruct((B, T, LANES), F32)],
        scratch_shapes=[pltpu.VMEM((2, DN_SC, LANES), F32), pltpu.VMEM((2, LANES, DN_SC), F32)],
        compiler_params=cp(dimension_semantics=("arbitrary", "arbitrary", "arbitrary")),
    )(qkv, qkv, qkv, gates, masks)

    outs = []
    G = DN_SCAN_HG
    for reverse, uw, qkd in ((False, uwf, qkf), (True, uwb, qkb)):
        if reverse:
            sc_of = lambda i: jnp.where(i == 0, 0, nsc - i)
        else:
            sc_of = lambda i: i
        sblk = lambda w, off=0, f=sc_of: pl.BlockSpec((None, DN_SC, w), lambda b, h, i: (b, f(i), off // w + h))
        outs.append(pl.pallas_call(
            functools.partial(_dn_scan_kernel, reverse=reverse),
            grid=(B, DN_QK_HEADS // G, nsc),
            in_specs=[sblk(G * DN_REP * 2 * D), sblk(G * DN_REP * C), sblk(G * D), sblk(G * D, DN_QK),
                      pl.BlockSpec((None, DN_SC, LANES), lambda b, h, i, f=sc_of: (b, f(i), 0))],
            out_specs=sblk(G * DN_REP * D),
            out_shape=jax.ShapeDtypeStruct((B, T, DN_V), F32),
            scratch_shapes=[pltpu.VMEM((G * DN_REP, D, D), F32)],
            compiler_params=cp(dimension_semantics=("arbitrary", "arbitrary", "arbitrary")),
        )(uw, qkd, qkv, qkv, gc))
    return outs


HG_SC = DN_SC
HG_NC = HG_SC // HG_CHUNK
HG_GRP = 8
assert HG_CHUNK == DN_CHUNK


def _hg_scan_kernel(*refs, reverse, final):
    if final:
        q_ref, f_ref, i_ref, lb_ref, tri_ref, of_ref, g_ref, ng_ref, o_ref, st_ref = refs
    else:
        q_ref, f_ref, i_ref, lb_ref, tri_ref, o_ref, st_ref = refs
    C, D = HG_CHUNK, HG_HEAD_DIM
    mid = (C - 1 - (C // 2 - 1)) if reverse else (C // 2 - 1)
    last = 0 if reverse else C - 1

    @pl.when(pl.program_id(2) == 0)
    def _():
        st_ref[...] = jnp.zeros_like(st_ref)

    tri = tri_ref[...]
    tri_c = tri[0:C, 0:C]
    heads = range(HG_GRP)
    qs, ks, bs = {}, {}, {}
    for h in heads:
        cols = slice(h * D, (h + 1) * D)
        lb = lb_ref[:, cols]
        f = lb + (1.0 - lb) * jax.nn.sigmoid(f_ref[:, cols])
        ks[h] = 1.0 - f
        qs[h] = q_ref[:, cols]
        bs[h] = jnp.dot(tri, jnp.log(f), preferred_element_type=F32, precision=lax.Precision.HIGHEST)
    for ci in range(HG_NC):
        c = (HG_NC - 1 - ci) if reverse else ci
        r0 = c * C
        rows = slice(r0, r0 + C)
        att, qd, kd, vb, dl = {}, {}, {}, {}, {}
        for h in heads:
            b = bs[h][rows]
            ref = b[mid:mid + 1, :]
            b_last = b[last:last + 1, :]
            q = qs[h][rows]
            k = ks[h][rows]
            qa = (q * jnp.exp(b - ref)).astype(BF16)
            ka = (k * jnp.exp(ref - b)).astype(BF16)
            att[h] = lax.dot_general(qa, ka, (((1,), (1,)), ((), ())), preferred_element_type=F32) * tri_c
            qd[h] = (q * jnp.exp(b)).astype(BF16)
            kd[h] = (k * jnp.exp(b_last - b)).astype(BF16)
            dl[h] = jnp.exp(b_last)
            vb[h] = i_ref[rows, h * D:(h + 1) * D]
        for h in heads:
            cols = slice(h * D, (h + 1) * D)
            st = st_ref[h]
            v16 = vb[h].astype(BF16)
            o = (jnp.dot(att[h].astype(BF16), v16, preferred_element_type=F32)
                 + lax.dot_general(qd[h], st.astype(BF16), (((1,), (1,)), ((), ())), preferred_element_type=F32))
            st_ref[h] = st * dl[h] + jnp.dot(vb[h].T.astype(BF16), kd[h], preferred_element_type=F32)
            if final:
                o = o + of_ref[rows, cols]
                y = o * lax.rsqrt(jnp.mean(o * o, axis=-1, keepdims=True) + EPS) * ng_ref[...]
                o_ref[rows, cols] = (y * jax.nn.silu(g_ref[rows, cols])).astype(o_ref.dtype)
            else:
                o_ref[rows, cols] = o


def _hgrn2_mixer(p, L, lb, norm_g):
    B, T, _ = p.shape
    nsc = T // HG_SC
    assert T % HG_SC == 0 and L == HG_SC and HG_HEADS % HG_GRP == 0
    gw = HG_GRP * HG_HEAD_DIM
    per = HG_WIDTH // gw
    tri = _dn_masks()[:, 0]
    ng = norm_g.astype(F32).reshape(1, HG_HEAD_DIM)
    lbf = lb.astype(F32)
    cp = pltpu.CompilerParams(dimension_semantics=("arbitrary", "arbitrary", "arbitrary"),
                              vmem_limit_bytes=VMEM_LIMIT)
    o_prev = None
    for reverse in (False, True):
        d = 1 if reverse else 0
        if reverse:
            sc_of = lambda i: jnp.where(i == 0, 0, nsc - i)
        else:
            sc_of = lambda i: i
        col = lambda sec, f=sc_of: pl.BlockSpec((None, HG_SC, gw), lambda b, h, i: (b, f(i), sec * per + h))
        in_specs = [col(0), col(1 + d), col(3),
                    pl.BlockSpec((None, 1, gw), lambda b, h, i: (d, 0, h)),
                    pl.BlockSpec((None, HG_SC, HG_SC), lambda b, h, i: (d, 0, 0))]
        args = [p, p, p, lbf.reshape(2, 1, HG_WIDTH), tri]
        final = reverse
        if final:
            in_specs += [pl.BlockSpec((None, HG_SC, gw), lambda b, h, i, f=sc_of: (b, f(i), h)), col(4),
                         pl.BlockSpec((1, HG_HEAD_DIM), lambda b, h, i: (0, 0))]
            args += [o_prev, p, ng]
        o_prev = pl.pallas_call(
            functools.partial(_hg_scan_kernel, reverse=reverse, final=final),
            grid=(B, HG_HEADS // HG_GRP, nsc),
            in_specs=in_specs,
            out_specs=pl.BlockSpec((None, HG_SC, gw), lambda b, h, i, f=sc_of: (b, f(i), h)),
            out_shape=jax.ShapeDtypeStruct((B, T, HG_WIDTH), BF16 if final else F32),
            scratch_shapes=[pltpu.VMEM((HG_GRP, HG_HEAD_DIM, HG_HEAD_DIM), F32)],
            compiler_params=cp,
        )(*args)
    return o_prev


CV_CW = 1024
CV_PAD = 8


def _dn_conv_kernel(xm_ref, xp_ref, xn_ref, w_ref, o_ref, xs_ref, *, nsc):
    s = pl.program_id(1)
    cb = pl.program_id(2)
    base = CV_PAD + GRID_W
    ext = DN_SC + 2 * GRID_W
    is_ctx = s == 0
    has_up = s > 1
    has_dn = jnp.logical_and(s > 0, s < nsc - 1)
    xs_ref[1, 0:CV_PAD, :] = jnp.zeros((CV_PAD, CV_CW), F32)
    xs_ref[1, CV_PAD + ext:, :] = jnp.zeros((CV_PAD, CV_CW), F32)
    xs_ref[1, CV_PAD:base, :] = jnp.where(has_up, xp_ref[...], 0.0)
    xs_ref[1, base:base + DN_SC, :] = xm_ref[...]
    xs_ref[1, base + DN_SC:base + DN_SC + GRID_W, :] = jnp.where(has_dn, xn_ref[...], 0.0)
    t = lax.broadcasted_iota(jnp.int32, (ext, 1), 0) - GRID_W
    pos = jnp.where(is_ctx, t, t % GRID_W)
    ok_lf = pos > 0
    ok_rt = pos < jnp.where(is_ctx, DN_SC - 1, GRID_W - 1)
    xs_ref[0, CV_PAD:CV_PAD + ext, :] = jnp.where(ok_lf, xs_ref[1, CV_PAD - 1:CV_PAD - 1 + ext, :], 0.0)
    xs_ref[2, CV_PAD:CV_PAD + ext, :] = jnp.where(ok_rt, xs_ref[1, CV_PAD + 1:CV_PAD + 1 + ext, :], 0.0)
    rows_on = jnp.where(is_ctx, 0.0, 1.0)
    acc = jnp.zeros((DN_SC, CV_CW), F32)
    for dr in (-1, 0, 1):
        for dc in (-1, 0, 1):
            k = (dr + 1) * 3 + (dc + 1)
            w = w_ref[k:k + 1, :]
            if dr != 0:
                w = w * rows_on
            start = base + dr * GRID_W
            acc = acc + xs_ref[dc + 1, start:start + DN_SC, :] * w
    y = jax.nn.silu(acc)
    nq = DN_QK // CV_CW
    scale = jnp.where(cb < nq, DN_HEAD_DIM ** -0.5, 1.0)
    is_qk = cb < 2 * nq
    for h in range(CV_CW // LANES):
        yh = y[:, h * LANES:(h + 1) * LANES]
        rs = lax.rsqrt(jnp.sum(yh * yh, axis=-1, keepdims=True) + EPS) * scale
        o_ref[:, h * LANES:(h + 1) * LANES] = (yh * jnp.where(is_qk, rs, 1.0)).astype(o_ref.dtype)


def _dn_gate_kernel(x_ref, na_ref, dtb_ref, o_ref):
    x = x_ref[...]
    lane = lax.broadcasted_iota(jnp.int32, (1, LANES), 1)
    o_ref[...] = jnp.where(lane < 2 * DN_V_HEADS, jax.nn.sigmoid(x),
                           na_ref[...] * jax.nn.softplus(x + dtb_ref[...]))


def _dn_inputs(p, pg, L, conv_w, a_log, dt_bias):
    B, T, _ = p.shape
    nsc = T // DN_SC
    gpb = DN_SC // GRID_W
    nrow = T // GRID_W
    assert L == DN_SC and DN_QK % CV_CW == 0 and DN_QKV % CV_CW == 0 and DN_QKV % LANES == 0
    w9 = conv_w.astype(F32).reshape(9, DN_QKV)
    cp = functools.partial(pltpu.CompilerParams, vmem_limit_bytes=VMEM_LIMIT)
    qkv = pl.pallas_call(
        functools.partial(_dn_conv_kernel, nsc=nsc),
        grid=(B, nsc, DN_QKV // CV_CW),
        in_specs=[pl.BlockSpec((None, DN_SC, CV_CW), lambda b, s, c: (b, s, c)),
                  pl.BlockSpec((None, GRID_W, CV_CW), lambda b, s, c: (b, jnp.maximum(s * gpb - 1, 0), c)),
                  pl.BlockSpec((None, GRID_W, CV_CW), lambda b, s, c: (b, jnp.minimum(s * gpb + gpb, nrow - 1), c)),
                  pl.BlockSpec((9, CV_CW), lambda b, s, c: (0, c))],
        out_specs=pl.BlockSpec((None, DN_SC, CV_CW), lambda b, s, c: (b, s, c)),
        out_shape=jax.ShapeDtypeStruct((B, T, DN_QKV), BF16),
        scratch_shapes=[pltpu.VMEM((3, 2 * CV_PAD + 2 * GRID_W + DN_SC, CV_CW), F32)],
        compiler_params=cp(dimension_semantics=("arbitrary", "arbitrary", "arbitrary")),
    )(p, p, p, w9)

    zeros = jnp.zeros((2 * DN_V_HEADS,), F32)
    na = jnp.concatenate([zeros, -jnp.exp(a_log.astype(F32)).reshape(-1)]).reshape(1, LANES)
    dtb = jnp.concatenate([zeros, dt_bias.astype(F32).reshape(-1)]).reshape(1, LANES)
    gates = pl.pallas_call(
        _dn_gate_kernel,
        grid=(B, nsc),
        in_specs=[pl.BlockSpec((None, DN_SC, LANES), lambda b, s: (b, s, 0)),
                  pl.BlockSpec((1, LANES), lambda b, s: (0, 0)),
                  pl.BlockSpec((1, LANES), lambda b, s: (0, 0))],
        out_specs=pl.BlockSpec((None, DN_SC, LANES), lambda b, s: (b, s, 0)),
        out_shape=jax.ShapeDtypeStruct((B, T, LANES), F32),
        compiler_params=cp(dimension_semantics=("arbitrary", "arbitrary")),
    )(pg, na, dtb)
    return qkv, gates


PJ_TM = 768
PJ_TN = 1024
PO_TM = 384
PO_TK = 1024


def _row_mod(mods_ref, slot, row0, nrows, n_ctx):
    is_ctx = (row0 + lax.broadcasted_iota(jnp.int32, (nrows, 1), 0)) < n_ctx
    return jnp.where(is_ctx, mods_ref[0, slot:slot + 1, :], mods_ref[1, slot:slot + 1, :])


def _norm_mod(x, g, mods_ref, shift_slot, row0, n_ctx):
    y = x * lax.rsqrt(jnp.mean(x * x, axis=-1, keepdims=True) + EPS) * g
    n = x.shape[0]
    return y * (1.0 + _row_mod(mods_ref, shift_slot + 1, row0, n, n_ctx)) + _row_mod(mods_ref, shift_slot, row0, n, n_ctx)


def _in_proj_kernel(x_ref, g_ref, mods_ref, w_ref, o_ref, h_ref, *, slot, n_ctx):
    @pl.when(pl.program_id(2) == 0)
    def _():
        row0 = pl.program_id(1) * x_ref.shape[0]
        h_ref[...] = _norm_mod(x_ref[...], g_ref[...], mods_ref, slot, row0, n_ctx).astype(BF16)

    o_ref[...] = jnp.dot(h_ref[...], w_ref[...], preferred_element_type=F32)


def _in_proj(xs, g, mods, w, slot, n_ctx, col0=0, ncols=None, tn=PJ_TN):
    B, T, D = xs.shape
    N = w.shape[1] - col0 if ncols is None else ncols
    assert T % PJ_TM == 0 and N % tn == 0 and col0 % tn == 0
    cb0 = col0 // tn
    return pl.pallas_call(
        functools.partial(_in_proj_kernel, slot=slot, n_ctx=n_ctx),
        grid=(B, T // PJ_TM, N // tn),
        in_specs=[pl.BlockSpec((None, PJ_TM, D), lambda b, i, n: (b, i, 0)),
                  pl.BlockSpec((1, D), lambda b, i, n: (0, 0)),
                  pl.BlockSpec((None, 2, N_MOD, D), lambda b, i, n: (b, 0, 0, 0)),
                  pl.BlockSpec((D, tn), lambda b, i, n: (0, cb0 + n))],
        out_specs=pl.BlockSpec((None, PJ_TM, tn), lambda b, i, n: (b, i, n)),
        out_shape=jax.ShapeDtypeStruct((B, T, N), F32),
        scratch_shapes=[pltpu.VMEM((PJ_TM, D), BF16)],
        compiler_params=pltpu.CompilerParams(dimension_semantics=("arbitrary", "arbitrary", "arbitrary"),
                                             vmem_limit_bytes=VMEM_LIMIT),
    )(xs, g.astype(F32).reshape(1, D), mods, w)


def _residual(x, o, g, mods_ref, slot, row0, n_ctx):
    on = o * lax.rsqrt(jnp.mean(o * o, axis=-1, keepdims=True) + EPS) * g
    return x + _row_mod(mods_ref, slot, row0, x.shape[0], n_ctx) * on


def _out_proj_even_kernel(a_ref, b_ref, w_ref, x_ref, g_ref, mods_ref, o_ref, *, slot, n_ctx):
    lhs = jnp.concatenate([a_ref[...], b_ref[...].astype(BF16)], axis=1)
    o = jnp.dot(lhs, w_ref[...], preferred_element_type=F32)
    row0 = pl.program_id(1) * x_ref.shape[0]
    o_ref[...] = _residual(x_ref[...], o, g_ref[...], mods_ref, slot, row0, n_ctx)


def _out_proj_even(o_hg, o_s5, w, xs, g, mods, slot, n_ctx):
    B, T, D = xs.shape
    assert T % PO_TM == 0
    row = lambda wd: pl.BlockSpec((None, PO_TM, wd), lambda b, i: (b, i, 0))
    return pl.pallas_call(
        functools.partial(_out_proj_even_kernel, slot=slot, n_ctx=n_ctx),
        grid=(B, T // PO_TM),
        in_specs=[row(HG_WIDTH), row(S5_WIDTH), pl.BlockSpec((D, D), lambda b, i: (0, 0)), row(D),
                  pl.BlockSpec((1, D), lambda b, i: (0, 0)),
                  pl.BlockSpec((None, 2, N_MOD, D), lambda b, i: (b, 0, 0, 0))],
        out_specs=row(D),
        out_shape=jax.ShapeDtypeStruct((B, T, D), F32),
        compiler_params=pltpu.CompilerParams(dimension_semantics=("arbitrary", "arbitrary"),
                                             vmem_limit_bytes=VMEM_LIMIT),
    )(o_hg, o_s5, w, xs, g.astype(F32).reshape(1, D), mods)


def _out_proj_odd_kernel(of_ref, ob_ref, z_ref, ng_ref, w_ref, x_ref, g_ref, mods_ref, o_ref, acc_ref, *, slot, n_ctx):
    k = pl.program_id(2)

    @pl.when(k == 0)
    def _():
        acc_ref[...] = jnp.zeros_like(acc_ref)

    parts = []
    for h in range(of_ref.shape[1] // DN_HEAD_DIM):
        cols = slice(h * DN_HEAD_DIM, (h + 1) * DN_HEAD_DIM)
        o = of_ref[:, cols] + ob_ref[:, cols]
        y = o * lax.rsqrt(jnp.mean(o * o, axis=-1, keepdims=True) + EPS) * ng_ref[...]
        parts.append((y * jax.nn.silu(z_ref[:, cols])).astype(BF16))
    acc_ref[...] += jnp.dot(jnp.concatenate(parts, axis=1), w_ref[...], preferred_element_type=F32)

    @pl.when(k == pl.num_programs(2) - 1)
    def _():
        row0 = pl.program_id(1) * x_ref.shape[0]
        o_ref[...] = _residual(x_ref[...], acc_ref[...], g_ref[...], mods_ref, slot, row0, n_ctx)


def _out_proj_odd(o_f, o_b, p, z_col0, norm_g, w, xs, g, mods, slot, n_ctx):
    B, T, D = xs.shape
    assert T % PO_TM == 0 and DN_V % PO_TK == 0 and z_col0 % PO_TK == 0
    zb = z_col0 // PO_TK
    return pl.pallas_call(
        functools.partial(_out_proj_odd_kernel, slot=slot, n_ctx=n_ctx),
        grid=(B, T // PO_TM, DN_V // PO_TK),
        in_specs=[pl.BlockSpec((None, PO_TM, PO_TK), lambda b, i, k: (b, i, k)),
                  pl.BlockSpec((None, PO_TM, PO_TK), lambda b, i, k: (b, i, k)),
                  pl.BlockSpec((None, PO_TM, PO_TK), lambda b, i, k: (b, i, zb + k)),
                  pl.BlockSpec((1, DN_HEAD_DIM), lambda b, i, k: (0, 0)),
                  pl.BlockSpec((PO_TK, D), lambda b, i, k: (k, 0)),
                  pl.BlockSpec((None, PO_TM, D), lambda b, i, k: (b, i, 0)),
                  pl.BlockSpec((1, D), lambda b, i, k: (0, 0)),
                  pl.BlockSpec((None, 2, N_MOD, D), lambda b, i, k: (b, 0, 0, 0))],
        out_specs=pl.BlockSpec((None, PO_TM, D), lambda b, i, k: (b, i, 0)),
        out_shape=jax.ShapeDtypeStruct((B, T, D), F32),
        scratch_shapes=[pltpu.VMEM((PO_TM, D), F32)],
        compiler_params=pltpu.CompilerParams(dimension_semantics=("arbitrary", "arbitrary", "arbitrary"),
                                             vmem_limit_bytes=VMEM_LIMIT),
    )(o_f, o_b, p, norm_g.astype(F32).reshape(1, DN_HEAD_DIM), w, xs, g.astype(F32).reshape(1, D), mods)


def _post_residual_kernel(x_ref, yc_ref, yl_ref, g_ref, mods_ref, o_ref, *, slot, n_ctx):
    i = pl.program_id(1)
    row0 = i * x_ref.shape[0]
    y = jnp.where(row0 < n_ctx, yc_ref[...], yl_ref[...])
    o_ref[...] = _residual(x_ref[...], y, g_ref[...], mods_ref, slot, row0, n_ctx)


def _post_residual(xs, y_ctx, y_lat, g, mods, slot, n_ctx, latent_only):
    B, T, D = xs.shape
    tm = n_ctx
    assert T % tm == 0
    if y_ctx is None:
        y_ctx = y_lat
    row = pl.BlockSpec((None, tm, D), lambda b, i: (b, i, 0))
    lat = pl.BlockSpec((None, tm, D), lambda b, i: (b, jnp.maximum(i - 1, 0), 0))
    return pl.pallas_call(
        functools.partial(_post_residual_kernel, slot=slot, n_ctx=n_ctx),
        grid=(B, T // tm),
        in_specs=[row, pl.BlockSpec((None, tm, D), lambda b, i: (b, 0, 0)), lat,
                  pl.BlockSpec((1, D), lambda b, i: (0, 0)),
                  pl.BlockSpec((None, 2, N_MOD, D), lambda b, i: (b, 0, 0, 0))],
        out_specs=lat if latent_only else row,
        out_shape=jax.ShapeDtypeStruct((B, T - n_ctx if latent_only else T, D), F32),
        compiler_params=pltpu.CompilerParams(dimension_semantics=("arbitrary", "arbitrary"),
                                             vmem_limit_bytes=VMEM_LIMIT),
    )(xs, y_ctx, y_lat, g.astype(F32).reshape(1, D), mods)


MOE_FT = 256
MOE_MC = 256


def _router_kernel(x_ref, g_ref, mods_ref, rw_ref, h_ref, aff_ref, *, slot, n_ctx):
    row0 = pl.program_id(1) * x_ref.shape[0]
    h = _norm_mod(x_ref[...], g_ref[...], mods_ref, slot, row0, n_ctx)
    h_ref[...] = h
    logits = lax.dot_general(rw_ref[...], h.astype(BF16), (((1,), (1,)), ((), ())), preferred_element_type=F32)
    aff_ref[...] = jax.nn.softmax(logits, axis=0)


def _router(xs, g, mods, router_w, slot, n_ctx):
    B, T, D = xs.shape
    E = router_w.shape[1]
    row = pl.BlockSpec((None, PJ_TM, D), lambda b, i: (b, i, 0))
    return pl.pallas_call(
        functools.partial(_router_kernel, slot=slot, n_ctx=n_ctx),
        grid=(B, T // PJ_TM),
        in_specs=[row, pl.BlockSpec((1, D), lambda b, i: (0, 0)),
                  pl.BlockSpec((None, 2, N_MOD, D), lambda b, i: (b, 0, 0, 0)),
                  pl.BlockSpec((E, D), lambda b, i: (0, 0))],
        out_specs=[row, pl.BlockSpec((None, E, PJ_TM), lambda b, i: (b, 0, i))],
        out_shape=[jax.ShapeDtypeStruct((B, T, D), F32), jax.ShapeDtypeStruct((B, E, T), F32)],
        compiler_params=pltpu.CompilerParams(dimension_semantics=("arbitrary", "arbitrary"),
                                             vmem_limit_bytes=VMEM_LIMIT),
    )(xs, g.astype(F32).reshape(1, D), mods, router_w.T.astype(BF16))


def _expert_ffn_kernel(idx_ref, gate_ref, h_ref, wg_ref, wu_ref, wd_ref, o_ref, stage_ref, xs_ref, sem, *, cap):
    f = pl.program_id(2)

    @pl.when(f == 0)
    def _():
        def issue(s, carry):
            pltpu.make_async_copy(h_ref.at[pl.ds(idx_ref[0, s], 1), :], stage_ref.at[pl.ds(s, 1), :], sem).start()
            return carry

        lax.fori_loop(0, cap, issue, 0, unroll=8)
        pltpu.make_async_copy(h_ref.at[pl.ds(0, cap), :], stage_ref, sem).wait()
        xs_ref[...] = stage_ref[...].astype(BF16)

        o_ref[...] = jnp.zeros_like(o_ref)

    wg = wg_ref[...].astype(BF16)
    wu = wu_ref[...].astype(BF16)
    wd = wd_ref[...].astype(BF16)
    mc = min(MOE_MC, cap)
    for r0 in range(0, cap, mc):
        x = xs_ref[r0:r0 + mc, :]
        gt = jnp.dot(x, wg, preferred_element_type=F32)
        up = jnp.dot(x, wu, preferred_element_type=F32)
        hid = (jax.nn.silu(gt) * up).astype(BF16)
        o_ref[r0:r0 + mc, :] += jnp.dot(hid, wd, preferred_element_type=F32)

    @pl.when(f == pl.num_programs(2) - 1)
    def _():
        o_ref[...] = o_ref[...] * gate_ref[...]


def _expert_ffn(h, rows, gate, w_gate, w_up, w_down, layer):
    R, D = h.shape
    _, E, _, FF = w_gate.shape
    B, _, cap = rows.shape
    ft = min(MOE_FT, FF)
    assert FF % ft == 0 and cap <= R
    return pl.pallas_call(
        functools.partial(_expert_ffn_kernel, cap=cap),
        grid=(E, B, FF // ft),
        in_specs=[pl.BlockSpec((None, 1, cap), lambda e, b, f: (b * E + e, 0, 0), memory_space=pltpu.SMEM),
                  pl.BlockSpec((None, None, cap, 1), lambda e, b, f: (b, e, 0, 0)),
                  pl.BlockSpec(memory_space=pl.ANY),
                  pl.BlockSpec((None, None, D, ft), lambda e, b, f: (layer, e, 0, f)),
                  pl.BlockSpec((None, None, D, ft), lambda e, b, f: (layer, e, 0, f)),
                  pl.BlockSpec((None, None, ft, D), lambda e, b, f: (layer, e, f, 0))],
        out_specs=pl.BlockSpec((None, None, cap, D), lambda e, b, f: (b, e, 0, 0)),
        out_shape=jax.ShapeDtypeStruct((B, E, cap, D), F32),
        scratch_shapes=[pltpu.VMEM((cap, D), F32), pltpu.VMEM((cap, D), BF16), pltpu.SemaphoreType.DMA],
        compiler_params=pltpu.CompilerParams(dimension_semantics=("arbitrary", "arbitrary", "arbitrary"),
                                             vmem_limit_bytes=VMEM_LIMIT),
    )(rows.reshape(B * E, 1, cap), gate.reshape(B, E, cap, 1), h, w_gate, w_up, w_down)


def _ec_moe(h, aff, row0, n, w_gate, w_up, w_down, layer):
    B, T, D = h.shape
    cap = EC_CAPACITY * n // N_EXPERTS
    gate, idx = lax.top_k(aff[:, :, row0:row0 + n], cap)
    rows = idx.astype(jnp.int32) + (jnp.arange(B, dtype=jnp.int32) * T + row0)[:, None, None]
    y = _expert_ffn(h.reshape(B * T, D), rows, gate, w_gate, w_up, w_down, layer)
    return jax.vmap(lambda ib, yb: jnp.zeros((n, D), yb.dtype).at[ib.reshape(-1)].add(yb.reshape(-1, D)))(idx, y)


def kernel(x, c, ctx, c_ctx, mod_w, mod_b, norm_g, ev_w_in, ev_w_out, hg_lb, hg_norm_g,
           s5_a_re, s5_a_im, s5_log_dt, s5_b_re, s5_b_im, s5_c_re, s5_c_im, s5_d, s5_glu_w, s5_glu_b,
           od_w_in, od_conv_w, od_a_log, od_dt_bias, od_norm_g, od_w_out,
           router_w, moe_w_gate, moe_w_up, moe_w_down):
    B, L, D = ctx.shape
    lb_all = jnp.cumsum(jax.nn.softmax(hg_lb.astype(F32), axis=0), axis=0)
    sc = jax.nn.silu(c)
    scc = jax.nn.silu(c_ctx)
    xs = jnp.concatenate([ctx, x], axis=1)
    for layer in range(DEPTH):
        last = layer == DEPTH - 1
        j = layer // 2
        m_lat = (sc @ mod_w[layer] + mod_b[layer]).reshape(B, N_MOD, D)
        m_ctx = (scc @ mod_w[layer] + mod_b[layer]).reshape(N_MOD, D)
        mods = jnp.stack([jnp.broadcast_to(m_ctx, (B, N_MOD, D)), m_lat], axis=1)
        g_pre_mix, g_post_mix, g_pre_ffn, g_post_ffn = norm_g[layer]
        if layer % 2 == 0:
            p = _in_proj(xs, g_pre_mix, mods, ev_w_in[j].astype(BF16), 0, L)
            o_hg = _hgrn2_mixer(p, L, lb_all[j], hg_norm_g[j])
            s5w = _s5_weights(s5_a_re[j], s5_a_im[j], s5_log_dt[j], s5_b_re[j], s5_b_im[j],
                              s5_c_re[j], s5_c_im[j], s5_d[j], s5_glu_w[j], s5_glu_b[j])
            o_s5 = _s5_mixer(p, 5 * HG_WIDTH, L, s5w)
            xs = _out_proj_even(o_hg, o_s5, ev_w_out[j].astype(BF16), xs, g_post_mix, mods, 2, L)
        else:
            w_in = od_w_in[j].astype(BF16)
            p = _in_proj(xs, g_pre_mix, mods, w_in, 0, L, ncols=DN_QKV + DN_V)
            pg = _in_proj(xs, g_pre_mix, mods, w_in, 0, L, col0=DN_QKV + DN_V, tn=LANES)
            qkv, gates = _dn_inputs(p, pg, L, od_conv_w[j], od_a_log[j], od_dt_bias[j])
            o_f, o_b = _delta_rule(qkv, gates, L)
            xs = _out_proj_odd(o_f, o_b, p, DN_QKV, od_norm_g[j], od_w_out[j].astype(BF16), xs, g_post_mix,
                               mods, 2, L)
        h, aff = _router(xs, g_pre_ffn, mods, router_w[layer], 3, L)
        moe = functools.partial(_ec_moe, h, aff, w_gate=moe_w_gate, w_up=moe_w_up, w_down=moe_w_down, layer=layer)
        y_ctx = None if last else moe(0, L)
        xs = _post_residual(xs, y_ctx, moe(L, xs.shape[1] - L), g_post_ffn, mods, 5, L, latent_only=last)
    return xs
```

```python
import functools
import math

import jax
import jax.numpy as jnp
from jax import lax
from jax.experimental import pallas as pl
from jax.experimental.pallas import tpu as pltpu

D_MODEL = 2048
DEPTH = 2
GRID_W = 64
N_MOD = 6
EPS = 1e-6

HG_WIDTH = D_MODEL // 2
HG_HEAD_DIM = 128
HG_HEADS = HG_WIDTH // HG_HEAD_DIM
HG_CHUNK = 64
S5_WIDTH = D_MODEL - HG_WIDTH
S5_GROUP = 16
S5_GROUPS = S5_WIDTH // S5_GROUP
S5_STATE = 64

DN_QK_HEADS = 16
DN_V_HEADS = 32
DN_HEAD_DIM = 128
DN_QK = DN_QK_HEADS * DN_HEAD_DIM
DN_V = DN_V_HEADS * DN_HEAD_DIM
DN_QKV = 2 * DN_QK + DN_V
DN_CHUNK = 64

N_EXPERTS = 16
EC_CAPACITY = 2

LANES = 128
VMEM_LIMIT = 56 * 1024 * 1024

S5_LC = 16
S5_TILES = S5_WIDTH // LANES
S5_GPT = LANES // S5_GROUP
S5_ST = S5_GPT * S5_STATE
S5_CAT = S5_LC * LANES
S5_NSPLIT = 2

F32 = jnp.float32
BF16 = jnp.bfloat16


def _s5_weights(a_re, a_im, log_dt, b_re, b_im, c_re, c_im, d_skip, glu_w, glu_b):
    G, P, C, LC, NT, GT = S5_GROUPS, S5_STATE, S5_GROUP, S5_LC, S5_TILES, S5_GPT
    ar, ai = a_re.astype(F32), a_im.astype(F32)
    dt = jnp.exp(log_dt.astype(F32))[..., None]
    steps = jnp.arange(LC, dtype=F32)
    cmul = lambda x, y: (x[0] * y[0] - x[1] * y[1], x[0] * y[1] + x[1] * y[0])

    def apow(e):
        ex = e[:, None, None, None]
        mag = jnp.exp(ex * (ar * dt)[None])
        ang = ex * (ai * dt)[None]
        return mag * jnp.cos(ang), mag * jnp.sin(ang)

    one = apow(jnp.ones((1,), F32))
    nr, ni = one[0][0] - 1.0, one[1][0]
    den = ar * ar + ai * ai
    zoh = ((nr * ar + ni * ai) / den, (ni * ar - nr * ai) / den)
    bb = cmul((zoh[0][..., None], zoh[1][..., None]), (b_re.astype(F32), b_im.astype(F32)))
    cm = (c_re.astype(F32), c_im.astype(F32))
    hi = lax.Precision.HIGHEST
    at = lambda z, idx: (z[0][idx], z[1][idx])
    lane_grp = jnp.arange(LANES) // C

    def spread(a, width):
        keep = lane_grp[:, None] == (jnp.arange(GT * width) // width)[None, :]
        return jnp.where(keep, jnp.concatenate([a] * GT, axis=-1), 0.0)

    cp = cmul((cm[0][None], cm[1][None]), at(apow(steps), (slice(None), slice(None), slice(None), None)))
    kk = (jnp.einsum('tdgcp,dgpk->dtgck', cp[0], bb[0], precision=hi)
          - jnp.einsum('tdgcp,dgpk->dtgck', cp[1], bb[1], precision=hi))
    bd = spread(jnp.swapaxes(kk, -1, -2).reshape(2, LC, NT, LANES, C), C).astype(BF16)
    s_in = jnp.arange(LC)[:, None, None]
    s_out = jnp.arange(LC)[None, :, None]
    lag = jnp.arange(LC)[None, None, :]
    sel = jnp.concatenate([s_out - s_in == lag, s_in - s_out == lag], axis=-1).astype(BF16)
    toe = jnp.einsum('abt,tjrc->jarbc', sel, bd.reshape(2 * LC, NT, LANES, LANES), preferred_element_type=F32)
    toe = toe.astype(BF16).reshape(NT, S5_CAT, S5_CAT)

    def state_cols(zf, zb):
        return jnp.concatenate([spread(zf[0], P), spread(zf[1], P), spread(zb[0], P), spread(zb[1], P)],
                               axis=-1).astype(BF16)

    def pw_dir(e, d):
        z = apow(e)
        return z[0][:, d][..., None], z[1][:, d][..., None]

    def inj_rows(z):
        return jnp.moveaxis(z.reshape(LC, NT, GT, P, C), -1, -2).reshape(LC, NT, LANES, P)

    ef = cmul(pw_dir(LC - 1 - steps, 0), (bb[0][0][None], bb[1][0][None]))
    eb = cmul(pw_dir(steps, 1), (bb[0][1][None], bb[1][1][None]))
    inj = state_cols((inj_rows(ef[0]), inj_rows(ef[1])), (inj_rows(eb[0]), inj_rows(eb[1])))
    inj = jnp.swapaxes(inj, 0, 1).reshape(NT, S5_CAT, 4 * S5_ST)

    def pw_row(e, d):
        z = apow(e)
        return z[0][:, d][:, :, None, :], z[1][:, d][:, :, None, :]

    def rd_rows(z):
        return z.reshape(LC, NT, LANES, P)

    wf = cmul((cm[0][0][None], cm[1][0][None]), pw_row(steps + 1.0, 0))
    wb = cmul((cm[0][1][None], cm[1][1][None]), pw_row(LC - steps, 1))
    rd = state_cols((rd_rows(wf[0]), rd_rows(-wf[1])), (rd_rows(wb[0]), rd_rows(-wb[1])))
    w2 = jnp.transpose(rd, (1, 3, 0, 2)).reshape(NT, 4 * S5_ST, S5_CAT)

    a_lc = apow(jnp.full((1,), LC, F32))
    a16 = jnp.stack([a_lc[0][0, 0], a_lc[1][0, 0], a_lc[0][0, 1], a_lc[1][0, 1]], 0)
    a16 = a16.reshape(4, NT, S5_ST).transpose(1, 0, 2)
    gw = spread(glu_w.astype(F32).reshape(NT, LANES, C), C).astype(BF16)
    dk = d_skip.astype(F32).reshape(NT, 1, LANES)
    gb = glu_b.astype(F32).reshape(NT, 1, LANES)
    return toe, inj, w2, a16, dk, gw, gb


def _s5_inject_kernel(u_ref, toe_ref, inj_ref, r_ref, ucat_ref, *, nb, nch, ntoe):
    n = pl.program_id(1)

    @pl.when(n == 0)
    def _():
        for b in range(nb):
            for s in range(S5_LC):
                ucat_ref[b * nch:(b + 1) * nch, s * LANES:(s + 1) * LANES] = (
                    u_ref[b, pl.ds(s, nch, stride=S5_LC), :].astype(BF16))

    def emit(w_ref):
        r = jnp.dot(ucat_ref[...], w_ref[...], preferred_element_type=F32)
        for b in range(nb):
            r_ref[b] = r[b * nch:(b + 1) * nch]

    @pl.when(n < ntoe)
    def _():
        emit(toe_ref)

    @pl.when(n >= ntoe)
    def _():
        emit(inj_ref)


def _s5_scan_kernel(s_ref, a_ref, x_ref, *, nch, nctx):
    st = S5_ST
    arf, aif, arb, aib = a_ref[0:1, :], a_ref[1:2, :], a_ref[2:3, :], a_ref[3:4, :]

    def body(i, carry):
        fr, fi, br, bi = carry
        kf = i
        kb = jnp.where(i < nctx, nctx - 1 - i, nch + nctx - 1 - i)
        x_ref[pl.ds(kf, 1), 0:st] = fr
        x_ref[pl.ds(kf, 1), st:2 * st] = fi
        x_ref[pl.ds(kb, 1), 2 * st:3 * st] = br
        x_ref[pl.ds(kb, 1), 3 * st:4 * st] = bi
        sfr = s_ref[pl.ds(kf, 1), 0:st]
        sfi = s_ref[pl.ds(kf, 1), st:2 * st]
        sbr = s_ref[pl.ds(kb, 1), 2 * st:3 * st]
        sbi = s_ref[pl.ds(kb, 1), 3 * st:4 * st]
        return (arf * fr - aif * fi + sfr, arf * fi + aif * fr + sfi,
                arb * br - aib * bi + sbr, arb * bi + aib * br + sbi)

    z = jnp.zeros((1, st), F32)
    lax.fori_loop(0, nch, body, (z, z, z, z))


def _s5_readout_kernel(yi_ref, x_ref, w_ref, u_ref, d_ref, gw_ref, gb_ref, o_ref, *, nch):
    half = pl.program_id(2)
    y = yi_ref[...] + jnp.dot(x_ref[...].astype(BF16), w_ref[...], preferred_element_type=F32)
    per = S5_LC // S5_NSPLIT
    for sl in range(per):
        s = half * per + sl
        ys = y[:, sl * LANES:(sl + 1) * LANES] + d_ref[...] * u_ref[pl.ds(s, nch, stride=S5_LC), :]
        ys = jax.nn.gelu(ys)
        z = jnp.dot(ys.astype(BF16), gw_ref[...], preferred_element_type=F32) + gb_ref[...]
        o_ref[pl.ds(s, nch, stride=S5_LC), :] = ys * jax.nn.sigmoid(z)


def _s5_mixer(p, u_col0, L, weights):
    toe, inj, w2, a16, dk, gw, gb = weights
    B, T, _ = p.shape
    nch = T // S5_LC
    nctx = L // S5_LC
    assert T % S5_LC == 0 and L % S5_LC == 0 and u_col0 % LANES == 0
    ucol = u_col0 // LANES
    nblk1 = 1024
    ntoe = toe.shape[-1] // nblk1
    n1 = toe.shape[-1] + inj.shape[-1]
    cparams = functools.partial(pltpu.CompilerParams, vmem_limit_bytes=VMEM_LIMIT)

    r = pl.pallas_call(
        functools.partial(_s5_inject_kernel, nb=B, nch=nch, ntoe=ntoe),
        grid=(S5_TILES, n1 // nblk1),
        in_specs=[pl.BlockSpec((B, T, LANES), lambda j, n: (0, 0, ucol + j)),
                  pl.BlockSpec((None, S5_CAT, nblk1), lambda j, n: (j, 0, jnp.minimum(n, ntoe - 1))),
                  pl.BlockSpec((None, S5_CAT, nblk1), lambda j, n: (j, 0, jnp.maximum(n - ntoe, 0)))],
        out_specs=pl.BlockSpec((B, None, nch, nblk1), lambda j, n: (0, j, 0, n)),
        out_shape=jax.ShapeDtypeStruct((B, S5_TILES, nch, n1), F32),
        scratch_shapes=[pltpu.VMEM((B * nch, S5_CAT), BF16)],
        compiler_params=cparams(dimension_semantics=("arbitrary", "arbitrary")),
    )(p, toe, inj)

    sblk = S5_CAT // (4 * S5_ST)
    assert S5_CAT % (4 * S5_ST) == 0
    xin = pl.pallas_call(
        functools.partial(_s5_scan_kernel, nch=nch, nctx=nctx),
        grid=(B, S5_TILES),
        in_specs=[pl.BlockSpec((None, None, nch, 4 * S5_ST), lambda b, j: (b, j, 0, sblk)),
                  pl.BlockSpec((None, 4, S5_ST), lambda b, j: (j, 0, 0))],
        out_specs=pl.BlockSpec((None, None, nch, 4 * S5_ST), lambda b, j: (b, j, 0, 0)),
        out_shape=jax.ShapeDtypeStruct((B, S5_TILES, nch, 4 * S5_ST), F32),
        compiler_params=cparams(dimension_semantics=("arbitrary", "arbitrary")),
    )(r, a16)

    ncol = S5_CAT // S5_NSPLIT
    return pl.pallas_call(
        functools.partial(_s5_readout_kernel, nch=nch),
        grid=(S5_TILES, B, S5_NSPLIT),
        in_specs=[pl.BlockSpec((None, None, nch, ncol), lambda j, b, h: (b, j, 0, h)),
                  pl.BlockSpec((None, None, nch, 4 * S5_ST), lambda j, b, h: (b, j, 0, 0)),
                  pl.BlockSpec((None, 4 * S5_ST, ncol), lambda j, b, h: (j, 0, h)),
                  pl.BlockSpec((None, T, LANES), lambda j, b, h: (b, 0, ucol + j)),
                  pl.BlockSpec((None, 1, LANES), lambda j, b, h: (j, 0, 0)),
                  pl.BlockSpec((None, LANES, LANES), lambda j, b, h: (j, 0, 0)),
                  pl.BlockSpec((None, 1, LANES), lambda j, b, h: (j, 0, 0))],
        out_specs=pl.BlockSpec((None, T, LANES), lambda j, b, h: (b, 0, j)),
        out_shape=jax.ShapeDtypeStruct((B, T, S5_WIDTH), F32),
        compiler_params=cparams(dimension_semantics=("arbitrary", "arbitrary", "arbitrary")),
    )(r, xin, w2, p, dk, gw, gb)


DN_SC = 256
DN_NC = DN_SC // DN_CHUNK
DN_REP = DN_V_HEADS // DN_QK_HEADS
DN_LEVELS = DN_CHUNK.bit_length() - 1
DN_SCAN_HG = 8
DN_PREP_HG = 2


def _dn_masks():
    out = []
    for reverse in (False, True):
        i = jnp.arange(DN_SC)[:, None]
        j = jnp.arange(DN_SC)[None, :]
        same = (i // DN_CHUNK) == (j // DN_CHUNK)
        if reverse:
            i, j = j, i
        ms = [same & (i >= j), same & (i > j)]
        for lv in range(DN_LEVELS):
            ms.append(same & ((i >> (lv + 1)) == (j >> (lv + 1))) & (((i >> lv) & 1) == 1) & (((j >> lv) & 1) == 0))
        out.append(jnp.stack(ms))
    return jnp.stack(out).astype(F32)


def _lane_pick(x, col):
    hot = (lax.broadcasted_iota(jnp.int32, (1, LANES), 1) == col).astype(F32)
    return jnp.sum(x * hot, axis=1, keepdims=True)


def _dn_prep_kernel(q_ref, k_ref, v_ref, g_ref, m_ref, uwf_ref, uwb_ref, qkf_ref, qkb_ref, gc_ref,
                    gcs_ref, gct_ref):
    C, D = DN_CHUNK, DN_HEAD_DIM
    hgrp = pl.program_id(2)
    gb = g_ref[...]

    @pl.when(hgrp == 0)
    def _():
        lane = lax.broadcasted_iota(jnp.int32, (1, LANES), 1)
        out = gb
        for d in range(2):
            cs = jnp.dot(m_ref[d, 0], gb, preferred_element_type=F32, precision=lax.Precision.HIGHEST)
            gcs_ref[d] = cs
            gct_ref[d] = cs.T
            lo = 2 * DN_V_HEADS + d * DN_V_HEADS
            out = jnp.where((lane >= lo) & (lane < lo + DN_V_HEADS), cs, out)
        gc_ref[...] = out

    kk, qk, kf = {}, {}, {}
    for hq in range(DN_PREP_HG):
        q = q_ref[:, hq * D:(hq + 1) * D]
        k = k_ref[:, hq * D:(hq + 1) * D]
        kk[hq] = lax.dot_general(k, k, (((1,), (1,)), ((), ())), preferred_element_type=F32)
        qk[hq] = lax.dot_general(q, k, (((1,), (1,)), ((), ())), preferred_element_type=F32)
        kf[hq] = k.astype(F32)
    uw_refs = (uwf_ref, uwb_ref)
    qk_refs = (qkf_ref, qkb_ref)
    insts = [(d, r) for d in range(2) for r in range(DN_PREP_HG * DN_REP)]
    bcol, gcol, dec, a_mat, m = {}, {}, {}, {}, {}
    for i in insts:
        d, r = i
        hv = hgrp * (DN_PREP_HG * DN_REP) + r
        incl = m_ref[d, 0]
        strict = m_ref[d, 1]
        bcol[i] = _lane_pick(gb, d * DN_V_HEADS + hv)
        gcol[i] = _lane_pick(gcs_ref[d], 2 * DN_V_HEADS + d * DN_V_HEADS + hv)
        grow = gct_ref[d, pl.ds(2 * DN_V_HEADS + d * DN_V_HEADS + hv, 1), :]
        dec[i] = incl * jnp.exp(jnp.where(incl > 0, gcol[i] - grow, 0.0))
        a_mat[i] = strict * (bcol[i] * (kk[r // DN_REP] * dec[i]))
        m[i] = (incl - strict) - a_mat[i] * m_ref[d, 2]
    for lv in range(1, DN_LEVELS):
        mb = {i: m[i].astype(BF16) for i in insts}
        x = {i: jnp.dot(mb[i], (a_mat[i] * m_ref[i[0], 2 + lv]).astype(BF16), preferred_element_type=F32)
             for i in insts}
        for i in insts:
            m[i] = m[i] - jnp.dot(x[i].astype(BF16), mb[i], preferred_element_type=F32)
    for i in insts:
        d, r = i
        egc = jnp.exp(gcol[i])
        v = v_ref[:, r * D:(r + 1) * D].astype(F32)
        rhs = jnp.concatenate([(v * bcol[i]).astype(BF16), (kf[r // DN_REP] * (bcol[i] * egc)).astype(BF16)], axis=1)
        uw = jnp.dot(m[i].astype(BF16), rhs, preferred_element_type=F32)
        uw_refs[d][:, r * 2 * D:(r + 1) * 2 * D] = uw.astype(BF16)
    for d in range(2):
        parts = []
        for r in range(DN_PREP_HG * DN_REP):
            qd = qk[r // DN_REP] * dec[(d, r)]
            parts.append(jnp.concatenate([qd[c * C:(c + 1) * C, c * C:(c + 1) * C] for c in range(DN_NC)],
                                         axis=0))
        qk_refs[d][...] = jnp.concatenate(parts, axis=1).astype(BF16)


def _dn_scan_kernel(uw_ref, qkd_ref, q_ref, k_ref, gc_ref, o_ref, s_ref, *, reverse):
    C, D = DN_CHUNK, DN_HEAD_DIM
    dirn = 1 if reverse else 0
    hgrp = pl.program_id(1)
    last = 0 if reverse else C - 1

    @pl.when(pl.program_id(2) == 0)
    def _():
        s_ref[...] = jnp.zeros_like(s_ref)

    heads = range(DN_SCAN_HG * DN_REP)
    for ci in range(DN_NC):
        c = (DN_NC - 1 - ci) if reverse else ci
        r0 = c * C
        gcb = gc_ref[r0:r0 + C, :]
        gcol, g_end, s_old, ws = {}, {}, {}, {}
        for hl in heads:
            hq = hl // DN_REP
            hv = hgrp * (DN_SCAN_HG * DN_REP) + hl
            gcol[hl] = _lane_pick(gcb, 2 * DN_V_HEADS + dirn * DN_V_HEADS + hv)
            g_end[hl] = gcol[hl][last:last + 1, :]
            qf = q_ref[r0:r0 + C, hq * D:(hq + 1) * D].astype(F32)
            w = uw_ref[r0:r0 + C, hl * 2 * D + D:(hl + 1) * 2 * D]
            s_old[hl] = s_ref[hl]
            wq = jnp.concatenate([w, (qf * jnp.exp(gcol[hl])).astype(BF16)], axis=0)
            ws[hl] = jnp.dot(wq, s_old[hl].astype(BF16), preferred_element_type=F32)
        for hl in heads:
            hq = hl // DN_REP
            u = uw_ref[r0:r0 + C, hl * 2 * D:hl * 2 * D + D].astype(F32)
            vnb = (u - ws[hl][:C]).astype(BF16)
            qkd = qkd_ref[r0:r0 + C, hl * C:(hl + 1) * C]
            o_ref[r0:r0 + C, hl * D:(hl + 1) * D] = ws[hl][C:] + jnp.dot(qkd, vnb, preferred_element_type=F32)
            kf = k_ref[r0:r0 + C, hq * D:(hq + 1) * D].astype(F32)
            k_dec_t = (kf * jnp.exp(g_end[hl] - gcol[hl])).T.astype(BF16)
            s_ref[hl] = s_old[hl] * jnp.exp(g_end[hl]) + jnp.dot(k_dec_t, vnb, preferred_element_type=F32)


def _delta_rule(qkv, gates, L):
    B, T, _ = qkv.shape
    nsc = T // DN_SC
    assert T % DN_SC == 0 and L == DN_SC and DN_QK_HEADS % DN_SCAN_HG == 0 and DN_QK_HEADS % DN_PREP_HG == 0
    D, C = DN_HEAD_DIM, DN_CHUNK
    masks = _dn_masks()
    cp = functools.partial(pltpu.CompilerParams, vmem_limit_bytes=VMEM_LIMIT)
    P = DN_PREP_HG
    blk = lambda w, off=0: pl.BlockSpec((None, DN_SC, w), lambda b, s, h: (b, s, off // w + h))
    uwf, uwb, qkf, qkb, gc = pl.pallas_call(
        _dn_prep_kernel,
        grid=(B, nsc, DN_QK_HEADS // P),
        in_specs=[blk(P * D), blk(P * D, DN_QK), blk(P * DN_REP * D, 2 * DN_QK),
                  pl.BlockSpec((None, DN_SC, LANES), lambda b, s, h: (b, s, 0)),
                  pl.BlockSpec(masks.shape, lambda b, s, h: (0, 0, 0, 0))],
        out_specs=[blk(P * DN_REP * 2 * D), blk(P * DN_REP * 2 * D), blk(P * DN_REP * C), blk(P * DN_REP * C),
                   pl.BlockSpec((None, DN_SC, LANES), lambda b, s, h: (b, s, 0))],
        out_shape=[jax.ShapeDtypeStruct((B, T, DN_V_HEADS * 2 * D), BF16)] * 2
        + [jax.ShapeDtypeStruct((B, T, DN_V_HEADS * C), BF16)] * 2
        + [jax.ShapeDtypeStruct((B, T, LANES), F32)],
        scratch_shapes=[pltpu.VMEM((2, DN_SC, LANES), F32), pltpu.VMEM((2, LANES, DN_SC), F32)],
        compiler_params=cp(dimension_semantics=("arbitrary", "arbitrary", "arbitrary")),
    )(qkv, qkv, qkv, gates, masks)

    outs = []
    G = DN_SCAN_HG
    for reverse, uw, qkd in ((False, uwf, qkf), (True, uwb, qkb)):
        if reverse:
            sc_of = lambda i: jnp.where(i == 0, 0, nsc - i)
        else:
            sc_of = lambda i: i
        sblk = lambda w, off=0, f=sc_of: pl.BlockSpec((None, DN_SC, w), lambda b, h, i: (b, f(i), off // w + h))
        outs.append(pl.pallas_call(
            functools.partial(_dn_scan_kernel, reverse=reverse),
            grid=(B, DN_QK_HEADS // G, nsc),
            in_specs=[sblk(G * DN_REP * 2 * D), sblk(G * DN_REP * C), sblk(G * D), sblk(G * D, DN_QK),
                      pl.BlockSpec((None, DN_SC, LANES), lambda b, h, i, f=sc_of: (b, f(i), 0))],
            out_specs=sblk(G * DN_REP * D),
            out_shape=jax.ShapeDtypeStruct((B, T, DN_V), F32),
            scratch_shapes=[pltpu.VMEM((G * DN_REP, D, D), F32)],
            compiler_params=cp(dimension_semantics=("arbitrary", "arbitrary", "arbitrary")),
        )(uw, qkd, qkv, qkv, gc))
    return outs


HG_SC = DN_SC
HG_NC = HG_SC // HG_CHUNK
HG_GRP = 8
assert HG_CHUNK == DN_CHUNK


def _hg_scan_kernel(*refs, reverse, final):
    if final:
        q_ref, f_ref, i_ref, lb_ref, tri_ref, of_ref, g_ref, ng_ref, o_ref, st_ref = refs
    else:
        q_ref, f_ref, i_ref, lb_ref, tri_ref, o_ref, st_ref = refs
    C, D = HG_CHUNK, HG_HEAD_DIM
    mid = (C - 1 - (C // 2 - 1)) if reverse else (C // 2 - 1)
    last = 0 if reverse else C - 1

    @pl.when(pl.program_id(2) == 0)
    def _():
        st_ref[...] = jnp.zeros_like(st_ref)

    tri = tri_ref[...]
    tri_c = tri[0:C, 0:C]
    heads = range(HG_GRP)
    qs, ks, bs = {}, {}, {}
    for h in heads:
        cols = slice(h * D, (h + 1) * D)
        lb = lb_ref[:, cols]
        f = lb + (1.0 - lb) * jax.nn.sigmoid(f_ref[:, cols])
        ks[h] = 1.0 - f
        qs[h] = q_ref[:, cols]
        bs[h] = jnp.dot(tri, jnp.log(f), preferred_element_type=F32, precision=lax.Precision.HIGHEST)
    for ci in range(HG_NC):
        c = (HG_NC - 1 - ci) if reverse else ci
        r0 = c * C
        rows = slice(r0, r0 + C)
        att, qd, kd, vb, dl = {}, {}, {}, {}, {}
        for h in heads:
            b = bs[h][rows]
            ref = b[mid:mid + 1, :]
            b_last = b[last:last + 1, :]
            q = qs[h][rows]
            k = ks[h][rows]
            qa = (q * jnp.exp(b - ref)).astype(BF16)
            ka = (k * jnp.exp(ref - b)).astype(BF16)
            att[h] = lax.dot_general(qa, ka, (((1,), (1,)), ((), ())), preferred_element_type=F32) * tri_c
            qd[h] = (q * jnp.exp(b)).astype(BF16)
            kd[h] = (k * jnp.exp(b_last - b)).astype(BF16)
            dl[h] = jnp.exp(b_last)
            vb[h] = i_ref[rows, h * D:(h + 1) * D]
        for h in heads:
            cols = slice(h * D, (h + 1) * D)
            st = st_ref[h]
            v16 = vb[h].astype(BF16)
            o = (jnp.dot(att[h].astype(BF16), v16, preferred_element_type=F32)
                 + lax.dot_general(qd[h], st.astype(BF16), (((1,), (1,)), ((), ())), preferred_element_type=F32))
            st_ref[h] = st * dl[h] + jnp.dot(vb[h].T.astype(BF16), kd[h], preferred_element_type=F32)
            if final:
                o = o + of_ref[rows, cols]
                y = o * lax.rsqrt(jnp.mean(o * o, axis=-1, keepdims=True) + EPS) * ng_ref[...]
                o_ref[rows, cols] = (y * jax.nn.silu(g_ref[rows, cols])).astype(o_ref.dtype)
            else:
                o_ref[rows, cols] = o


def _hgrn2_mixer(p, L, lb, norm_g):
    B, T, _ = p.shape
    nsc = T // HG_SC
    assert T % HG_SC == 0 and L == HG_SC and HG_HEADS % HG_GRP == 0
    gw = HG_GRP * HG_HEAD_DIM
    per = HG_WIDTH // gw
    tri = _dn_masks()[:, 0]
    ng = norm_g.astype(F32).reshape(1, HG_HEAD_DIM)
    lbf = lb.astype(F32)
    cp = pltpu.CompilerParams(dimension_semantics=("arbitrary", "arbitrary", "arbitrary"),
                              vmem_limit_bytes=VMEM_LIMIT)
    o_prev = None
    for reverse in (False, True):
        d = 1 if reverse else 0
        if reverse:
            sc_of = lambda i: jnp.where(i == 0, 0, nsc - i)
        else:
            sc_of = lambda i: i
        col = lambda sec, f=sc_of: pl.BlockSpec((None, HG_SC, gw), lambda b, h, i: (b, f(i), sec * per + h))
        in_specs = [col(0), col(1 + d), col(3),
                    pl.BlockSpec((None, 1, gw), lambda b, h, i: (d, 0, h)),
                    pl.BlockSpec((None, HG_SC, HG_SC), lambda b, h, i: (d, 0, 0))]
        args = [p, p, p, lbf.reshape(2, 1, HG_WIDTH), tri]
        final = reverse
        if final:
            in_specs += [pl.BlockSpec((None, HG_SC, gw), lambda b, h, i, f=sc_of: (b, f(i), h)), col(4),
                         pl.BlockSpec((1, HG_HEAD_DIM), lambda b, h, i: (0, 0))]
            args += [o_prev, p, ng]
        o_prev = pl.pallas_call(
            functools.partial(_hg_scan_kernel, reverse=reverse, final=final),
            grid=(B, HG_HEADS // HG_GRP, nsc),
            in_specs=in_specs,
            out_specs=pl.BlockSpec((None, HG_SC, gw), lambda b, h, i, f=sc_of: (b, f(i), h)),
            out_shape=jax.ShapeDtypeStruct((B, T, HG_WIDTH), BF16 if final else F32),
            scratch_shapes=[pltpu.VMEM((HG_GRP, HG_HEAD_DIM, HG_HEAD_DIM), F32)],
            compiler_params=cp,
        )(*args)
    return o_prev


CV_CW = 1024
CV_PAD = 8


def _dn_conv_kernel(xm_ref, xp_ref, xn_ref, w_ref, o_ref, xs_ref, *, nsc):
    s = pl.program_id(1)
    cb = pl.program_id(2)
    base = CV_PAD + GRID_W
    ext = DN_SC + 2 * GRID_W
    is_ctx = s == 0
    has_up = s > 1
    has_dn = jnp.logical_and(s > 0, s < nsc - 1)
    xs_ref[1, 0:CV_PAD, :] = jnp.zeros((CV_PAD, CV_CW), F32)
    xs_ref[1, CV_PAD + ext:, :] = jnp.zeros((CV_PAD, CV_CW), F32)
    xs_ref[1, CV_PAD:base, :] = jnp.where(has_up, xp_ref[...], 0.0)
    xs_ref[1, base:base + DN_SC, :] = xm_ref[...]
    xs_ref[1, base + DN_SC:base + DN_SC + GRID_W, :] = jnp.where(has_dn, xn_ref[...], 0.0)
    t = lax.broadcasted_iota(jnp.int32, (ext, 1), 0) - GRID_W
    pos = jnp.where(is_ctx, t, t % GRID_W)
    ok_lf = pos > 0
    ok_rt = pos < jnp.where(is_ctx, jnp.int32(DN_SC - 1), jnp.int32(GRID_W - 1))
    xs_ref[0, CV_PAD:CV_PAD + ext, :] = jnp.where(ok_lf, xs_ref[1, CV_PAD - 1:CV_PAD - 1 + ext, :], 0.0)
    xs_ref[2, CV_PAD:CV_PAD + ext, :] = jnp.where(ok_rt, xs_ref[1, CV_PAD + 1:CV_PAD + 1 + ext, :], 0.0)
    rows_on = jnp.where(is_ctx, 0.0, 1.0)
    acc = jnp.zeros((DN_SC, CV_CW), F32)
    for dr in (-1, 0, 1):
        for dc in (-1, 0, 1):
            k = (dr + 1) * 3 + (dc + 1)
            w = w_ref[k:k + 1, :]
            if dr != 0:
                w = w * rows_on
            start = base + dr * GRID_W
            acc = acc + xs_ref[dc + 1, start:start + DN_SC, :] * w
    y = jax.nn.silu(acc)
    nq = DN_QK // CV_CW
    scale = jnp.where(cb < nq, DN_HEAD_DIM ** -0.5, 1.0)
    is_qk = cb < 2 * nq
    for h in range(CV_CW // LANES):
        yh = y[:, h * LANES:(h + 1) * LANES]
        rs = lax.rsqrt(jnp.sum(yh * yh, axis=-1, keepdims=True) + EPS) * scale
        o_ref[:, h * LANES:(h + 1) * LANES] = (yh * jnp.where(is_qk, rs, 1.0)).astype(o_ref.dtype)


def _dn_gate_kernel(x_ref, na_ref, dtb_ref, o_ref):
    x = x_ref[...]
    lane = lax.broadcasted_iota(jnp.int32, (1, LANES), 1)
    o_ref[...] = jnp.where(lane < 2 * DN_V_HEADS, jax.nn.sigmoid(x),
                           na_ref[...] * jax.nn.softplus(x + dtb_ref[...]))


def _dn_inputs(p, pg, L, conv_w, a_log, dt_bias):
    B, T, _ = p.shape
    nsc = T // DN_SC
    gpb = DN_SC // GRID_W
    nrow = T // GRID_W
    assert L == DN_SC and DN_QK % CV_CW == 0 and DN_QKV % CV_CW == 0 and DN_QKV % LANES == 0
    w9 = conv_w.astype(F32).reshape(9, DN_QKV)
    cp = functools.partial(pltpu.CompilerParams, vmem_limit_bytes=VMEM_LIMIT)
    qkv = pl.pallas_call(
        functools.partial(_dn_conv_kernel, nsc=nsc),
        grid=(B, nsc, DN_QKV // CV_CW),
        in_specs=[pl.BlockSpec((None, DN_SC, CV_CW), lambda b, s, c: (b, s, c)),
                  pl.BlockSpec((None, GRID_W, CV_CW), lambda b, s, c: (b, jnp.maximum(s * gpb - 1, 0), c)),
                  pl.BlockSpec((None, GRID_W, CV_CW), lambda b, s, c: (b, jnp.minimum(s * gpb + gpb, nrow - 1), c)),
                  pl.BlockSpec((9, CV_CW), lambda b, s, c: (0, c))],
        out_specs=pl.BlockSpec((None, DN_SC, CV_CW), lambda b, s, c: (b, s, c)),
        out_shape=jax.ShapeDtypeStruct((B, T, DN_QKV), BF16),
        scratch_shapes=[pltpu.VMEM((3, 2 * CV_PAD + 2 * GRID_W + DN_SC, CV_CW), F32)],
        compiler_params=cp(dimension_semantics=("arbitrary", "arbitrary", "arbitrary")),
    )(p, p, p, w9)

    zeros = jnp.zeros((2 * DN_V_HEADS,), F32)
    na = jnp.concatenate([zeros, -jnp.exp(a_log.astype(F32)).reshape(-1)]).reshape(1, LANES)
    dtb = jnp.concatenate([zeros, dt_bias.astype(F32).reshape(-1)]).reshape(1, LANES)
    gates = pl.pallas_call(
        _dn_gate_kernel,
        grid=(B, nsc),
        in_specs=[pl.BlockSpec((None, DN_SC, LANES), lambda b, s: (b, s, 0)),
                  pl.BlockSpec((1, LANES), lambda b, s: (0, 0)),
                  pl.BlockSpec((1, LANES), lambda b, s: (0, 0))],
        out_specs=pl.BlockSpec((None, DN_SC, LANES), lambda b, s: (b, s, 0)),
        out_shape=jax.ShapeDtypeStruct((B, T, LANES), F32),
        compiler_params=cp(dimension_semantics=("arbitrary", "arbitrary")),
    )(pg, na, dtb)
    return qkv, gates


PJ_TM = 768
PJ_TN = 1024
PO_TM = 384
PO_TK = 1024


def _row_mod(mods_ref, slot, row0, nrows, n_ctx):
    is_ctx = (row0 + lax.broadcasted_iota(jnp.int32, (nrows, 1), 0)) < n_ctx
    return jnp.where(is_ctx, mods_ref[0, slot:slot + 1, :], mods_ref[1, slot:slot + 1, :])


def _norm_mod(x, g, mods_ref, shift_slot, row0, n_ctx):
    y = x * lax.rsqrt(jnp.mean(x * x, axis=-1, keepdims=True) + EPS) * g
    n = x.shape[0]
    return y * (1.0 + _row_mod(mods_ref, shift_slot + 1, row0, n, n_ctx)) + _row_mod(mods_ref, shift_slot, row0, n, n_ctx)


def _in_proj_kernel(x_ref, g_ref, mods_ref, w_ref, o_ref, h_ref, *, slot, n_ctx):
    @pl.when(pl.program_id(2) == 0)
    def _():
        row0 = pl.program_id(1) * x_ref.shape[0]
        h_ref[...] = _norm_mod(x_ref[...], g_ref[...], mods_ref, slot, row0, n_ctx).astype(BF16)

    o_ref[...] = jnp.dot(h_ref[...], w_ref[...], preferred_element_type=F32)


def _in_proj(xs, g, mods, w, slot, n_ctx, col0=0, ncols=None, tn=PJ_TN):
    B, T, D = xs.shape
    N = w.shape[1] - col0 if ncols is None else ncols
    assert T % PJ_TM == 0 and N % tn == 0 and col0 % tn == 0
    cb0 = col0 // tn
    return pl.pallas_call(
        functools.partial(_in_proj_kernel, slot=slot, n_ctx=n_ctx),
        grid=(B, T // PJ_TM, N // tn),
        in_specs=[pl.BlockSpec((None, PJ_TM, D), lambda b, i, n: (b, i, 0)),
                  pl.BlockSpec((1, D), lambda b, i, n: (0, 0)),
                  pl.BlockSpec((None, 2, N_MOD, D), lambda b, i, n: (b, 0, 0, 0)),
                  pl.BlockSpec((D, tn), lambda b, i, n: (0, cb0 + n))],
        out_specs=pl.BlockSpec((None, PJ_TM, tn), lambda b, i, n: (b, i, n)),
        out_shape=jax.ShapeDtypeStruct((B, T, N), F32),
        scratch_shapes=[pltpu.VMEM((PJ_TM, D), BF16)],
        compiler_params=pltpu.CompilerParams(dimension_semantics=("arbitrary", "arbitrary", "arbitrary"),
                                             vmem_limit_bytes=VMEM_LIMIT),
    )(xs, g.astype(F32).reshape(1, D), mods, w)


def _residual(x, o, g, mods_ref, slot, row0, n_ctx):
    on = o * lax.rsqrt(jnp.mean(o * o, axis=-1, keepdims=True) + EPS) * g
    return x + _row_mod(mods_ref, slot, row0, x.shape[0], n_ctx) * on


def _out_proj_even_kernel(a_ref, b_ref, w_ref, x_ref, g_ref, mods_ref, o_ref, *, slot, n_ctx):
    lhs = jnp.concatenate([a_ref[...], b_ref[...].astype(BF16)], axis=1)
    o = jnp.dot(lhs, w_ref[...], preferred_element_type=F32)
    row0 = pl.program_id(1) * x_ref.shape[0]
    o_ref[...] = _residual(x_ref[...], o, g_ref[...], mods_ref, slot, row0, n_ctx)


def _out_proj_even(o_hg, o_s5, w, xs, g, mods, slot, n_ctx):
    B, T, D = xs.shape
    assert T % PO_TM == 0
    row = lambda wd: pl.BlockSpec((None, PO_TM, wd), lambda b, i: (b, i, 0))
    return pl.pallas_call(
        functools.partial(_out_proj_even_kernel, slot=slot, n_ctx=n_ctx),
        grid=(B, T // PO_TM),
        in_specs=[row(HG_WIDTH), row(S5_WIDTH), pl.BlockSpec((D, D), lambda b, i: (0, 0)), row(D),
                  pl.BlockSpec((1, D), lambda b, i: (0, 0)),
                  pl.BlockSpec((None, 2, N_MOD, D), lambda b, i: (b, 0, 0, 0))],
        out_specs=row(D),
        out_shape=jax.ShapeDtypeStruct((B, T, D), F32),
        compiler_params=pltpu.CompilerParams(dimension_semantics=("arbitrary", "arbitrary"),
                                             vmem_limit_bytes=VMEM_LIMIT),
    )(o_hg, o_s5, w, xs, g.astype(F32).reshape(1, D), mods)


def _out_proj_odd_kernel(of_ref, ob_ref, z_ref, ng_ref, w_ref, x_ref, g_ref, mods_ref, o_ref, acc_ref, *, slot, n_ctx):
    k = pl.program_id(2)

    @pl.when(k == 0)
    def _():
        acc_ref[...] = jnp.zeros_like(acc_ref)

    parts = []
    for h in range(of_ref.shape[1] // DN_HEAD_DIM):
        cols = slice(h * DN_HEAD_DIM, (h + 1) * DN_HEAD_DIM)
        o = of_ref[:, cols] + ob_ref[:, cols]
        y = o * lax.rsqrt(jnp.mean(o * o, axis=-1, keepdims=True) + EPS) * ng_ref[...]
        parts.append((y * jax.nn.silu(z_ref[:, cols])).astype(BF16))
    acc_ref[...] += jnp.dot(jnp.concatenate(parts, axis=1), w_ref[...], preferred_element_type=F32)

    @pl.when(k == pl.num_programs(2) - 1)
    def _():
        row0 = pl.program_id(1) * x_ref.shape[0]
        o_ref[...] = _residual(x_ref[...], acc_ref[...], g_ref[...], mods_ref, slot, row0, n_ctx)


def _out_proj_odd(o_f, o_b, p, z_col0, norm_g, w, xs, g, mods, slot, n_ctx):
    B, T, D = xs.shape
    assert T % PO_TM == 0 and DN_V % PO_TK == 0 and z_col0 % PO_TK == 0
    zb = z_col0 // PO_TK
    return pl.pallas_call(
        functools.partial(_out_proj_odd_kernel, slot=slot, n_ctx=n_ctx),
        grid=(B, T // PO_TM, DN_V // PO_TK),
        in_specs=[pl.BlockSpec((None, PO_TM, PO_TK), lambda b, i, k: (b, i, k)),
                  pl.BlockSpec((None, PO_TM, PO_TK), lambda b, i, k: (b, i, k)),
                  pl.BlockSpec((None, PO_TM, PO_TK), lambda b, i, k: (b, i, zb + k)),
                  pl.BlockSpec((1, DN_HEAD_DIM), lambda b, i, k: (0, 0)),
                  pl.BlockSpec((PO_TK, D), lambda b, i, k: (k, 0)),
                  pl.BlockSpec((None, PO_TM, D), lambda b, i, k: (b, i, 0)),
                  pl.BlockSpec((1, D), lambda b, i, k: (0, 0)),
                  pl.BlockSpec((None, 2, N_MOD, D), lambda b, i, k: (b, 0, 0, 0))],
        out_specs=pl.BlockSpec((None, PO_TM, D), lambda b, i, k: (b, i, 0)),
        out_shape=jax.ShapeDtypeStruct((B, T, D), F32),
        scratch_shapes=[pltpu.VMEM((PO_TM, D), F32)],
        compiler_params=pltpu.CompilerParams(dimension_semantics=("arbitrary", "arbitrary", "arbitrary"),
                                             vmem_limit_bytes=VMEM_LIMIT),
    )(o_f, o_b, p, norm_g.astype(F32).reshape(1, DN_HEAD_DIM), w, xs, g.astype(F32).reshape(1, D), mods)


def _post_residual_kernel(x_ref, yc_ref, yl_ref, g_ref, mods_ref, o_ref, *, slot, n_ctx):
    i = pl.program_id(1)
    row0 = i * x_ref.shape[0]
    y = jnp.where(row0 < n_ctx, yc_ref[...], yl_ref[...])
    o_ref[...] = _residual(x_ref[...], y, g_ref[...], mods_ref, slot, row0, n_ctx)


def _post_residual(xs, y_ctx, y_lat, g, mods, slot, n_ctx, latent_only):
    B, T, D = xs.shape
    tm = n_ctx
    assert T % tm == 0
    if y_ctx is None:
        y_ctx = y_lat
    row = pl.BlockSpec((None, tm, D), lambda b, i: (b, i, 0))
    lat = pl.BlockSpec((None, tm, D), lambda b, i: (b, jnp.maximum(i - 1, 0), 0))
    return pl.pallas_call(
        functools.partial(_post_residual_kernel, slot=slot, n_ctx=n_ctx),
        grid=(B, T // tm),
        in_specs=[row, pl.BlockSpec((None, tm, D), lambda b, i: (b, 0, 0)), lat,
                  pl.BlockSpec((1, D), lambda b, i: (0, 0)),
                  pl.BlockSpec((None, 2, N_MOD, D), lambda b, i: (b, 0, 0, 0))],
        out_specs=lat if latent_only else row,
        out_shape=jax.ShapeDtypeStruct((B, T - n_ctx if latent_only else T, D), F32),
        compiler_params=pltpu.CompilerParams(dimension_semantics=("arbitrary", "arbitrary"),
                                             vmem_limit_bytes=VMEM_LIMIT),
    )(xs, y_ctx, y_lat, g.astype(F32).reshape(1, D), mods)


MOE_FT = 256
MOE_MC = 256


def _router_kernel(x_ref, g_ref, mods_ref, rw_ref, h_ref, aff_ref, *, slot, n_ctx):
    row0 = pl.program_id(1) * x_ref.shape[0]
    h = _norm_mod(x_ref[...], g_ref[...], mods_ref, slot, row0, n_ctx)
    h_ref[...] = h
    logits = lax.dot_general(rw_ref[...], h.astype(BF16), (((1,), (1,)), ((), ())), preferred_element_type=F32)
    aff_ref[...] = jax.nn.softmax(logits, axis=0)


def _router(xs, g, mods, router_w, slot, n_ctx):
    B, T, D = xs.shape
    E = router_w.shape[1]
    row = pl.BlockSpec((None, PJ_TM, D), lambda b, i: (b, i, 0))
    return pl.pallas_call(
        functools.partial(_router_kernel, slot=slot, n_ctx=n_ctx),
        grid=(B, T // PJ_TM),
        in_specs=[row, pl.BlockSpec((1, D), lambda b, i: (0, 0)),
                  pl.BlockSpec((None, 2, N_MOD, D), lambda b, i: (b, 0, 0, 0)),
                  pl.BlockSpec((E, D), lambda b, i: (0, 0))],
        out_specs=[row, pl.BlockSpec((None, E, PJ_TM), lambda b, i: (b, 0, i))],
        out_shape=[jax.ShapeDtypeStruct((B, T, D), F32), jax.ShapeDtypeStruct((B, E, T), F32)],
        compiler_params=pltpu.CompilerParams(dimension_semantics=("arbitrary", "arbitrary"),
                                             vmem_limit_bytes=VMEM_LIMIT),
    )(xs, g.astype(F32).reshape(1, D), mods, router_w.T.astype(BF16))


def _expert_ffn_kernel(idx_ref, gate_ref, h_ref, wg_ref, wu_ref, wd_ref, o_ref, stage_ref, xs_ref, sem, *, cap):
    f = pl.program_id(2)

    @pl.when(f == 0)
    def _():
        def issue(s, carry):
            pltpu.make_async_copy(h_ref.at[pl.ds(idx_ref[0, s], 1), :], stage_ref.at[pl.ds(s, 1), :], sem).start()
            return carry

        lax.fori_loop(0, cap, issue, 0, unroll=8)
        pltpu.make_async_copy(h_ref.at[pl.ds(0, cap), :], stage_ref, sem).wait()
        xs_ref[...] = stage_ref[...].astype(BF16)

        o_ref[...] = jnp.zeros_like(o_ref)

    wg = wg_ref[...].astype(BF16)
    wu = wu_ref[...].astype(BF16)
    wd = wd_ref[...].astype(BF16)
    mc = min(MOE_MC, cap)
    for r0 in range(0, cap, mc):
        x = xs_ref[r0:r0 + mc, :]
        gt = jnp.dot(x, wg, preferred_element_type=F32)
        up = jnp.dot(x, wu, preferred_element_type=F32)
        hid = (jax.nn.silu(gt) * up).astype(BF16)
        o_ref[r0:r0 + mc, :] += jnp.dot(hid, wd, preferred_element_type=F32)

    @pl.when(f == pl.num_programs(2) - 1)
    def _():
        o_ref[...] = o_ref[...] * gate_ref[...]


def _expert_ffn(h, rows, gate, w_gate, w_up, w_down, layer):
    R, D = h.shape
    _, E, _, FF = w_gate.shape
    B, _, cap = rows.shape
    ft = min(MOE_FT, FF)
    assert FF % ft == 0 and cap <= R
    return pl.pallas_call(
        functools.partial(_expert_ffn_kernel, cap=cap),
        grid=(E, B, FF // ft),
        in_specs=[pl.BlockSpec((None, 1, cap), lambda e, b, f: (b * E + e, 0, 0), memory_space=pltpu.SMEM),
                  pl.BlockSpec((None, None, cap, 1), lambda e, b, f: (b, e, 0, 0)),
                  pl.BlockSpec(memory_space=pl.ANY),
                  pl.BlockSpec((None, None, D, ft), lambda e, b, f: (layer, e, 0, f)),
                  pl.BlockSpec((None, None, D, ft), lambda e, b, f: (layer, e, 0, f)),
                  pl.BlockSpec((None, None, ft, D), lambda e, b, f: (layer, e, f, 0))],
        out_specs=pl.BlockSpec((None, None, cap, D), lambda e, b, f: (b, e, 0, 0)),
        out_shape=jax.ShapeDtypeStruct((B, E, cap, D), F32),
        scratch_shapes=[pltpu.VMEM((cap, D), F32), pltpu.VMEM((cap, D), BF16), pltpu.SemaphoreType.DMA],
        compiler_params=pltpu.CompilerParams(dimension_semantics=("arbitrary", "arbitrary", "arbitrary"),
                                             vmem_limit_bytes=VMEM_LIMIT),
    )(rows.reshape(B * E, 1, cap), gate.reshape(B, E, cap, 1), h, w_gate, w_up, w_down)


def _ec_moe(h, aff, row0, n, w_gate, w_up, w_down, layer):
    B, T, D = h.shape
    cap = EC_CAPACITY * n // N_EXPERTS
    gate, idx = lax.top_k(aff[:, :, row0:row0 + n], cap)
    rows = idx.astype(jnp.int32) + (jnp.arange(B, dtype=jnp.int32) * T + row0)[:, None, None]
    y = _expert_ffn(h.reshape(B * T, D), rows, gate, w_gate, w_up, w_down, layer)
    return jax.vmap(lambda ib, yb: jnp.zeros((n, D), yb.dtype).at[ib.reshape(-1)].add(yb.reshape(-1, D)))(idx, y)


def kernel(x, c, ctx, c_ctx, mod_w, mod_b, norm_g, ev_w_in, ev_w_out, hg_lb, hg_norm_g,
           s5_a_re, s5_a_im, s5_log_dt, s5_b_re, s5_b_im, s5_c_re, s5_c_im, s5_d, s5_glu_w, s5_glu_b,
           od_w_in, od_conv_w, od_a_log, od_dt_bias, od_norm_g, od_w_out,
           router_w, moe_w_gate, moe_w_up, moe_w_down):
    B, L, D = ctx.shape
    lb_all = jnp.cumsum(jax.nn.softmax(hg_lb.astype(F32), axis=0), axis=0)
    sc = jax.nn.silu(c)
    scc = jax.nn.silu(c_ctx)
    xs = jnp.concatenate([ctx, x], axis=1)
    for layer in range(DEPTH):
        last = layer == DEPTH - 1
        j = layer // 2
        m_lat = (sc @ mod_w[layer] + mod_b[layer]).reshape(B, N_MOD, D)
        m_ctx = (scc @ mod_w[layer] + mod_b[layer]).reshape(N_MOD, D)
        mods = jnp.stack([jnp.broadcast_to(m_ctx, (B, N_MOD, D)), m_lat], axis=1)
        g_pre_mix, g_post_mix, g_pre_ffn, g_post_ffn = norm_g[layer]
        if layer % 2 == 0:
            p = _in_proj(xs, g_pre_mix, mods, ev_w_in[j].astype(BF16), 0, L)
            o_hg = _hgrn2_mixer(p, L, lb_all[j], hg_norm_g[j])
            s5w = _s5_weights(s5_a_re[j], s5_a_im[j], s5_log_dt[j], s5_b_re[j], s5_b_im[j],
                              s5_c_re[j], s5_c_im[j], s5_d[j], s5_glu_w[j], s5_glu_b[j])
            o_s5 = _s5_mixer(p, 5 * HG_WIDTH, L, s5w)
            xs = _out_proj_even(o_hg, o_s5, ev_w_out[j].astype(BF16), xs, g_post_mix, mods, 2, L)
        else:
            w_in = od_w_in[j].astype(BF16)
            p = _in_proj(xs, g_pre_mix, mods, w_in, 0, L, ncols=DN_QKV + DN_V)
            pg = _in_proj(xs, g_pre_mix, mods, w_in, 0, L, col0=DN_QKV + DN_V, tn=LANES)
            qkv, gates = _dn_inputs(p, pg, L, od_conv_w[j], od_a_log[j], od_dt_bias[j])
            o_f, o_b = _delta_rule(qkv, gates, L)
            xs = _out_proj_odd(o_f, o_b, p, DN_QKV, od_norm_g[j], od_w_out[j].astype(BF16), xs, g_post_mix,
                               mods, 2, L)
        h, aff = _router(xs, g_pre_ffn, mods, router_w[layer], 3, L)
        moe = functools.partial(_ec_moe, h, aff, w_gate=moe_w_gate, w_up=moe_w_up, w_down=moe_w_down, layer=layer)
        y_ctx = None if last else moe(0, L)
        xs = _post_residual(xs, y_ctx, moe(L, xs.shape[1] - L), g_post_ffn, mods, 5, L, latent_only=last)
    return xs
```

```python
import functools
import math

import jax
import jax.numpy as jnp
from jax import lax
from jax.experimental import pallas as pl
from jax.experimental.pallas import tpu as pltpu

D_MODEL = 2048
DEPTH = 2
GRID_W = 64
N_MOD = 6
EPS = 1e-6

HG_WIDTH = D_MODEL // 2
HG_HEAD_DIM = 128
HG_HEADS = HG_WIDTH // HG_HEAD_DIM
HG_CHUNK = 64
S5_WIDTH = D_MODEL - HG_WIDTH
S5_GROUP = 16
S5_GROUPS = S5_WIDTH // S5_GROUP
S5_STATE = 64

DN_QK_HEADS = 16
DN_V_HEADS = 32
DN_HEAD_DIM = 128
DN_QK = DN_QK_HEADS * DN_HEAD_DIM
DN_V = DN_V_HEADS * DN_HEAD_DIM
DN_QKV = 2 * DN_QK + DN_V
DN_CHUNK = 64

N_EXPERTS = 16
EC_CAPACITY = 2

LANES = 128
VMEM_LIMIT = 56 * 1024 * 1024

S5_LC = 16
S5_TILES = S5_WIDTH // LANES
S5_GPT = LANES // S5_GROUP
S5_ST = S5_GPT * S5_STATE
S5_CAT = S5_LC * LANES
S5_NSPLIT = 2

F32 = jnp.float32
BF16 = jnp.bfloat16


def _s5_weights(a_re, a_im, log_dt, b_re, b_im, c_re, c_im, d_skip, glu_w, glu_b):
    G, P, C, LC, NT, GT = S5_GROUPS, S5_STATE, S5_GROUP, S5_LC, S5_TILES, S5_GPT
    ar, ai = a_re.astype(F32), a_im.astype(F32)
    dt = jnp.exp(log_dt.astype(F32))[..., None]
    steps = jnp.arange(LC, dtype=F32)
    cmul = lambda x, y: (x[0] * y[0] - x[1] * y[1], x[0] * y[1] + x[1] * y[0])

    def apow(e):
        ex = e[:, None, None, None]
        mag = jnp.exp(ex * (ar * dt)[None])
        ang = ex * (ai * dt)[None]
        return mag * jnp.cos(ang), mag * jnp.sin(ang)

    one = apow(jnp.ones((1,), F32))
    nr, ni = one[0][0] - 1.0, one[1][0]
    den = ar * ar + ai * ai
    zoh = ((nr * ar + ni * ai) / den, (ni * ar - nr * ai) / den)
    bb = cmul((zoh[0][..., None], zoh[1][..., None]), (b_re.astype(F32), b_im.astype(F32)))
    cm = (c_re.astype(F32), c_im.astype(F32))
    hi = lax.Precision.HIGHEST
    at = lambda z, idx: (z[0][idx], z[1][idx])
    lane_grp = jnp.arange(LANES) // C

    def spread(a, width):
        keep = lane_grp[:, None] == (jnp.arange(GT * width) // width)[None, :]
        return jnp.where(keep, jnp.concatenate([a] * GT, axis=-1), 0.0)

    cp = cmul((cm[0][None], cm[1][None]), at(apow(steps), (slice(None), slice(None), slice(None), None)))
    kk = (jnp.einsum('tdgcp,dgpk->dtgck', cp[0], bb[0], precision=hi)
          - jnp.einsum('tdgcp,dgpk->dtgck', cp[1], bb[1], precision=hi))
    bd = spread(jnp.swapaxes(kk, -1, -2).reshape(2, LC, NT, LANES, C), C).astype(BF16)
    s_in = jnp.arange(LC)[:, None, None]
    s_out = jnp.arange(LC)[None, :, None]
    lag = jnp.arange(LC)[None, None, :]
    sel = jnp.concatenate([s_out - s_in == lag, s_in - s_out == lag], axis=-1).astype(BF16)
    toe = jnp.einsum('abt,tjrc->jarbc', sel, bd.reshape(2 * LC, NT, LANES, LANES), preferred_element_type=F32)
    toe = toe.astype(BF16).reshape(NT, S5_CAT, S5_CAT)

    def state_cols(zf, zb):
        return jnp.concatenate([spread(zf[0], P), spread(zf[1], P), spread(zb[0], P), spread(zb[1], P)],
                               axis=-1).astype(BF16)

    def pw_dir(e, d):
        z = apow(e)
        return z[0][:, d][..., None], z[1][:, d][..., None]

    def inj_rows(z):
        return jnp.moveaxis(z.reshape(LC, NT, GT, P, C), -1, -2).reshape(LC, NT, LANES, P)

    ef = cmul(pw_dir(LC - 1 - steps, 0), (bb[0][0][None], bb[1][0][None]))
    eb = cmul(pw_dir(steps, 1), (bb[0][1][None], bb[1][1][None]))
    inj = state_cols((inj_rows(ef[0]), inj_rows(ef[1])), (inj_rows(eb[0]), inj_rows(eb[1])))
    inj = jnp.swapaxes(inj, 0, 1).reshape(NT, S5_CAT, 4 * S5_ST)

    def pw_row(e, d):
        z = apow(e)
        return z[0][:, d][:, :, None, :], z[1][:, d][:, :, None, :]

    def rd_rows(z):
        return z.reshape(LC, NT, LANES, P)

    wf = cmul((cm[0][0][None], cm[1][0][None]), pw_row(steps + 1.0, 0))
    wb = cmul((cm[0][1][None], cm[1][1][None]), pw_row(LC - steps, 1))
    rd = state_cols((rd_rows(wf[0]), rd_rows(-wf[1])), (rd_rows(wb[0]), rd_rows(-wb[1])))
    w2 = jnp.transpose(rd, (1, 3, 0, 2)).reshape(NT, 4 * S5_ST, S5_CAT)

    a_lc = apow(jnp.full((1,), LC, F32))
    a16 = jnp.stack([a_lc[0][0, 0], a_lc[1][0, 0], a_lc[0][0, 1], a_lc[1][0, 1]], 0)
    a16 = a16.reshape(4, NT, S5_ST).transpose(1, 0, 2)
    gw = spread(glu_w.astype(F32).reshape(NT, LANES, C), C).astype(BF16)
    dk = d_skip.astype(F32).reshape(NT, 1, LANES)
    gb = glu_b.astype(F32).reshape(NT, 1, LANES)
    return toe, inj, w2, a16, dk, gw, gb


def _s5_inject_kernel(u_ref, toe_ref, inj_ref, r_ref, ucat_ref, *, nb, nch, ntoe):
    n = pl.program_id(1)

    @pl.when(n == 0)
    def _():
        for b in range(nb):
            for s in range(S5_LC):
                ucat_ref[b * nch:(b + 1) * nch, s * LANES:(s + 1) * LANES] = (
                    u_ref[b, pl.ds(s, nch, stride=S5_LC), :].astype(BF16))

    def emit(w_ref):
        r = jnp.dot(ucat_ref[...], w_ref[...], preferred_element_type=F32)
        for b in range(nb):
            r_ref[b] = r[b * nch:(b + 1) * nch]

    @pl.when(n < ntoe)
    def _():
        emit(toe_ref)

    @pl.when(n >= ntoe)
    def _():
        emit(inj_ref)


def _s5_scan_kernel(s_ref, a_ref, x_ref, *, nch, nctx):
    st = S5_ST
    arf, aif, arb, aib = a_ref[0:1, :], a_ref[1:2, :], a_ref[2:3, :], a_ref[3:4, :]

    def body(i, carry):
        fr, fi, br, bi = carry
        kf = i
        kb = jnp.where(i < nctx, nctx - 1 - i, nch + nctx - 1 - i)
        x_ref[pl.ds(kf, 1), 0:st] = fr
        x_ref[pl.ds(kf, 1), st:2 * st] = fi
        x_ref[pl.ds(kb, 1), 2 * st:3 * st] = br
        x_ref[pl.ds(kb, 1), 3 * st:4 * st] = bi
        sfr = s_ref[pl.ds(kf, 1), 0:st]
        sfi = s_ref[pl.ds(kf, 1), st:2 * st]
        sbr = s_ref[pl.ds(kb, 1), 2 * st:3 * st]
        sbi = s_ref[pl.ds(kb, 1), 3 * st:4 * st]
        return (arf * fr - aif * fi + sfr, arf * fi + aif * fr + sfi,
                arb * br - aib * bi + sbr, arb * bi + aib * br + sbi)

    z = jnp.zeros((1, st), F32)
    lax.fori_loop(0, nch, body, (z, z, z, z))


def _s5_readout_kernel(yi_ref, x_ref, w_ref, u_ref, d_ref, gw_ref, gb_ref, o_ref, *, nch):
    half = pl.program_id(2)
    y = yi_ref[...] + jnp.dot(x_ref[...].astype(BF16), w_ref[...], preferred_element_type=F32)
    per = S5_LC // S5_NSPLIT
    for sl in range(per):
        s = half * per + sl
        ys = y[:, sl * LANES:(sl + 1) * LANES] + d_ref[...] * u_ref[pl.ds(s, nch, stride=S5_LC), :]
        ys = jax.nn.gelu(ys)
        z = jnp.dot(ys.astype(BF16), gw_ref[...], preferred_element_type=F32) + gb_ref[...]
        o_ref[pl.ds(s, nch, stride=S5_LC), :] = ys * jax.nn.sigmoid(z)


def _s5_mixer(p, u_col0, L, weights):
    toe, inj, w2, a16, dk, gw, gb = weights
    B, T, _ = p.shape
    nch = T // S5_LC
    nctx = L // S5_LC
    assert T % S5_LC == 0 and L % S5_LC == 0 and u_col0 % LANES == 0
    ucol = u_col0 // LANES
    nblk1 = 1024
    ntoe = toe.shape[-1] // nblk1
    n1 = toe.shape[-1] + inj.shape[-1]
    cparams = functools.partial(pltpu.CompilerParams, vmem_limit_bytes=VMEM_LIMIT)

    r = pl.pallas_call(
        functools.partial(_s5_inject_kernel, nb=B, nch=nch, ntoe=ntoe),
        grid=(S5_TILES, n1 // nblk1),
        in_specs=[pl.BlockSpec((B, T, LANES), lambda j, n: (0, 0, ucol + j)),
                  pl.BlockSpec((None, S5_CAT, nblk1), lambda j, n: (j, 0, jnp.minimum(n, ntoe - 1))),
                  pl.BlockSpec((None, S5_CAT, nblk1), lambda j, n: (j, 0, jnp.maximum(n - ntoe, 0)))],
        out_specs=pl.BlockSpec((B, None, nch, nblk1), lambda j, n: (0, j, 0, n)),
        out_shape=jax.ShapeDtypeStruct((B, S5_TILES, nch, n1), F32),
        scratch_shapes=[pltpu.VMEM((B * nch, S5_CAT), BF16)],
        compiler_params=cparams(dimension_semantics=("arbitrary", "arbitrary")),
    )(p, toe, inj)

    sblk = S5_CAT // (4 * S5_ST)
    assert S5_CAT % (4 * S5_ST) == 0
    xin = pl.pallas_call(
        functools.partial(_s5_scan_kernel, nch=nch, nctx=nctx),
        grid=(B, S5_TILES),
        in_specs=[pl.BlockSpec((None, None, nch, 4 * S5_ST), lambda b, j: (b, j, 0, sblk)),
                  pl.BlockSpec((None, 4, S5_ST), lambda b, j: (j, 0, 0))],
        out_specs=pl.BlockSpec((None, None, nch, 4 * S5_ST), lambda b, j: (b, j, 0, 0)),
        out_shape=jax.ShapeDtypeStruct((B, S5_TILES, nch, 4 * S5_ST), F32),
        compiler_params=cparams(dimension_semantics=("arbitrary", "arbitrary")),
    )(r, a16)

    ncol = S5_CAT // S5_NSPLIT
    return pl.pallas_call(
        functools.partial(_s5_readout_kernel, nch=nch),
        grid=(S5_TILES, B, S5_NSPLIT),
        in_specs=[pl.BlockSpec((None, None, nch, ncol), lambda j, b, h: (b, j, 0, h)),
                  pl.BlockSpec((None, None, nch, 4 * S5_ST), lambda j, b, h: (b, j, 0, 0)),
                  pl.BlockSpec((None, 4 * S5_ST, ncol), lambda j, b, h: (j, 0, h)),
                  pl.BlockSpec((None, T, LANES), lambda j, b, h: (b, 0, ucol + j)),
                  pl.BlockSpec((None, 1, LANES), lambda j, b, h: (j, 0, 0)),
                  pl.BlockSpec((None, LANES, LANES), lambda j, b, h: (j, 0, 0)),
                  pl.BlockSpec((None, 1, LANES), lambda j, b, h: (j, 0, 0))],
        out_specs=pl.BlockSpec((None, T, LANES), lambda j, b, h: (b, 0, j)),
        out_shape=jax.ShapeDtypeStruct((B, T, S5_WIDTH), F32),
        compiler_params=cparams(dimension_semantics=("arbitrary", "arbitrary", "arbitrary")),
    )(r, xin, w2, p, dk, gw, gb)


DN_SC = 256
DN_NC = DN_SC // DN_CHUNK
DN_REP = DN_V_HEADS // DN_QK_HEADS
DN_LEVELS = DN_CHUNK.bit_length() - 1
DN_SCAN_HG = 8
DN_PREP_HG = 2


def _dn_masks():
    out = []
    for reverse in (False, True):
        i = jnp.arange(DN_SC)[:, None]
        j = jnp.arange(DN_SC)[None, :]
        same = (i // DN_CHUNK) == (j // DN_CHUNK)
        if reverse:
            i, j = j, i
        ms = [same & (i >= j), same & (i > j)]
        for lv in range(DN_LEVELS):
            ms.append(same & ((i >> (lv + 1)) == (j >> (lv + 1))) & (((i >> lv) & 1) == 1) & (((j >> lv) & 1) == 0))
        out.append(jnp.stack(ms))
    return jnp.stack(out).astype(F32)


def _lane_pick(x, col):
    hot = (lax.broadcasted_iota(jnp.int32, (1, LANES), 1) == col).astype(F32)
    return jnp.sum(x * hot, axis=1, keepdims=True)


def _dn_prep_kernel(q_ref, k_ref, v_ref, g_ref, m_ref, uwf_ref, uwb_ref, qkf_ref, qkb_ref, gc_ref,
                    gcs_ref, gct_ref):
    C, D = DN_CHUNK, DN_HEAD_DIM
    hgrp = pl.program_id(2)
    gb = g_ref[...]

    @pl.when(hgrp == 0)
    def _():
        lane = lax.broadcasted_iota(jnp.int32, (1, LANES), 1)
        out = gb
        for d in range(2):
            cs = jnp.dot(m_ref[d, 0], gb, preferred_element_type=F32, precision=lax.Precision.HIGHEST)
            gcs_ref[d] = cs
            gct_ref[d] = cs.T
            lo = 2 * DN_V_HEADS + d * DN_V_HEADS
            out = jnp.where((lane >= lo) & (lane < lo + DN_V_HEADS), cs, out)
        gc_ref[...] = out

    kk, qk, kf = {}, {}, {}
    for hq in range(DN_PREP_HG):
        q = q_ref[:, hq * D:(hq + 1) * D]
        k = k_ref[:, hq * D:(hq + 1) * D]
        kk[hq] = lax.dot_general(k, k, (((1,), (1,)), ((), ())), preferred_element_type=F32)
        qk[hq] = lax.dot_general(q, k, (((1,), (1,)), ((), ())), preferred_element_type=F32)
        kf[hq] = k.astype(F32)
    uw_refs = (uwf_ref, uwb_ref)
    qk_refs = (qkf_ref, qkb_ref)
    insts = [(d, r) for d in range(2) for r in range(DN_PREP_HG * DN_REP)]
    bcol, gcol, dec, a_mat, m = {}, {}, {}, {}, {}
    for i in insts:
        d, r = i
        hv = hgrp * (DN_PREP_HG * DN_REP) + r
        incl = m_ref[d, 0]
        strict = m_ref[d, 1]
        bcol[i] = _lane_pick(gb, d * DN_V_HEADS + hv)
        gcol[i] = _lane_pick(gcs_ref[d], 2 * DN_V_HEADS + d * DN_V_HEADS + hv)
        grow = gct_ref[d, pl.ds(2 * DN_V_HEADS + d * DN_V_HEADS + hv, 1), :]
        dec[i] = incl * jnp.exp(jnp.where(incl > 0, gcol[i] - grow, 0.0))
        a_mat[i] = strict * (bcol[i] * (kk[r // DN_REP] * dec[i]))
        m[i] = (incl - strict) - a_mat[i] * m_ref[d, 2]
    for lv in range(1, DN_LEVELS):
        mb = {i: m[i].astype(BF16) for i in insts}
        x = {i: jnp.dot(mb[i], (a_mat[i] * m_ref[i[0], 2 + lv]).astype(BF16), preferred_element_type=F32)
             for i in insts}
        for i in insts:
            m[i] = m[i] - jnp.dot(x[i].astype(BF16), mb[i], preferred_element_type=F32)
    for i in insts:
        d, r = i
        egc = jnp.exp(gcol[i])
        v = v_ref[:, r * D:(r + 1) * D].astype(F32)
        rhs = jnp.concatenate([(v * bcol[i]).astype(BF16), (kf[r // DN_REP] * (bcol[i] * egc)).astype(BF16)], axis=1)
        uw = jnp.dot(m[i].astype(BF16), rhs, preferred_element_type=F32)
        uw_refs[d][:, r * 2 * D:(r + 1) * 2 * D] = uw.astype(BF16)
    for d in range(2):
        parts = []
        for r in range(DN_PREP_HG * DN_REP):
            qd = qk[r // DN_REP] * dec[(d, r)]
            parts.append(jnp.concatenate([qd[c * C:(c + 1) * C, c * C:(c + 1) * C] for c in range(DN_NC)],
                                         axis=0))
        qk_refs[d][...] = jnp.concatenate(parts, axis=1).astype(BF16)


def _dn_scan_kernel(uw_ref, qkd_ref, q_ref, k_ref, gc_ref, o_ref, s_ref, *, reverse):
    C, D = DN_CHUNK, DN_HEAD_DIM
    dirn = 1 if reverse else 0
    hgrp = pl.program_id(1)
    last = 0 if reverse else C - 1

    @pl.when(pl.program_id(2) == 0)
    def _():
        s_ref[...] = jnp.zeros_like(s_ref)

    heads = range(DN_SCAN_HG * DN_REP)
    for ci in range(DN_NC):
        c = (DN_NC - 1 - ci) if reverse else ci
        r0 = c * C
        gcb = gc_ref[r0:r0 + C, :]
        gcol, g_end, s_old, ws = {}, {}, {}, {}
        for hl in heads:
            hq = hl // DN_REP
            hv = hgrp * (DN_SCAN_HG * DN_REP) + hl
            gcol[hl] = _lane_pick(gcb, 2 * DN_V_HEADS + dirn * DN_V_HEADS + hv)
            g_end[hl] = gcol[hl][last:last + 1, :]
            qf = q_ref[r0:r0 + C, hq * D:(hq + 1) * D].astype(F32)
            w = uw_ref[r0:r0 + C, hl * 2 * D + D:(hl + 1) * 2 * D]
            s_old[hl] = s_ref[hl]
            wq = jnp.concatenate([w, (qf * jnp.exp(gcol[hl])).astype(BF16)], axis=0)
            ws[hl] = jnp.dot(wq, s_old[hl].astype(BF16), preferred_element_type=F32)
        for hl in heads:
            hq = hl // DN_REP
            u = uw_ref[r0:r0 + C, hl * 2 * D:hl * 2 * D + D].astype(F32)
            vnb = (u - ws[hl][:C]).astype(BF16)
            qkd = qkd_ref[r0:r0 + C, hl * C:(hl + 1) * C]
            o_ref[r0:r0 + C, hl * D:(hl + 1) * D] = ws[hl][C:] + jnp.dot(qkd, vnb, preferred_element_type=F32)
            kf = k_ref[r0:r0 + C, hq * D:(hq + 1) * D].astype(F32)
            k_dec_t = (kf * jnp.exp(g_end[hl] - gcol[hl])).T.astype(BF16)
            s_ref[hl] = s_old[hl] * jnp.exp(g_end[hl]) + jnp.dot(k_dec_t, vnb, preferred_element_type=F32)


def _delta_rule(qkv, gates, L):
    B, T, _ = qkv.shape
    nsc = T // DN_SC
    assert T % DN_SC == 0 and L == DN_SC and DN_QK_HEADS % DN_SCAN_HG == 0 and DN_QK_HEADS % DN_PREP_HG == 0
    D, C = DN_HEAD_DIM, DN_CHUNK
    masks = _dn_masks()
    cp = functools.partial(pltpu.CompilerParams, vmem_limit_bytes=VMEM_LIMIT)
    P = DN_PREP_HG
    blk = lambda w, off=0: pl.BlockSpec((None, DN_SC, w), lambda b, s, h: (b, s, off // w + h))
    uwf, uwb, qkf, qkb, gc = pl.pallas_call(
        _dn_prep_kernel,
        grid=(B, nsc, DN_QK_HEADS // P),
        in_specs=[blk(P * D), blk(P * D, DN_QK), blk(P * DN_REP * D, 2 * DN_QK),
                  pl.BlockSpec((None, DN_SC, LANES), lambda b, s, h: (b, s, 0)),
                  pl.BlockSpec(masks.shape, lambda b, s, h: (0, 0, 0, 0))],
        out_specs=[blk(P * DN_REP * 2 * D), blk(P * DN_REP * 2 * D), blk(P * DN_REP * C), blk(P * DN_REP * C),
                   pl.BlockSpec((None, DN_SC, LANES), lambda b, s, h: (b, s, 0))],
        out_shape=[jax.ShapeDtypeStruct((B, T, DN_V_HEADS * 2 * D), BF16)] * 2
        + [jax.ShapeDtypeStruct((B, T, DN_V_HEADS * C), BF16)] * 2
        + [jax.ShapeDtypeStruct((B, T, LANES), F32)],
        scratch_shapes=[pltpu.VMEM((2, DN_SC, LANES), F32), pltpu.VMEM((2, LANES, DN_SC), F32)],
        compiler_params=cp(dimension_semantics=("arbitrary", "arbitrary", "arbitrary")),
    )(qkv, qkv, qkv, gates, masks)

    outs = []
    G = DN_SCAN_HG
    for reverse, uw, qkd in ((False, uwf, qkf), (True, uwb, qkb)):
        if reverse:
            sc_of = lambda i: jnp.where(i == 0, 0, nsc - i)
        else:
            sc_of = lambda i: i
        sblk = lambda w, off=0, f=sc_of: pl.BlockSpec((None, DN_SC, w), lambda b, h, i: (b, f(i), off // w + h))
        outs.append(pl.pallas_call(
            functools.partial(_dn_scan_kernel, reverse=reverse),
            grid=(B, DN_QK_HEADS // G, nsc),
            in_specs=[sblk(G * DN_REP * 2 * D), sblk(G * DN_REP * C), sblk(G * D), sblk(G * D, DN_QK),
                      pl.BlockSpec((None, DN_SC, LANES), lambda b, h, i, f=sc_of: (b, f(i), 0))],
            out_specs=sblk(G * DN_REP * D),
            out_shape=jax.ShapeDtypeStruct((B, T, DN_V), F32),
            scratch_shapes=[pltpu.VMEM((G * DN_REP, D, D), F32)],
            compiler_params=cp(dimension_semantics=("arbitrary", "arbitrary", "arbitrary")),
        )(uw, qkd, qkv, qkv, gc))
    return outs


HG_SC = DN_SC
HG_NC = HG_SC // HG_CHUNK
HG_GRP = 8
assert HG_CHUNK == DN_CHUNK


def _hg_scan_kernel(*refs, reverse, final):
    if final:
        q_ref, f_ref, i_ref, lb_ref, tri_ref, of_ref, g_ref, ng_ref, o_ref, st_ref = refs
    else:
        q_ref, f_ref, i_ref, lb_ref, tri_ref, o_ref, st_ref = refs
    C, D = HG_CHUNK, HG_HEAD_DIM
    mid = (C - 1 - (C // 2 - 1)) if reverse else (C // 2 - 1)
    last = 0 if reverse else C - 1

    @pl.when(pl.program_id(2) == 0)
    def _():
        st_ref[...] = jnp.zeros_like(st_ref)

    tri = tri_ref[...]
    tri_c = tri[0:C, 0:C]
    heads = range(HG_GRP)
    qs, ks, bs = {}, {}, {}
    for h in heads:
        cols = slice(h * D, (h + 1) * D)
        lb = lb_ref[:, cols]
        f = lb + (1.0 - lb) * jax.nn.sigmoid(f_ref[:, cols])
        ks[h] = 1.0 - f
        qs[h] = q_ref[:, cols]
        bs[h] = jnp.dot(tri, jnp.log(f), preferred_element_type=F32, precision=lax.Precision.HIGHEST)
    for ci in range(HG_NC):
        c = (HG_NC - 1 - ci) if reverse else ci
        r0 = c * C
        rows = slice(r0, r0 + C)
        att, qd, kd, vb, dl = {}, {}, {}, {}, {}
        for h in heads:
            b = bs[h][rows]
            ref = b[mid:mid + 1, :]
            b_last = b[last:last + 1, :]
            q = qs[h][rows]
            k = ks[h][rows]
            qa = (q * jnp.exp(b - ref)).astype(BF16)
            ka = (k * jnp.exp(ref - b)).astype(BF16)
            att[h] = lax.dot_general(qa, ka, (((1,), (1,)), ((), ())), preferred_element_type=F32) * tri_c
            qd[h] = (q * jnp.exp(b)).astype(BF16)
            kd[h] = (k * jnp.exp(b_last - b)).astype(BF16)
            dl[h] = jnp.exp(b_last)
            vb[h] = i_ref[rows, h * D:(h + 1) * D]
        for h in heads:
            cols = slice(h * D, (h + 1) * D)
            st = st_ref[h]
            v16 = vb[h].astype(BF16)
            o = (jnp.dot(att[h].astype(BF16), v16, preferred_element_type=F32)
                 + lax.dot_general(qd[h], st.astype(BF16), (((1,), (1,)), ((), ())), preferred_element_type=F32))
            st_ref[h] = st * dl[h] + jnp.dot(vb[h].T.astype(BF16), kd[h], preferred_element_type=F32)
            if final:
                o = o + of_ref[rows, cols]
                y = o * lax.rsqrt(jnp.mean(o * o, axis=-1, keepdims=True) + EPS) * ng_ref[...]
                o_ref[rows, cols] = (y * jax.nn.silu(g_ref[rows, cols])).astype(o_ref.dtype)
            else:
                o_ref[rows, cols] = o


def _hgrn2_mixer(p, L, lb, norm_g):
    B, T, _ = p.shape
    nsc = T // HG_SC
    assert T % HG_SC == 0 and L == HG_SC and HG_HEADS % HG_GRP == 0
    gw = HG_GRP * HG_HEAD_DIM
    per = HG_WIDTH // gw
    tri = _dn_masks()[:, 0]
    ng = norm_g.astype(F32).reshape(1, HG_HEAD_DIM)
    lbf = lb.astype(F32)
    cp = pltpu.CompilerParams(dimension_semantics=("arbitrary", "arbitrary", "arbitrary"),
                              vmem_limit_bytes=VMEM_LIMIT)
    o_prev = None
    for reverse in (False, True):
        d = 1 if reverse else 0
        if reverse:
            sc_of = lambda i: jnp.where(i == 0, 0, nsc - i)
        else:
            sc_of = lambda i: i
        col = lambda sec, f=sc_of: pl.BlockSpec((None, HG_SC, gw), lambda b, h, i: (b, f(i), sec * per + h))
        in_specs = [col(0), col(1 + d), col(3),
                    pl.BlockSpec((None, 1, gw), lambda b, h, i: (d, 0, h)),
                    pl.BlockSpec((None, HG_SC, HG_SC), lambda b, h, i: (d, 0, 0))]
        args = [p, p, p, lbf.reshape(2, 1, HG_WIDTH), tri]
        final = reverse
        if final:
            in_specs += [pl.BlockSpec((None, HG_SC, gw), lambda b, h, i, f=sc_of: (b, f(i), h)), col(4),
                         pl.BlockSpec((1, HG_HEAD_DIM), lambda b, h, i: (0, 0))]
            args += [o_prev, p, ng]
        o_prev = pl.pallas_call(
            functools.partial(_hg_scan_kernel, reverse=reverse, final=final),
            grid=(B, HG_HEADS // HG_GRP, nsc),
            in_specs=in_specs,
            out_specs=pl.BlockSpec((None, HG_SC, gw), lambda b, h, i, f=sc_of: (b, f(i), h)),
            out_shape=jax.ShapeDtypeStruct((B, T, HG_WIDTH), BF16 if final else F32),
            scratch_shapes=[pltpu.VMEM((HG_GRP, HG_HEAD_DIM, HG_HEAD_DIM), F32)],
            compiler_params=cp,
        )(*args)
    return o_prev


CV_CW = 1024
CV_PAD = 8


def _dn_conv_kernel(xm_ref, xp_ref, xn_ref, w_ref, o_ref, xs_ref, *, nsc):
    s = pl.program_id(1)
    cb = pl.program_id(2)
    base = CV_PAD + GRID_W
    ext = DN_SC + 2 * GRID_W
    is_ctx = s == 0
    has_up = s > 1
    has_dn = jnp.logical_and(s > 0, s < nsc - 1)
    xs_ref[1, 0:CV_PAD, :] = jnp.zeros((CV_PAD, CV_CW), F32)
    xs_ref[1, CV_PAD + ext:, :] = jnp.zeros((CV_PAD, CV_CW), F32)
    xs_ref[1, CV_PAD:base, :] = jnp.where(has_up, xp_ref[...], 0.0)
    xs_ref[1, base:base + DN_SC, :] = xm_ref[...]
    xs_ref[1, base + DN_SC:base + DN_SC + GRID_W, :] = jnp.where(has_dn, xn_ref[...], 0.0)
    t = lax.broadcasted_iota(jnp.int32, (ext, 1), 0) - GRID_W
    pos = jnp.where(is_ctx, t, t % GRID_W)
    ok_lf = pos > 0
    ok_rt = pos < jnp.where(is_ctx, jnp.int32(DN_SC - 1), jnp.int32(GRID_W - 1))
    xs_ref[0, CV_PAD:CV_PAD + ext, :] = jnp.where(ok_lf, xs_ref[1, CV_PAD - 1:CV_PAD - 1 + ext, :], 0.0)
    xs_ref[2, CV_PAD:CV_PAD + ext, :] = jnp.where(ok_rt, xs_ref[1, CV_PAD + 1:CV_PAD + 1 + ext, :], 0.0)
    rows_on = jnp.where(is_ctx, 0.0, 1.0)
    acc = jnp.zeros((DN_SC, CV_CW), F32)
    for dr in (-1, 0, 1):
        for dc in (-1, 0, 1):
            k = (dr + 1) * 3 + (dc + 1)
            w = w_ref[k:k + 1, :]
            if dr != 0:
                w = w * rows_on
            start = base + dr * GRID_W
            acc = acc + xs_ref[dc + 1, start:start + DN_SC, :] * w
    y = jax.nn.silu(acc)
    nq = DN_QK // CV_CW
    scale = jnp.where(cb < nq, DN_HEAD_DIM ** -0.5, 1.0)
    is_qk = cb < 2 * nq
    for h in range(CV_CW // LANES):
        yh = y[:, h * LANES:(h + 1) * LANES]
        rs = lax.rsqrt(jnp.sum(yh * yh, axis=-1, keepdims=True) + EPS) * scale
        o_ref[:, h * LANES:(h + 1) * LANES] = (yh * jnp.where(is_qk, rs, 1.0)).astype(o_ref.dtype)


def _dn_gate_kernel(x_ref, na_ref, dtb_ref, o_ref):
    x = x_ref[...]
    lane = lax.broadcasted_iota(jnp.int32, (1, LANES), 1)
    o_ref[...] = jnp.where(lane < 2 * DN_V_HEADS, jax.nn.sigmoid(x),
                           na_ref[...] * jax.nn.softplus(x + dtb_ref[...]))


def _dn_inputs(p, pg, L, conv_w, a_log, dt_bias):
    B, T, _ = p.shape
    nsc = T // DN_SC
    gpb = DN_SC // GRID_W
    nrow = T // GRID_W
    assert L == DN_SC and DN_QK % CV_CW == 0 and DN_QKV % CV_CW == 0 and DN_QKV % LANES == 0
    w9 = conv_w.astype(F32).reshape(9, DN_QKV)
    cp = functools.partial(pltpu.CompilerParams, vmem_limit_bytes=VMEM_LIMIT)
    qkv = pl.pallas_call(
        functools.partial(_dn_conv_kernel, nsc=nsc),
        grid=(B, nsc, DN_QKV // CV_CW),
        in_specs=[pl.BlockSpec((None, DN_SC, CV_CW), lambda b, s, c: (b, s, c)),
                  pl.BlockSpec((None, GRID_W, CV_CW), lambda b, s, c: (b, jnp.maximum(s * gpb - 1, 0), c)),
                  pl.BlockSpec((None, GRID_W, CV_CW), lambda b, s, c: (b, jnp.minimum(s * gpb + gpb, nrow - 1), c)),
                  pl.BlockSpec((9, CV_CW), lambda b, s, c: (0, c))],
        out_specs=pl.BlockSpec((None, DN_SC, CV_CW), lambda b, s, c: (b, s, c)),
        out_shape=jax.ShapeDtypeStruct((B, T, DN_QKV), BF16),
        scratch_shapes=[pltpu.VMEM((3, 2 * CV_PAD + 2 * GRID_W + DN_SC, CV_CW), F32)],
        compiler_params=cp(dimension_semantics=("arbitrary", "arbitrary", "arbitrary")),
    )(p, p, p, w9)

    zeros = jnp.zeros((2 * DN_V_HEADS,), F32)
    na = jnp.concatenate([zeros, -jnp.exp(a_log.astype(F32)).reshape(-1)]).reshape(1, LANES)
    dtb = jnp.concatenate([zeros, dt_bias.astype(F32).reshape(-1)]).reshape(1, LANES)
    gates = pl.pallas_call(
        _dn_gate_kernel,
        grid=(B, nsc),
        in_specs=[pl.BlockSpec((None, DN_SC, LANES), lambda b, s: (b, s, 0)),
                  pl.BlockSpec((1, LANES), lambda b, s: (0, 0)),
                  pl.BlockSpec((1, LANES), lambda b, s: (0, 0))],
        out_specs=pl.BlockSpec((None, DN_SC, LANES), lambda b, s: (b, s, 0)),
        out_shape=jax.ShapeDtypeStruct((B, T, LANES), F32),
        compiler_params=cp(dimension_semantics=("arbitrary", "arbitrary")),
    )(pg, na, dtb)
    return qkv, gates


PJ_TM = 768
PJ_TN = 1024
PO_TM = 384
PO_TK = 1024


def _row_mod(mods_ref, slot, row0, nrows, n_ctx):
    is_ctx = (row0 + lax.broadcasted_iota(jnp.int32, (nrows, 1), 0)) < n_ctx
    return jnp.where(is_ctx, mods_ref[0, slot:slot + 1, :], mods_ref[1, slot:slot + 1, :])


def _norm_mod(x, g, mods_ref, shift_slot, row0, n_ctx):
    y = x * lax.rsqrt(jnp.mean(x * x, axis=-1, keepdims=True) + EPS) * g
    n = x.shape[0]
    return y * (1.0 + _row_mod(mods_ref, shift_slot + 1, row0, n, n_ctx)) + _row_mod(mods_ref, shift_slot, row0, n, n_ctx)


def _in_proj_kernel(x_ref, g_ref, mods_ref, w_ref, o_ref, h_ref, *, slot, n_ctx):
    @pl.when(pl.program_id(2) == 0)
    def _():
        row0 = pl.program_id(1) * x_ref.shape[0]
        h_ref[...] = _norm_mod(x_ref[...], g_ref[...], mods_ref, slot, row0, n_ctx).astype(BF16)

    o_ref[...] = jnp.dot(h_ref[...], w_ref[...], preferred_element_type=F32)


def _in_proj(xs, g, mods, w, slot, n_ctx, col0=0, ncols=None, tn=PJ_TN):
    B, T, D = xs.shape
    N = w.shape[1] - col0 if ncols is None else ncols
    assert T % PJ_TM == 0 and N % tn == 0 and col0 % tn == 0
    cb0 = col0 // tn
    return pl.pallas_call(
        functools.partial(_in_proj_kernel, slot=slot, n_ctx=n_ctx),
        grid=(B, T // PJ_TM, N // tn),
        in_specs=[pl.BlockSpec((None, PJ_TM, D), lambda b, i, n: (b, i, 0)),
                  pl.BlockSpec((1, D), lambda b, i, n: (0, 0)),
                  pl.BlockSpec((None, 2, N_MOD, D), lambda b, i, n: (b, 0, 0, 0)),
                  pl.BlockSpec((D, tn), lambda b, i, n: (0, cb0 + n))],
        out_specs=pl.BlockSpec((None, PJ_TM, tn), lambda b, i, n: (b, i, n)),
        out_shape=jax.ShapeDtypeStruct((B, T, N), F32),
        scratch_shapes=[pltpu.VMEM((PJ_TM, D), BF16)],
        compiler_params=pltpu.CompilerParams(dimension_semantics=("arbitrary", "arbitrary", "arbitrary"),
                                             vmem_limit_bytes=VMEM_LIMIT),
    )(xs, g.astype(F32).reshape(1, D), mods, w)


def _residual(x, o, g, mods_ref, slot, row0, n_ctx):
    on = o * lax.rsqrt(jnp.mean(o * o, axis=-1, keepdims=True) + EPS) * g
    return x + _row_mod(mods_ref, slot, row0, x.shape[0], n_ctx) * on


def _out_proj_even_kernel(a_ref, b_ref, w_ref, x_ref, g_ref, mods_ref, o_ref, *, slot, n_ctx):
    lhs = jnp.concatenate([a_ref[...], b_ref[...].astype(BF16)], axis=1)
    o = jnp.dot(lhs, w_ref[...], preferred_element_type=F32)
    row0 = pl.program_id(1) * x_ref.shape[0]
    o_ref[...] = _residual(x_ref[...], o, g_ref[...], mods_ref, slot, row0, n_ctx)


def _out_proj_even(o_hg, o_s5, w, xs, g, mods, slot, n_ctx):
    B, T, D = xs.shape
    assert T % PO_TM == 0
    row = lambda wd: pl.BlockSpec((None, PO_TM, wd), lambda b, i: (b, i, 0))
    return pl.pallas_call(
        functools.partial(_out_proj_even_kernel, slot=slot, n_ctx=n_ctx),
        grid=(B, T // PO_TM),
        in_specs=[row(HG_WIDTH), row(S5_WIDTH), pl.BlockSpec((D, D), lambda b, i: (0, 0)), row(D),
                  pl.BlockSpec((1, D), lambda b, i: (0, 0)),
                  pl.BlockSpec((None, 2, N_MOD, D), lambda b, i: (b, 0, 0, 0))],
        out_specs=row(D),
        out_shape=jax.ShapeDtypeStruct((B, T, D), F32),
        compiler_params=pltpu.CompilerParams(dimension_semantics=("arbitrary", "arbitrary"),
                                             vmem_limit_bytes=VMEM_LIMIT),
    )(o_hg, o_s5, w, xs, g.astype(F32).reshape(1, D), mods)


def _out_proj_odd_kernel(of_ref, ob_ref, z_ref, ng_ref, w_ref, x_ref, g_ref, mods_ref, o_ref, acc_ref, *, slot, n_ctx):
    k = pl.program_id(2)

    @pl.when(k == 0)
    def _():
        acc_ref[...] = jnp.zeros_like(acc_ref)

    parts = []
    for h in range(of_ref.shape[1] // DN_HEAD_DIM):
        cols = slice(h * DN_HEAD_DIM, (h + 1) * DN_HEAD_DIM)
        o = of_ref[:, cols] + ob_ref[:, cols]
        y = o * lax.rsqrt(jnp.mean(o * o, axis=-1, keepdims=True) + EPS) * ng_ref[...]
        parts.append((y * jax.nn.silu(z_ref[:, cols])).astype(BF16))
    acc_ref[...] += jnp.dot(jnp.concatenate(parts, axis=1), w_ref[...], preferred_element_type=F32)

    @pl.when(k == pl.num_programs(2) - 1)
    def _():
        row0 = pl.program_id(1) * x_ref.shape[0]
        o_ref[...] = _residual(x_ref[...], acc_ref[...], g_ref[...], mods_ref, slot, row0, n_ctx)


def _out_proj_odd(o_f, o_b, p, z_col0, norm_g, w, xs, g, mods, slot, n_ctx):
    B, T, D = xs.shape
    assert T % PO_TM == 0 and DN_V % PO_TK == 0 and z_col0 % PO_TK == 0
    zb = z_col0 // PO_TK
    return pl.pallas_call(
        functools.partial(_out_proj_odd_kernel, slot=slot, n_ctx=n_ctx),
        grid=(B, T // PO_TM, DN_V // PO_TK),
        in_specs=[pl.BlockSpec((None, PO_TM, PO_TK), lambda b, i, k: (b, i, k)),
                  pl.BlockSpec((None, PO_TM, PO_TK), lambda b, i, k: (b, i, k)),
                  pl.BlockSpec((None, PO_TM, PO_TK), lambda b, i, k: (b, i, zb + k)),
                  pl.BlockSpec((1, DN_HEAD_DIM), lambda b, i, k: (0, 0)),
                  pl.BlockSpec((PO_TK, D), lambda b, i, k: (k, 0)),
                  pl.BlockSpec((None, PO_TM, D), lambda b, i, k: (b, i, 0)),
                  pl.BlockSpec((1, D), lambda b, i, k: (0, 0)),
                  pl.BlockSpec((None, 2, N_MOD, D), lambda b, i, k: (b, 0, 0, 0))],
        out_specs=pl.BlockSpec((None, PO_TM, D), lambda b, i, k: (b, i, 0)),
        out_shape=jax.ShapeDtypeStruct((B, T, D), F32),
        scratch_shapes=[pltpu.VMEM((PO_TM, D), F32)],
        compiler_params=pltpu.CompilerParams(dimension_semantics=("arbitrary", "arbitrary", "arbitrary"),
                                             vmem_limit_bytes=VMEM_LIMIT),
    )(o_f, o_b, p, norm_g.astype(F32).reshape(1, DN_HEAD_DIM), w, xs, g.astype(F32).reshape(1, D), mods)


def _post_residual_kernel(x_ref, yc_ref, yl_ref, g_ref, mods_ref, o_ref, *, slot, n_ctx):
    i = pl.program_id(1)
    row0 = i * x_ref.shape[0]
    y = jnp.where(row0 < n_ctx, yc_ref[...], yl_ref[...])
    o_ref[...] = _residual(x_ref[...], y, g_ref[...], mods_ref, slot, row0, n_ctx)


def _post_residual(xs, y_ctx, y_lat, g, mods, slot, n_ctx, latent_only):
    B, T, D = xs.shape
    tm = n_ctx
    assert T % tm == 0
    if y_ctx is None:
        y_ctx = y_lat
    row = pl.BlockSpec((None, tm, D), lambda b, i: (b, i, 0))
    lat = pl.BlockSpec((None, tm, D), lambda b, i: (b, jnp.maximum(i - 1, 0), 0))
    return pl.pallas_call(
        functools.partial(_post_residual_kernel, slot=slot, n_ctx=n_ctx),
        grid=(B, T // tm),
        in_specs=[row, pl.BlockSpec((None, tm, D), lambda b, i: (b, 0, 0)), lat,
                  pl.BlockSpec((1, D), lambda b, i: (0, 0)),
                  pl.BlockSpec((None, 2, N_MOD, D), lambda b, i: (b, 0, 0, 0))],
        out_specs=lat if latent_only else row,
        out_shape=jax.ShapeDtypeStruct((B, T - n_ctx if latent_only else T, D), F32),
        compiler_params=pltpu.CompilerParams(dimension_semantics=("arbitrary", "arbitrary"),
                                             vmem_limit_bytes=VMEM_LIMIT),
    )(xs, y_ctx, y_lat, g.astype(F32).reshape(1, D), mods)


MOE_FT = 256
MOE_MC = 256


def _router_kernel(x_ref, g_ref, mods_ref, rw_ref, h_ref, aff_ref, *, slot, n_ctx):
    row0 = pl.program_id(1) * x_ref.shape[0]
    h = _norm_mod(x_ref[...], g_ref[...], mods_ref, slot, row0, n_ctx)
    h_ref[...] = h
    logits = lax.dot_general(rw_ref[...], h.astype(BF16), (((1,), (1,)), ((), ())), preferred_element_type=F32)
    aff_ref[...] = jax.nn.softmax(logits, axis=0)


def _router(xs, g, mods, router_w, slot, n_ctx):
    B, T, D = xs.shape
    E = router_w.shape[1]
    row = pl.BlockSpec((None, PJ_TM, D), lambda b, i: (b, i, 0))
    return pl.pallas_call(
        functools.partial(_router_kernel, slot=slot, n_ctx=n_ctx),
        grid=(B, T // PJ_TM),
        in_specs=[row, pl.BlockSpec((1, D), lambda b, i: (0, 0)),
                  pl.BlockSpec((None, 2, N_MOD, D), lambda b, i: (b, 0, 0, 0)),
                  pl.BlockSpec((E, D), lambda b, i: (0, 0))],
        out_specs=[row, pl.BlockSpec((None, E, PJ_TM), lambda b, i: (b, 0, i))],
        out_shape=[jax.ShapeDtypeStruct((B, T, D), F32), jax.ShapeDtypeStruct((B, E, T), F32)],
        compiler_params=pltpu.CompilerParams(dimension_semantics=("arbitrary", "arbitrary"),
                                             vmem_limit_bytes=VMEM_LIMIT),
    )(xs, g.astype(F32).reshape(1, D), mods, router_w.T.astype(BF16))


def _expert_ffn_kernel(src_ref, dst_ref, gate_ref, h_ref, wg_ref, wu_ref, wd_ref, acc_in_ref, out_ref,
                       stage_ref, xs_ref, y_ref, sem, *, cap):
    del acc_in_ref
    f = pl.program_id(2)

    def rows_in(table_ref, hbm_ref):
        def issue(s, carry):
            pltpu.make_async_copy(hbm_ref.at[pl.ds(table_ref[0, s], 1), :], stage_ref.at[pl.ds(s, 1), :], sem).start()
            return carry

        lax.fori_loop(0, cap, issue, 0, unroll=8)
        pltpu.make_async_copy(hbm_ref.at[pl.ds(0, cap), :], stage_ref, sem).wait()

    @pl.when(f == 0)
    def _():
        rows_in(src_ref, h_ref)
        xs_ref[...] = stage_ref[...].astype(BF16)
        y_ref[...] = jnp.zeros_like(y_ref)

    wg = wg_ref[...].astype(BF16)
    wu = wu_ref[...].astype(BF16)
    wd = wd_ref[...].astype(BF16)
    mc = min(MOE_MC, cap)
    for r0 in range(0, cap, mc):
        x = xs_ref[r0:r0 + mc, :]
        gt = jnp.dot(x, wg, preferred_element_type=F32)
        up = jnp.dot(x, wu, preferred_element_type=F32)
        hid = (jax.nn.silu(gt) * up).astype(BF16)
        y_ref[r0:r0 + mc, :] += jnp.dot(hid, wd, preferred_element_type=F32)

    @pl.when(f == pl.num_programs(2) - 1)
    def _():
        rows_in(dst_ref, out_ref)
        stage_ref[...] = stage_ref[...] + y_ref[...] * gate_ref[...]

        def put(s, carry):
            pltpu.make_async_copy(stage_ref.at[pl.ds(s, 1), :], out_ref.at[pl.ds(dst_ref[0, s], 1), :], sem).start()
            return carry

        lax.fori_loop(0, cap, put, 0, unroll=8)
        pltpu.make_async_copy(stage_ref, out_ref.at[pl.ds(0, cap), :], sem).wait()


def _expert_ffn(h, src, dst, gate, n_out, w_gate, w_up, w_down, layer):
    R, D = h.shape
    _, E, _, FF = w_gate.shape
    B, _, cap = src.shape
    ft = min(MOE_FT, FF)
    assert FF % ft == 0 and cap <= R and cap <= n_out
    smem = lambda: pl.BlockSpec((None, 1, cap), lambda e, b, f: (b * E + e, 0, 0), memory_space=pltpu.SMEM)
    return pl.pallas_call(
        functools.partial(_expert_ffn_kernel, cap=cap),
        grid=(E, B, FF // ft),
        in_specs=[smem(), smem(),
                  pl.BlockSpec((None, None, cap, 1), lambda e, b, f: (b, e, 0, 0)),
                  pl.BlockSpec(memory_space=pl.ANY),
                  pl.BlockSpec((None, None, D, ft), lambda e, b, f: (layer, e, 0, f)),
                  pl.BlockSpec((None, None, D, ft), lambda e, b, f: (layer, e, 0, f)),
                  pl.BlockSpec((None, None, ft, D), lambda e, b, f: (layer, e, f, 0)),
                  pl.BlockSpec(memory_space=pl.ANY)],
        out_specs=pl.BlockSpec(memory_space=pl.ANY),
        out_shape=jax.ShapeDtypeStruct((n_out, D), F32),
        input_output_aliases={7: 0},
        scratch_shapes=[pltpu.VMEM((cap, D), F32), pltpu.VMEM((cap, D), BF16), pltpu.VMEM((cap, D), F32),
                        pltpu.SemaphoreType.DMA],
        compiler_params=pltpu.CompilerParams(dimension_semantics=("arbitrary", "arbitrary", "arbitrary"),
                                             vmem_limit_bytes=VMEM_LIMIT),
    )(src.reshape(B * E, 1, cap), dst.reshape(B * E, 1, cap), gate.reshape(B, E, cap, 1), h, w_gate, w_up, w_down,
      jnp.zeros((n_out, D), F32))


def _ec_moe(h, aff, row0, n, w_gate, w_up, w_down, layer):
    B, T, D = h.shape
    cap = EC_CAPACITY * n // N_EXPERTS
    gate, idx = lax.top_k(aff[:, :, row0:row0 + n], cap)
    idx = idx.astype(jnp.int32)
    batch = jnp.arange(B, dtype=jnp.int32)[:, None, None]
    out = _expert_ffn(h.reshape(B * T, D), idx + (batch * T + row0), idx + batch * n, gate, B * n,
                      w_gate, w_up, w_down, layer)
    return out.reshape(B, n, D)


def kernel(x, c, ctx, c_ctx, mod_w, mod_b, norm_g, ev_w_in, ev_w_out, hg_lb, hg_norm_g,
           s5_a_re, s5_a_im, s5_log_dt, s5_b_re, s5_b_im, s5_c_re, s5_c_im, s5_d, s5_glu_w, s5_glu_b,
           od_w_in, od_conv_w, od_a_log, od_dt_bias, od_norm_g, od_w_out,
           router_w, moe_w_gate, moe_w_up, moe_w_down):
    B, L, D = ctx.shape
    lb_all = jnp.cumsum(jax.nn.softmax(hg_lb.astype(F32), axis=0), axis=0)
    sc = jax.nn.silu(c)
    scc = jax.nn.silu(c_ctx)
    xs = jnp.concatenate([ctx, x], axis=1)
    for layer in range(DEPTH):
        last = layer == DEPTH - 1
        j = layer // 2
        m_lat = (sc @ mod_w[layer] + mod_b[layer]).reshape(B, N_MOD, D)
        m_ctx = (scc @ mod_w[layer] + mod_b[layer]).reshape(N_MOD, D)
        mods = jnp.stack([jnp.broadcast_to(m_ctx, (B, N_MOD, D)), m_lat], axis=1)
        g_pre_mix, g_post_mix, g_pre_ffn, g_post_ffn = norm_g[layer]
        if layer % 2 == 0:
            p = _in_proj(xs, g_pre_mix, mods, ev_w_in[j].astype(BF16), 0, L)
            o_hg = _hgrn2_mixer(p, L, lb_all[j], hg_norm_g[j])
            s5w = _s5_weights(s5_a_re[j], s5_a_im[j], s5_log_dt[j], s5_b_re[j], s5_b_im[j],
                              s5_c_re[j], s5_c_im[j], s5_d[j], s5_glu_w[j], s5_glu_b[j])
            o_s5 = _s5_mixer(p, 5 * HG_WIDTH, L, s5w)
            xs = _out_proj_even(o_hg, o_s5, ev_w_out[j].astype(BF16), xs, g_post_mix, mods, 2, L)
        else:
            w_in = od_w_in[j].astype(BF16)
            p = _in_proj(xs, g_pre_mix, mods, w_in, 0, L, ncols=DN_QKV + DN_V)
            pg = _in_proj(xs, g_pre_mix, mods, w_in, 0, L, col0=DN_QKV + DN_V, tn=LANES)
            qkv, gates = _dn_inputs(p, pg, L, od_conv_w[j], od_a_log[j], od_dt_bias[j])
            o_f, o_b = _delta_rule(qkv, gates, L)
            xs = _out_proj_odd(o_f, o_b, p, DN_QKV, od_norm_g[j], od_w_out[j].astype(BF16), xs, g_post_mix,
                               mods, 2, L)
        h, aff = _router(xs, g_pre_ffn, mods, router_w[layer], 3, L)
        moe = functools.partial(_ec_moe, h, aff, w_gate=moe_w_gate, w_up=moe_w_up, w_down=moe_w_down, layer=layer)
        y_ctx = None if last else moe(0, L)
        xs = _post_residual(xs, y_ctx, moe(L, xs.shape[1] - L), g_post_ffn, mods, 5, L, latent_only=last)
    return xs
```
